```python
import math
import jax
import jax.numpy as jnp
from jax import lax
import numpy as np


D_MODEL = 1024
BATCH = 8
SEQ = 2048
DEPTH = 1

MIX_WIDTH = D_MODEL
MLA_HEADS = 8
MLA_NOPE_DIM = 64
MLA_ROPE_DIM = 32
MLA_V_DIM = MIX_WIDTH // 2 // MLA_HEADS
MLA_Q_RANK = D_MODEL // 4
MLA_KV_RANK = D_MODEL // 8
ROPE_BASE = 10000.0
DIFF_HEADS = 4
DIFF_HEAD_DIM = MIX_WIDTH // 4 // DIFF_HEADS
DIFF_V_DIM = 2 * DIFF_HEAD_DIM
N_IN = MLA_Q_RANK + MLA_KV_RANK + MLA_ROPE_DIM + 3 * DIFF_HEADS * DIFF_V_DIM
N_EXPERTS = 256
TOP_K = 8
N_GROUPS = 8
TOPK_GROUPS = 4
EXPERT_DIM = D_MODEL // 4
SHARED_DIM = D_MODEL // 4
ROUTED_SCALE = 2.5
MOE_BLOCK = 128
Q_BLOCK = 128
NORM_EPS = 1e-6
SUBLN_EPS = 1e-5
NEG_INF = -1e30
MAX_POS_OFFSET = 1024

kernel_name = 'hybrid_mla_diffattn_moe_block'


def _rms_norm(x, g, eps=NORM_EPS):
    xf = x.astype(jnp.float32)
    y = xf * lax.rsqrt(jnp.mean(xf * xf, axis=-1, keepdims=True) + eps)
    return (y * g.astype(jnp.float32)).astype(x.dtype)


def _rope_cos_sin(positions, dim, dtype):
    inv = 1.0 / (ROPE_BASE ** (jnp.arange(0, dim, 2, dtype=jnp.float32) / dim))
    ang = positions.astype(jnp.float32)[..., None] * inv
    return jnp.cos(ang).astype(dtype), jnp.sin(ang).astype(dtype)


def _rotate(x, cos, sin):
    x1, x2 = jnp.split(x, 2, axis=-1)
    return jnp.concatenate([x1 * cos - x2 * sin, x2 * cos + x1 * sin], axis=-1)


def _causal_mask(q0, seq):
    q_idx = q0 + jnp.arange(Q_BLOCK)
    k_idx = jnp.arange(seq)
    return q_idx[:, None] >= k_idx[None, :]


def _sweep_query_blocks(block_fn, seq):
    out = lax.map(block_fn, jnp.arange(seq // Q_BLOCK))
    nb, b, h, qb, dv = out.shape
    return out.transpose(1, 2, 0, 3, 4).reshape(b, h, nb * qb, dv)


def _mla(cq, ckv, k_rope, positions, q_norm_g, kv_norm_g, w_uq, w_ukv):
    B, S, _ = cq.shape
    q = jnp.einsum('bsr,rhd->bhsd', _rms_norm(cq, q_norm_g), w_uq)
    kv = jnp.einsum('bsr,rhd->bhsd', _rms_norm(ckv, kv_norm_g), w_ukv)
    q_nope, q_rope = q[..., :MLA_NOPE_DIM], q[..., MLA_NOPE_DIM:]
    k_nope, v = kv[..., :MLA_NOPE_DIM], kv[..., MLA_NOPE_DIM:]
    cos, sin = _rope_cos_sin(positions, MLA_ROPE_DIM, cq.dtype)
    q_rope = _rotate(q_rope, cos[:, None], sin[:, None])
    k_rope = _rotate(k_rope, cos, sin)
    scale = 1.0 / math.sqrt(MLA_NOPE_DIM + MLA_ROPE_DIM)

    def block(i):
        q0 = i * Q_BLOCK
        qn = lax.dynamic_slice_in_dim(q_nope, q0, Q_BLOCK, axis=2)
        qr = lax.dynamic_slice_in_dim(q_rope, q0, Q_BLOCK, axis=2)
        s = (jnp.einsum('bhqd,bhkd->bhqk', qn, k_nope)
             + jnp.einsum('bhqd,bkd->bhqk', qr, k_rope)).astype(jnp.float32) * scale
        s = jnp.where(_causal_mask(q0, S), s, NEG_INF)
        p = jax.nn.softmax(s, axis=-1).astype(v.dtype)
        return jnp.einsum('bhqk,bhkd->bhqd', p, v)

    o = _sweep_query_blocks(block, S)
    return o.transpose(0, 2, 1, 3).reshape(B, S, MLA_HEADS * MLA_V_DIM)


def _diff_attention(dq, dk, dv, positions, lam_q1, lam_k1, lam_q2, lam_k2, subln_g, lambda_init):
    B, S, _ = dq.shape
    q = dq.reshape(B, S, DIFF_HEADS, 2, DIFF_HEAD_DIM).transpose(0, 2, 3, 1, 4)
    k = dk.reshape(B, S, DIFF_HEADS, 2, DIFF_HEAD_DIM).transpose(0, 2, 3, 1, 4)
    v = dv.reshape(B, S, DIFF_HEADS, DIFF_V_DIM).transpose(0, 2, 1, 3)
    f32 = jnp.float32
    lam = (jnp.exp(jnp.sum(lam_q1.astype(f32) * lam_k1.astype(f32)))
           - jnp.exp(jnp.sum(lam_q2.astype(f32) * lam_k2.astype(f32))) + lambda_init)
    slopes = jnp.exp2(-8.0 * jnp.arange(1, DIFF_HEADS + 1, dtype=f32) / DIFF_HEADS)
    pos = positions.astype(f32)
    scale = 1.0 / math.sqrt(DIFF_HEAD_DIM)

    def block(i):
        q0 = i * Q_BLOCK
        qi = lax.dynamic_slice_in_dim(q, q0, Q_BLOCK, axis=3)
        s = jnp.einsum('bhcqd,bhckd->bhcqk', qi, k).astype(f32) * scale
        pq = lax.dynamic_slice_in_dim(pos, q0, Q_BLOCK, axis=1)
        dist = jnp.abs(pq[:, None, :, None] - pos[:, None, None, :])
        s = s + (-slopes[None, :, None, None] * dist)[:, :, None]
        s = jnp.where(_causal_mask(q0, S), s, NEG_INF)
        p = jax.nn.softmax(s, axis=-1)
        a = (p[:, :, 0] - lam * p[:, :, 1]).astype(v.dtype)
        return jnp.einsum('bhqk,bhkd->bhqd', a, v)

    o = _sweep_query_blocks(block, S)
    o = _rms_norm(o, subln_g, SUBLN_EPS) * (1.0 - lambda_init)
    return o.transpose(0, 2, 1, 3).reshape(B, S, DIFF_HEADS * DIFF_V_DIM)


def _mixer(h, positions, w_in, q_norm_g, kv_norm_g, w_uq, w_ukv,
           lam_q1, lam_k1, lam_q2, lam_k2, subln_g, w_o, lambda_init):
    proj = h @ w_in
    d3 = DIFF_HEADS * DIFF_V_DIM
    a = MLA_Q_RANK
    b = a + MLA_KV_RANK
    c = b + MLA_ROPE_DIM
    cq, ckv, k_rope, dq, dk, dv = jnp.split(proj, [a, b, c, c + d3, c + 2 * d3], axis=-1)
    mla_out = _mla(cq, ckv, k_rope, positions, q_norm_g, kv_norm_g, w_uq, w_ukv)
    diff_out = _diff_attention(dq, dk, dv, positions, lam_q1, lam_k1, lam_q2, lam_k2,
                               subln_g, lambda_init)
    return jnp.concatenate([mla_out, diff_out], axis=-1) @ w_o


def _moe(h, router_w, router_b, w_gate, w_up, w_down, sw_gate, sw_up, sw_down):
    B, S, D = h.shape
    T = B * S
    A = T * TOP_K
    hf = h.reshape(T, D)
    scores = jax.nn.sigmoid((hf @ router_w).astype(jnp.float32))
    sel = scores + router_b.astype(jnp.float32)
    grouped = sel.reshape(T, N_GROUPS, N_EXPERTS // N_GROUPS)
    group_score = jnp.sum(lax.top_k(grouped, 2)[0], axis=-1)
    _, top_groups = lax.top_k(group_score, TOPK_GROUPS)
    group_mask = jnp.sum(jax.nn.one_hot(top_groups, N_GROUPS), axis=1) > 0
    expert_mask = jnp.repeat(group_mask, N_EXPERTS // N_GROUPS, axis=1)
    _, idx = lax.top_k(jnp.where(expert_mask, sel, -jnp.inf), TOP_K)
    wts = jnp.take_along_axis(scores, idx, axis=1)
    wts = wts / jnp.sum(wts, axis=-1, keepdims=True) * ROUTED_SCALE
    flat_e = idx.reshape(A).astype(jnp.int32)
    flat_tok = jnp.repeat(jnp.arange(T, dtype=jnp.int32), TOP_K)
    flat_w = wts.reshape(A)
    e_sorted, order = lax.sort((flat_e, jnp.arange(A, dtype=jnp.int32)), num_keys=1, is_stable=True)
    counts = jnp.bincount(flat_e, length=N_EXPERTS)
    starts = jnp.cumsum(counts) - counts
    padded = (counts + MOE_BLOCK - 1) // MOE_BLOCK * MOE_BLOCK
    pends = jnp.cumsum(padded)
    pstarts = pends - padded
    dest = pstarts[e_sorted] + (jnp.arange(A, dtype=jnp.int32) - starts[e_sorted])
    P = A + N_EXPERTS * MOE_BLOCK
    n_blocks = P // MOE_BLOCK
    buf_tok = jnp.zeros((P,), jnp.int32).at[dest].set(flat_tok[order])
    buf_w = jnp.zeros((P,), h.dtype).at[dest].set(flat_w[order].astype(h.dtype))
    blk_expert = jnp.minimum(
        jnp.searchsorted(pends, jnp.arange(n_blocks, dtype=jnp.int32) * MOE_BLOCK, side='right'),
        N_EXPERTS - 1)

    def run(bi):
        rows = lax.dynamic_slice_in_dim(buf_tok, bi * MOE_BLOCK, MOE_BLOCK)
        rw = lax.dynamic_slice_in_dim(buf_w, bi * MOE_BLOCK, MOE_BLOCK)
        e = blk_expert[bi]
        xb = hf[rows]
        y = (jax.nn.silu(xb @ w_gate[e]) * (xb @ w_up[e])) @ w_down[e]
        return y * rw[:, None]

    y_buf = lax.map(run, jnp.arange(n_blocks)).reshape(P, D)
    routed = jax.ops.segment_sum(y_buf, buf_tok, num_segments=T)
    shared = (jax.nn.silu(hf @ sw_gate) * (hf @ sw_up)) @ sw_down
    return (routed + shared).reshape(B, S, D)


def setup_inputs(seed: int = 0) -> dict:
    key = jax.random.key(seed)
    ks = iter(jax.random.split(key, 40))
    L, D = DEPTH, D_MODEL

    def nrm(shape, scale):
        return scale * jax.random.normal(next(ks), shape, jnp.float32)

    def gain(shape):
        return 1.0 + nrm(shape, 0.02)

    x = nrm((BATCH, SEQ, D), 1.0)
    c = nrm((BATCH, D), 1.0)
    offs = jax.random.randint(next(ks), (BATCH, 1), 0, MAX_POS_OFFSET, dtype=jnp.int32)
    positions = jnp.arange(SEQ, dtype=jnp.int32)[None, :] + offs
    return {
        'x': x, 'c': c, 'positions': positions,
        'w_ada': nrm((L, D, 6 * D), 0.5 * D ** -0.5),
        'b_ada': nrm((L, 6 * D), 0.02),
        'attn_pre_g': gain((L, D)),
        'attn_post_g': gain((L, D)),
        'w_in': nrm((L, D, N_IN), D ** -0.5),
        'q_norm_g': gain((L, MLA_Q_RANK)),
        'kv_norm_g': gain((L, MLA_KV_RANK)),
        'w_uq': nrm((L, MLA_Q_RANK, MLA_HEADS, MLA_NOPE_DIM + MLA_ROPE_DIM), MLA_Q_RANK ** -0.5),
        'w_ukv': nrm((L, MLA_KV_RANK, MLA_HEADS, MLA_NOPE_DIM + MLA_V_DIM), MLA_KV_RANK ** -0.5),
        'lam_q1': nrm((L, DIFF_HEAD_DIM), 0.1),
        'lam_k1': nrm((L, DIFF_HEAD_DIM), 0.1),
        'lam_q2': nrm((L, DIFF_HEAD_DIM), 0.1),
        'lam_k2': nrm((L, DIFF_HEAD_DIM), 0.1),
        'diff_subln_g': gain((L, DIFF_V_DIM)),
        'w_o': nrm((L, MIX_WIDTH, D), MIX_WIDTH ** -0.5),
        'ffn_pre_g': gain((L, D)),
        'ffn_post_g': gain((L, D)),
        'router_w': nrm((L, D, N_EXPERTS), D ** -0.5),
        'router_b': nrm((L, N_EXPERTS), 0.01),
        'exp_w_gate': nrm((L, N_EXPERTS, D, EXPERT_DIM), D ** -0.5),
        'exp_w_up': nrm((L, N_EXPERTS, D, EXPERT_DIM), D ** -0.5),
        'exp_w_down': nrm((L, N_EXPERTS, EXPERT_DIM, D), EXPERT_DIM ** -0.5),
        'shared_w_gate': nrm((L, D, SHARED_DIM), D ** -0.5),
        'shared_w_up': nrm((L, D, SHARED_DIM), D ** -0.5),
        'shared_w_down': nrm((L, SHARED_DIM, D), SHARED_DIM ** -0.5),
    }


def reference(x, c, positions, w_ada, b_ada, attn_pre_g, attn_post_g, w_in, q_norm_g, kv_norm_g,
              w_uq, w_ukv, lam_q1, lam_k1, lam_q2, lam_k2, diff_subln_g, w_o, ffn_pre_g, ffn_post_g,
              router_w, router_b, exp_w_gate, exp_w_up, exp_w_down,
              shared_w_gate, shared_w_up, shared_w_down):
    for l in range(DEPTH):
        lambda_init = 0.8 - 0.6 * math.exp(-0.3 * l)
        mod = jax.nn.silu(c) @ w_ada[l] + b_ada[l]
        sh_a, sc_a, g_a, sh_f, sc_f, g_f = jnp.split(mod[:, None, :], 6, axis=-1)
        h = _rms_norm(x, attn_pre_g[l]) * (1.0 + sc_a) + sh_a
        y = _mixer(h, positions, w_in[l], q_norm_g[l], kv_norm_g[l], w_uq[l], w_ukv[l],
                   lam_q1[l], lam_k1[l], lam_q2[l], lam_k2[l], diff_subln_g[l], w_o[l], lambda_init)
        x = x + g_a * _rms_norm(y, attn_post_g[l])
        h = _rms_norm(x, ffn_pre_g[l]) * (1.0 + sc_f) + sh_f
        y = _moe(h, router_w[l], router_b[l], exp_w_gate[l], exp_w_up[l], exp_w_down[l],
                 shared_w_gate[l], shared_w_up[l], shared_w_down[l])
        x = x + g_f * _rms_norm(y, ffn_post_g[l])
    return x
```

```python
import functools
import math

import jax
import jax.numpy as jnp
from jax import lax
from jax.experimental import pallas as pl
from jax.experimental.pallas import tpu as pltpu

F32 = jnp.float32
BF16 = jnp.bfloat16
I32 = jnp.int32

MLA_HEADS = 8
MLA_NOPE = 64
MLA_ROPE = 32
MLA_V = 64
MLA_Q_RANK = 256
MLA_KV_RANK = 128
ROPE_BASE = 10000.0
DIFF_HEADS = 4
DIFF_HD = 64
DIFF_V = 128
N_EXPERTS = 256
TOP_K = 8
N_GROUPS = 8
GROUP_SIZE = N_EXPERTS // N_GROUPS
TOPK_GROUPS = 4
ROUTED_SCALE = 2.5
NORM_EPS = 1e-6
SUBLN_EPS = 1e-5
LOG2E = 1.4426950408889634

LANE = 128
HEAD_PAD = 128

TM_PRE = 512
TQ = 256
TM_POST = 512
TM_DISP = 256
TM_COMB = 128
BM = 128
VMEM_LIMIT = 48 * 1024 * 1024


def _cparams(*sem):
    return pltpu.CompilerParams(dimension_semantics=sem, vmem_limit_bytes=VMEM_LIMIT)


def _rms(x, eps):
    return x * lax.rsqrt(jnp.mean(x * x, axis=-1, keepdims=True) + eps)


def _sigmoid(x):
    return 1.0 / (1.0 + jnp.exp(-x))


def _dot(a, b):
    return jnp.dot(a, b, preferred_element_type=F32)


def _dot_nt(a, b):
    return lax.dot_general(a, b, (((1,), (1,)), ((), ())), preferred_element_type=F32)


def _ada_kernel(c_ref, w_ref, b_ref, o_ref):
    c = c_ref[...]
    a = c * _sigmoid(c)
    o_ref[...] = jnp.dot(a, w_ref[...], preferred_element_type=F32,
                         precision=lax.Precision.HIGHEST) + b_ref[...]


def _ada(c, w_ada, b_ada):
    B, D = c.shape
    n = w_ada.shape[1]
    return pl.pallas_call(
        _ada_kernel,
        out_shape=jax.ShapeDtypeStruct((B, n), F32),
        grid=(n // D,),
        in_specs=[pl.BlockSpec((B, D), lambda j: (0, 0)),
                  pl.BlockSpec((D, D), lambda j: (0, j)),
                  pl.BlockSpec((1, D), lambda j: (0, j))],
        out_specs=pl.BlockSpec((B, D), lambda j: (0, j)),
        compiler_params=_cparams("arbitrary"),
        name="ada",
    )(c, w_ada, b_ada.reshape(1, n))


_C_CQ = 0
_C_CKV = _C_CQ + MLA_Q_RANK
_C_KRA = _C_CKV + MLA_KV_RANK
_C_KRB = _C_KRA + LANE
_C_DQ = _C_KRB + LANE
_C_DK = _C_DQ + DIFF_HEADS * DIFF_V
_C_DV = _C_DK + DIFF_HEADS * DIFF_V
_C_END = _C_DV + DIFF_HEADS * DIFF_V


def _pre_attn_kernel(x_ref, mod_ref, pos_ref, inv_ref, g_ref, w1_ref, qg_ref, kvg_ref,
                     wqa_ref, wqb_ref, wkn_ref, wv_ref,
                     q_ref, k_ref, v_ref, dq_ref, dk_ref, dv_ref):
    x = x_ref[...]
    sh = mod_ref[0:1, :]
    sc = mod_ref[1:2, :]
    h = _rms(x, NORM_EPS) * g_ref[...]
    h = h * (1.0 + sc) + sh
    p = _dot(h.astype(BF16), w1_ref[...])

    ang = pos_ref[...] * inv_ref[...]
    lane = lax.broadcasted_iota(I32, ang.shape, 1)
    in_rope = (lane >= MLA_NOPE) & (lane < MLA_NOPE + MLA_ROPE)
    cos_r = jnp.where(in_rope, jnp.cos(ang), 0.0)
    sin_r = jnp.where(in_rope, jnp.sin(ang), 0.0)
    cos_q = jnp.where(lane < MLA_NOPE, 1.0, cos_r)

    cqn = (_rms(p[:, _C_CQ:_C_CKV], NORM_EPS) * qg_ref[...]).astype(BF16)
    qa = _dot(cqn, wqa_ref[...])
    qb = _dot(cqn, wqb_ref[...])
    q_scale = LOG2E / math.sqrt(MLA_NOPE + MLA_ROPE)
    cos_t = jnp.concatenate([cos_q] * MLA_HEADS, axis=1)
    sin_t = jnp.concatenate([sin_r] * MLA_HEADS, axis=1)
    q_ref[...] = ((qa * cos_t + qb * sin_t) * q_scale).astype(BF16)

    ckvn = (_rms(p[:, _C_CKV:_C_KRA], NORM_EPS) * kvg_ref[...]).astype(BF16)
    kn = _dot(ckvn, wkn_ref[...])
    kr = p[:, _C_KRA:_C_KRB] * cos_r + p[:, _C_KRB:_C_DQ] * sin_r
    k_ref[...] = (kn + jnp.concatenate([kr] * MLA_HEADS, axis=1)).astype(BF16)
    v_ref[...] = _dot(ckvn, wv_ref[...]).astype(BF16)

    dq_ref[...] = (p[:, _C_DQ:_C_DK] * (LOG2E / math.sqrt(DIFF_HD))).astype(BF16)
    dk_ref[...] = p[:, _C_DK:_C_DV].astype(BF16)
    dv_ref[...] = p[:, _C_DV:_C_END].astype(BF16)


def _pre_attn(x2, mod3, pos_col, attn_pre_g, w_in, q_norm_g, kv_norm_g, w_uq, w_ukv, S):
    T, D = x2.shape
    tm = min(TM_PRE, S)
    tpb = S // tm
    f = lambda a: a.astype(BF16)
    d3 = DIFF_HEADS * DIFF_V
    a = MLA_Q_RANK
    b = a + MLA_KV_RANK
    c = b + MLA_ROPE

    def swap(r):
        hlf = MLA_ROPE // 2
        return jnp.concatenate([-r[..., hlf:], r[..., :hlf]], axis=-1)

    def pad_rope(r):
        return jnp.pad(r, ((0, 0), (MLA_NOPE, LANE - MLA_NOPE - MLA_ROPE)))

    w_kr = w_in[:, b:c]
    w1 = jnp.concatenate([w_in[:, :b], pad_rope(w_kr), pad_rope(swap(w_kr)), w_in[:, c:]], axis=1)
    assert w1.shape[1] == _C_END
    padq = HEAD_PAD - MLA_NOPE - MLA_ROPE
    wqa = jnp.pad(w_uq, ((0, 0), (0, 0), (0, padq))).reshape(MLA_Q_RANK, MLA_HEADS * HEAD_PAD)
    q_rope = w_uq[..., MLA_NOPE:]
    wqb = jnp.pad(swap(q_rope), ((0, 0), (0, 0), (MLA_NOPE, padq))).reshape(MLA_Q_RANK, MLA_HEADS * HEAD_PAD)
    wkn = jnp.pad(w_ukv[..., :MLA_NOPE], ((0, 0), (0, 0), (0, HEAD_PAD - MLA_NOPE))).reshape(
        MLA_KV_RANK, MLA_HEADS * HEAD_PAD)
    wv = w_ukv[..., MLA_NOPE:].reshape(MLA_KV_RANK, MLA_HEADS * MLA_V)

    inv = 1.0 / (ROPE_BASE ** (jnp.arange(0, MLA_ROPE, 2, dtype=F32) / MLA_ROPE))
    inv_lane = jnp.pad(jnp.concatenate([inv, inv]), (MLA_NOPE, LANE - MLA_NOPE - MLA_ROPE)).reshape(1, LANE)

    full = lambda arr: pl.BlockSpec(arr.shape, lambda i: (0,) * arr.ndim)
    row = lambda w: pl.BlockSpec((tm, w), lambda i: (i, 0))
    ins = [x2, mod3, pos_col, inv_lane, attn_pre_g.reshape(1, D), f(w1), q_norm_g.reshape(1, -1),
           kv_norm_g.reshape(1, -1), f(wqa), f(wqb), f(wkn), f(wv)]
    in_specs = [row(D), pl.BlockSpec((None, 6, D), lambda i: (i // tpb, 0, 0)), row(1)] + \
               [full(arr) for arr in ins[3:]]
    widths = [MLA_HEADS * HEAD_PAD, MLA_HEADS * HEAD_PAD, MLA_HEADS * MLA_V, d3, d3, d3]
    return pl.pallas_call(
        _pre_attn_kernel,
        out_shape=[jax.ShapeDtypeStruct((T, w), BF16) for w in widths],
        grid=(T // tm,),
        in_specs=in_specs,
        out_specs=[row(w) for w in widths],
        compiler_params=_cparams("parallel"),
        name="pre_attn",
    )(*ins)


def _online_update(s, m, l, acc, v):
    m_new = jnp.maximum(m, jnp.max(s, axis=-1, keepdims=True))
    p = jnp.exp2(s - m_new)
    alpha = jnp.exp2(m - m_new)
    l_new = alpha * l + jnp.sum(p, axis=-1, keepdims=True)
    acc_new = alpha * acc + _dot(p.astype(BF16), v)
    return m_new, l_new, acc_new


def _causal_mask(tq):
    r = lax.broadcasted_iota(I32, (tq, tq), 0)
    c = lax.broadcasted_iota(I32, (tq, tq), 1)
    return r >= c


def _mla_attn_kernel(q_ref, k_ref, v_ref, o_ref):
    i = pl.program_id(2)
    tq = q_ref.shape[0]
    outs = []
    for j in range(2):
        q = q_ref[:, j * HEAD_PAD:(j + 1) * HEAD_PAD]
        ksl = slice(j * HEAD_PAD, (j + 1) * HEAD_PAD)
        vsl = slice(j * MLA_V, (j + 1) * MLA_V)

        def step(kb, carry, masked):
            m, l, acc = carry
            r0 = pl.multiple_of(kb * tq, tq)
            s = _dot_nt(q, k_ref[pl.ds(r0, tq), ksl])
            if masked:
                s = jnp.where(_causal_mask(tq), s, -jnp.inf)
            return _online_update(s, m, l, acc, v_ref[pl.ds(r0, tq), vsl])

        init = (jnp.full((tq, 1), -jnp.inf, F32), jnp.zeros((tq, 1), F32), jnp.zeros((tq, MLA_V), F32))
        carry = lax.fori_loop(0, i, lambda kb, c: step(kb, c, False), init)
        m, l, acc = step(i, carry, True)
        outs.append(acc / l)
    o_ref[...] = jnp.concatenate(outs, axis=1).astype(o_ref.dtype)


def _mla_attn(q, k, v, B, S):
    tq = min(TQ, S)
    q3 = q.reshape(B, S, -1)
    k3 = k.reshape(B, S, -1)
    v3 = v.reshape(B, S, -1)
    return pl.pallas_call(
        _mla_attn_kernel,
        out_shape=jax.ShapeDtypeStruct((B, S, MLA_HEADS * MLA_V), BF16),
        grid=(B, MLA_HEADS // 2, S // tq),
        in_specs=[pl.BlockSpec((None, tq, 2 * HEAD_PAD), lambda b, h, i: (b, i, h)),
                  pl.BlockSpec((None, S, 2 * HEAD_PAD), lambda b, h, i: (b, 0, h)),
                  pl.BlockSpec((None, S, 2 * MLA_V), lambda b, h, i: (b, 0, h))],
        out_specs=pl.BlockSpec((None, tq, 2 * MLA_V), lambda b, h, i: (b, i, h)),
        compiler_params=_cparams("parallel", "parallel", "arbitrary"),
        name="mla_attn",
    )(q3, k3, v3)


def _diff_attn_kernel(lambda_init, q_ref, k_ref, v_ref, pc_ref, pr_ref, lam_ref, g_ref, o_ref):
    h = pl.program_id(1)
    i = pl.program_id(2)
    tq = q_ref.shape[0]
    q = q_ref[...]
    lane = lax.broadcasted_iota(I32, q.shape, 1)
    zero = jnp.zeros_like(q)
    q1 = jnp.where(lane < DIFF_HD, q, zero)
    q2 = jnp.where(lane >= DIFF_HD, q, zero)
    hv = jnp.full((1, 1), h, I32).astype(F32)
    nslope = -LOG2E * jnp.exp2(-8.0 * (hv + 1.0) / DIFF_HEADS)
    pq = pc_ref[...]

    def step(kb, carry, masked):
        r0 = pl.multiple_of(kb * tq, tq)
        kblk = k_ref[pl.ds(r0, tq), :]
        vblk = v_ref[pl.ds(r0, tq), :]
        bias = nslope * jnp.abs(pq - pr_ref[:, pl.ds(r0, tq)])
        s1 = _dot_nt(q1, kblk) + bias
        s2 = _dot_nt(q2, kblk) + bias
        if masked:
            cm = _causal_mask(tq)
            s1 = jnp.where(cm, s1, -jnp.inf)
            s2 = jnp.where(cm, s2, -jnp.inf)
        c1, c2 = carry
        return _online_update(s1, *c1, vblk), _online_update(s2, *c2, vblk)

    init1 = (jnp.full((tq, 1), -jnp.inf, F32), jnp.zeros((tq, 1), F32), jnp.zeros((tq, DIFF_V), F32))
    carry = lax.fori_loop(0, i, lambda kb, c: step(kb, c, False), (init1, init1))
    (m1, l1, a1), (m2, l2, a2) = step(i, carry, True)

    lv = lam_ref[...]
    lam = (jnp.exp(jnp.sum(lv[0:1] * lv[1:2], axis=-1, keepdims=True))
           - jnp.exp(jnp.sum(lv[2:3] * lv[3:4], axis=-1, keepdims=True)) + lambda_init)
    o = a1 / l1 - lam * (a2 / l2)
    o = _rms(o, SUBLN_EPS) * g_ref[...] * (1.0 - lambda_init)
    o_ref[...] = o.astype(o_ref.dtype)


def _diff_attn(dq, dk, dv, pos_col, pos_row, lam4, subln_g, lambda_init, B, S):
    tq = min(TQ, S)
    nq = S // tq
    q3 = dq.reshape(B, S, -1)
    k3 = dk.reshape(B, S, -1)
    v3 = dv.reshape(B, S, -1)
    return pl.pallas_call(
        functools.partial(_diff_attn_kernel, lambda_init),
        out_shape=jax.ShapeDtypeStruct((B, S, DIFF_HEADS * DIFF_V), BF16),
        grid=(B, DIFF_HEADS, nq),
        in_specs=[pl.BlockSpec((None, tq, DIFF_V), lambda b, h, i: (b, i, h)),
                  pl.BlockSpec((None, S, DIFF_V), lambda b, h, i: (b, 0, h)),
                  pl.BlockSpec((None, S, DIFF_V), lambda b, h, i: (b, 0, h)),
                  pl.BlockSpec((tq, 1), lambda b, h, i: (b * nq + i, 0)),
                  pl.BlockSpec((None, 1, S), lambda b, h, i: (b, 0, 0)),
                  pl.BlockSpec((4, DIFF_HD), lambda b, h, i: (0, 0)),
                  pl.BlockSpec((1, DIFF_V), lambda b, h, i: (0, 0))],
        out_specs=pl.BlockSpec((None, tq, DIFF_V), lambda b, h, i: (b, i, h)),
        compiler_params=_cparams("parallel", "parallel", "arbitrary"),
        name="diff_attn",
    )(q3, k3, v3, pos_col, pos_row, lam4, subln_g.reshape(1, DIFF_V))


def _first_argmax(v, io, n, axis):
    m = jnp.max(v, axis=axis, keepdims=True)
    ix = jnp.min(jnp.where(v == m, io, n), axis=axis, keepdims=True)
    return m, ix


def _post_attn_kernel(x_ref, mla_ref, dif_ref, mod_ref, woa_ref, wob_ref, pg_ref, fg_ref,
                      rwt_ref, rb_ref,
                      x1_ref, h2_ref, idx_ref, wts_ref, rank_ref, cnt_ref, run_ref):
    step = pl.program_id(0)
    tm = x_ref.shape[0]
    E, G, GS = N_EXPERTS, N_GROUPS, GROUP_SIZE

    @pl.when(step == 0)
    def _():
        run_ref[...] = jnp.zeros_like(run_ref)

    g_a = mod_ref[2:3, :]
    sh_f = mod_ref[3:4, :]
    sc_f = mod_ref[4:5, :]
    y = _dot(mla_ref[...], woa_ref[...]) + _dot(dif_ref[...], wob_ref[...])
    x1 = x_ref[...] + g_a * (_rms(y, NORM_EPS) * pg_ref[...])
    x1_ref[...] = x1
    h2 = (_rms(x1, NORM_EPS) * fg_ref[...]) * (1.0 + sc_f) + sh_f
    h2_ref[...] = h2

    logits = lax.dot_general(rwt_ref[...], h2, (((1,), (1,)), ((), ())),
                             preferred_element_type=F32, precision=lax.Precision.HIGHEST)
    scores = _sigmoid(logits)
    sel = scores + rb_ref[...]

    sio = lax.broadcasted_iota(I32, (GS, tm), 0)
    gs_rows = []
    for g in range(G):
        blk = sel[g * GS:(g + 1) * GS, :]
        m1, i1 = _first_argmax(blk, sio, GS, 0)
        m2 = jnp.max(jnp.where(sio == i1, -jnp.inf, blk), axis=0, keepdims=True)
        gs_rows.append(m1 + m2)
    gs = jnp.concatenate(gs_rows, axis=0)

    gio = lax.broadcasted_iota(I32, (G, tm), 0)
    gkeep = jnp.zeros((G, tm), F32)
    for _ in range(TOPK_GROUPS):
        _, ix = _first_argmax(gs, gio, G, 0)
        pick = gio == ix
        gkeep = jnp.where(pick, 1.0, gkeep)
        gs = jnp.where(pick, -jnp.inf, gs)
    ekeep = jnp.concatenate([jnp.broadcast_to(gkeep[g:g + 1, :], (GS, tm)) for g in range(G)], axis=0)
    cand = jnp.where(ekeep > 0.0, sel, -jnp.inf)

    eio = lax.broadcasted_iota(I32, (E, tm), 0)
    idx_rows, w_rows = [], []
    for _ in range(TOP_K):
        _, ix = _first_argmax(cand, eio, E, 0)
        pick = eio == ix
        w_rows.append(jnp.sum(jnp.where(pick, scores, 0.0), axis=0, keepdims=True))
        cand = jnp.where(pick, -jnp.inf, cand)
        idx_rows.append(ix)
    idx = jnp.concatenate(idx_rows, axis=0)
    w = jnp.concatenate(w_rows, axis=0)
    wts_ref[...] = w / jnp.sum(w, axis=0, keepdims=True) * ROUTED_SCALE
    idx_ref[...] = idx

    onehot = jnp.zeros((E, tm), F32)
    for k in range(TOP_K):
        onehot = onehot + jnp.where(eio == idx_rows[k], 1.0, 0.0)
    tr = lax.broadcasted_iota(I32, (tm, tm), 0)
    tc = lax.broadcasted_iota(I32, (tm, tm), 1)
    before = jnp.where(tr < tc, 1.0, 0.0).astype(BF16)
    prior = _dot(onehot.astype(BF16), before) + run_ref[...]
    rank_rows = [jnp.sum(jnp.where(eio == idx_rows[k], prior, 0.0), axis=0, keepdims=True)
                 for k in range(TOP_K)]
    rank_ref[...] = jnp.concatenate(rank_rows, axis=0).astype(I32)
    run_ref[...] += jnp.sum(onehot, axis=1, keepdims=True)
    cnt_ref[...] = run_ref[...].astype(I32)


def _post_attn(x2, mla, dif, mod3, w_o, attn_post_g, ffn_pre_g, router_w, router_b, S):
    T, D = x2.shape
    tm = min(TM_POST, S)
    tpb = S // tm
    half = MLA_HEADS * MLA_V
    woa = w_o[:half].astype(BF16)
    wob = w_o[half:].astype(BF16)
    rwt = router_w.T
    full = lambda arr: pl.BlockSpec(arr.shape, lambda i: (0,) * arr.ndim)
    row = lambda w: pl.BlockSpec((tm, w), lambda i: (i, 0))
    col = lambda r: pl.BlockSpec((r, tm), lambda i: (0, i))
    ins = [x2, mla.reshape(T, -1), dif.reshape(T, -1), mod3, woa, wob, attn_post_g.reshape(1, D),
           ffn_pre_g.reshape(1, D), rwt, router_b.reshape(N_EXPERTS, 1)]
    in_specs = [row(D), row(half), row(D - half), pl.BlockSpec((None, 6, D), lambda i: (i // tpb, 0, 0))] + \
               [full(arr) for arr in ins[4:]]
    return pl.pallas_call(
        _post_attn_kernel,
        out_shape=[jax.ShapeDtypeStruct((T, D), F32), jax.ShapeDtypeStruct((T, D), F32),
                   jax.ShapeDtypeStruct((TOP_K, T), I32), jax.ShapeDtypeStruct((TOP_K, T), F32),
                   jax.ShapeDtypeStruct((TOP_K, T), I32), jax.ShapeDtypeStruct((N_EXPERTS, 1), I32)],
        grid=(T // tm,),
        in_specs=in_specs,
        out_specs=[row(D), row(D), col(TOP_K), col(TOP_K), col(TOP_K),
                   pl.BlockSpec((N_EXPERTS, 1), lambda i: (0, 0))],
        scratch_shapes=[pltpu.VMEM((N_EXPERTS, 1), F32)],
        compiler_params=_cparams("arbitrary"),
        name="post_attn",
    )(*ins)


def _dest_kernel(idx_ref, rank_ref, ps_ref, o_ref):
    idx = idx_ref[...]
    tm = idx.shape[1]
    eio = lax.broadcasted_iota(I32, (N_EXPERTS, tm), 0)
    ps = ps_ref[...]
    rows = [jnp.sum(jnp.where(eio == idx[k:k + 1, :], ps, 0.0), axis=0, keepdims=True)
            for k in range(TOP_K)]
    o_ref[...] = jnp.concatenate(rows, axis=0).astype(I32) + rank_ref[...]


def _dest(idx, rank, pstart):
    K, T = idx.shape
    tm = min(2048, T)
    col = pl.BlockSpec((K, tm), lambda i: (0, i))
    return pl.pallas_call(
        _dest_kernel,
        out_shape=jax.ShapeDtypeStruct((K, T), I32),
        grid=(T // tm,),
        in_specs=[col, col, pl.BlockSpec((N_EXPERTS, 1), lambda i: (0, 0))],
        out_specs=col,
        compiler_params=_cparams("parallel"),
        name="dest",
    )(idx, rank, pstart.astype(F32).reshape(N_EXPERTS, 1))


def _dispatch_kernel(dest_ref, h_ref, xs_in_ref, xs_ref, sem):
    del xs_in_ref
    tm = h_ref.shape[0]

    def issue(t, _):
        for k in range(TOP_K):
            pltpu.make_async_copy(h_ref.at[pl.ds(t, 1), :],
                                  xs_ref.at[pl.ds(dest_ref[k, t], 1), :], sem).start()
        return 0

    lax.fori_loop(0, tm, issue, 0)
    for _ in range(TOP_K):
        pltpu.make_async_copy(h_ref, xs_ref.at[pl.ds(0, tm), :], sem).wait()


def _dispatch(h2, dest, P):
    T, D = h2.shape
    tm = min(TM_DISP, T)
    nt = T // tm
    dest3 = dest.reshape(TOP_K, nt, tm).transpose(1, 0, 2)
    xs0 = jnp.zeros((P, D), F32)
    return pl.pallas_call(
        _dispatch_kernel,
        out_shape=jax.ShapeDtypeStruct((P, D), F32),
        grid=(nt,),
        in_specs=[pl.BlockSpec((None, TOP_K, tm), lambda i: (i, 0, 0), memory_space=pltpu.SMEM),
                  pl.BlockSpec((tm, D), lambda i: (i, 0)),
                  pl.BlockSpec(memory_space=pl.ANY)],
        out_specs=pl.BlockSpec(memory_space=pl.ANY),
        scratch_shapes=[pltpu.SemaphoreType.DMA(())],
        input_output_aliases={2: 0},
        compiler_params=_cparams("arbitrary"),
        name="dispatch",
    )(dest3, h2, xs0)


def _experts_kernel(be_ref, nu_ref, xs_ref, wg_ref, wu_ref, wd_ref, y_ref, wgb, wub, wdb):
    i = pl.program_id(0)
    first = (i == 0) | (be_ref[i] != be_ref[jnp.maximum(i - 1, 0)])

    @pl.when(first)
    def _():
        wgb[...] = wg_ref[...].astype(BF16)
        wub[...] = wu_ref[...].astype(BF16)
        wdb[...] = wd_ref[...].astype(BF16)

    @pl.when(i < nu_ref[0])
    def _():
        x = xs_ref[...].astype(BF16)
        g = _dot(x, wgb[...])
        u = _dot(x, wub[...])
        a = (g * _sigmoid(g)) * u
        y_ref[...] = _dot(a.astype(BF16), wdb[...])

    @pl.when(i >= nu_ref[0])
    def _():
        y_ref[...] = jnp.zeros_like(y_ref)


def _experts(xs, blk_expert, n_used, w_gate, w_up, w_down):
    P, D = xs.shape
    E, _, F = w_gate.shape
    nb = P // BM
    last = lambda i, be, nu: jnp.minimum(i, nu[0] - 1)
    return pl.pallas_call(
        _experts_kernel,
        out_shape=jax.ShapeDtypeStruct((P, D), F32),
        grid_spec=pltpu.PrefetchScalarGridSpec(
            num_scalar_prefetch=2,
            grid=(nb,),
            in_specs=[pl.BlockSpec((BM, D), lambda i, be, nu: (last(i, be, nu), 0)),
                      pl.BlockSpec((None, D, F), lambda i, be, nu: (be[i], 0, 0)),
                      pl.BlockSpec((None, D, F), lambda i, be, nu: (be[i], 0, 0)),
                      pl.BlockSpec((None, F, D), lambda i, be, nu: (be[i], 0, 0))],
            out_specs=pl.BlockSpec((BM, D), lambda i, be, nu: (i, 0)),
            scratch_shapes=[pltpu.VMEM((D, F), BF16), pltpu.VMEM((D, F), BF16), pltpu.VMEM((F, D), BF16)],
        ),
        compiler_params=_cparams("arbitrary"),
        name="experts",
    )(blk_expert, n_used, xs, w_gate, w_up, w_down)


def _combine_kernel(dcur_ref, dnxt_ref, w_ref, x1_ref, h_ref, mod_ref, pg_ref, sg_ref, su_ref, sd_ref,
                    ys_ref, o_ref, rows, sem):
    i = pl.program_id(0)
    n = pl.num_programs(0)
    tm = x1_ref.shape[0]
    slot = i % 2

    def gather(dref, s):
        def issue(t, _):
            for k in range(TOP_K):
                pltpu.make_async_copy(ys_ref.at[pl.ds(dref[k, t], 1), :],
                                      rows.at[s, k, pl.ds(t, 1), :], sem.at[s]).start()
            return 0
        lax.fori_loop(0, tm, issue, 0)

    @pl.when(i == 0)
    def _():
        gather(dcur_ref, 0)

    @pl.when(i + 1 < n)
    def _():
        gather(dnxt_ref, 1 - slot)

    for k in range(TOP_K):
        pltpu.make_async_copy(ys_ref.at[pl.ds(0, tm), :], rows.at[slot, k], sem.at[slot]).wait()

    w = w_ref[...]
    routed = w[:, 0:1] * rows[slot, 0]
    for k in range(1, TOP_K):
        routed = routed + w[:, k:k + 1] * rows[slot, k]
    hb = h_ref[...].astype(BF16)
    g = _dot(hb, sg_ref[...])
    u = _dot(hb, su_ref[...])
    shared = _dot(((g * _sigmoid(g)) * u).astype(BF16), sd_ref[...])
    y = routed + shared
    g_f = mod_ref[5:6, :]
    o_ref[...] = x1_ref[...] + g_f * (_rms(y, NORM_EPS) * pg_ref[...])


def _combine(dest, wts, x1, h2, mod3, ffn_post_g, sw_gate, sw_up, sw_down, ys, S):
    T, D = x1.shape
    tm = min(TM_COMB, S)
    nt = T // tm
    tpb = S // tm
    dest3 = dest.reshape(TOP_K, nt, tm).transpose(1, 0, 2)
    w_tk = wts.T
    full = lambda arr: pl.BlockSpec(arr.shape, lambda i: (0,) * arr.ndim)
    row = lambda w: pl.BlockSpec((tm, w), lambda i: (i, 0))
    sg, su, sd = sw_gate.astype(BF16), sw_up.astype(BF16), sw_down.astype(BF16)
    pg = ffn_post_g.reshape(1, D)
    return pl.pallas_call(
        _combine_kernel,
        out_shape=jax.ShapeDtypeStruct((T, D), F32),
        grid=(nt,),
        in_specs=[pl.BlockSpec((None, TOP_K, tm), lambda i: (i, 0, 0), memory_space=pltpu.SMEM),
                  pl.BlockSpec((None, TOP_K, tm), lambda i: (jnp.minimum(i + 1, nt - 1), 0, 0),
                               memory_space=pltpu.SMEM),
                  row(TOP_K), row(D), row(D),
                  pl.BlockSpec((None, 6, D), lambda i: (i // tpb, 0, 0)),
                  full(pg), full(sg), full(su), full(sd),
                  pl.BlockSpec(memory_space=pl.ANY)],
        out_specs=row(D),
        scratch_shapes=[pltpu.VMEM((2, TOP_K, tm, D), F32), pltpu.SemaphoreType.DMA((2,))],
        compiler_params=_cparams("arbitrary"),
        name="combine",
    )(dest3, dest3, w_tk, x1, h2, mod3, pg, sg, su, sd, ys)


def _moe(h2, x1, idx, wts, rank, counts, mod3, ffn_post_g, exp_w_gate, exp_w_up, exp_w_down,
         sw_gate, sw_up, sw_down, S):
    T, D = h2.shape
    A = T * TOP_K
    P = A + N_EXPERTS * BM
    nb = P // BM
    cnt = counts.reshape(N_EXPERTS)
    blocks = (cnt + BM - 1) // BM
    bends = jnp.cumsum(blocks).astype(I32)
    pstart = (bends - blocks) * BM
    n_used = bends[-1:]
    blk_expert = jnp.minimum(jnp.searchsorted(bends, jnp.arange(nb, dtype=I32), side='right'),
                             N_EXPERTS - 1).astype(I32)
    last_e = blk_expert[jnp.maximum(n_used[0] - 1, 0)]
    blk_expert = jnp.where(jnp.arange(nb) < n_used[0], blk_expert, last_e)

    dest = _dest(idx, rank, pstart)
    xs = _dispatch(h2, dest, P)
    ys = _experts(xs, blk_expert, n_used, exp_w_gate, exp_w_up, exp_w_down)
    return _combine(dest, wts, x1, h2, mod3, ffn_post_g, sw_gate, sw_up, sw_down, ys, S)


def _layer(x, c, positions, lambda_init, w_ada, b_ada, attn_pre_g, attn_post_g, w_in, q_norm_g, kv_norm_g,
           w_uq, w_ukv, lam_q1, lam_k1, lam_q2, lam_k2, diff_subln_g, w_o, ffn_pre_g, ffn_post_g,
           router_w, router_b, exp_w_gate, exp_w_up, exp_w_down, sw_gate, sw_up, sw_down):
    B, S, D = x.shape
    T = B * S
    x2 = x.reshape(T, D)
    posf = positions.astype(F32)
    pos_col = posf.reshape(T, 1)
    pos_row = posf.reshape(B, 1, S)

    mod3 = _ada(c, w_ada, b_ada).reshape(B, 6, D)
    q, k, v, dq, dk, dv = _pre_attn(x2, mod3, pos_col, attn_pre_g, w_in, q_norm_g, kv_norm_g, w_uq, w_ukv, S)
    mla = _mla_attn(q, k, v, B, S)
    lam4 = jnp.stack([lam_q1, lam_k1, lam_q2, lam_k2])
    dif = _diff_attn(dq, dk, dv, pos_col, pos_row, lam4, diff_subln_g, lambda_init, B, S)
    x1, h2, idx, wts, rank, counts = _post_attn(x2, mla, dif, mod3, w_o, attn_post_g, ffn_pre_g,
                                                router_w, router_b, S)
    out = _moe(h2, x1, idx, wts, rank, counts, mod3, ffn_post_g, exp_w_gate, exp_w_up, exp_w_down,
               sw_gate, sw_up, sw_down, S)
    return out.reshape(B, S, D)


def kernel(x, c, positions, w_ada, b_ada, attn_pre_g, attn_post_g, w_in, q_norm_g, kv_norm_g, w_uq, w_ukv,
           lam_q1, lam_k1, lam_q2, lam_k2, diff_subln_g, w_o, ffn_pre_g, ffn_post_g, router_w, router_b,
           exp_w_gate, exp_w_up, exp_w_down, shared_w_gate, shared_w_up, shared_w_down):
    depth = w_ada.shape[0]
    for l in range(depth):
        lambda_init = 0.8 - 0.6 * math.exp(-0.3 * l)
        x = _layer(x, c, positions, lambda_init, w_ada[l], b_ada[l], attn_pre_g[l], attn_post_g[l], w_in[l],
                   q_norm_g[l], kv_norm_g[l], w_uq[l], w_ukv[l], lam_q1[l], lam_k1[l], lam_q2[l], lam_k2[l],
                   diff_subln_g[l], w_o[l], ffn_pre_g[l], ffn_post_g[l], router_w[l], router_b[l],
                   exp_w_gate[l], exp_w_up[l], exp_w_down[l], shared_w_gate[l], shared_w_up[l],
                   shared_w_down[l])
    return x
```

```python
import functools
import math

import jax
import jax.numpy as jnp
from jax import lax
from jax.experimental import pallas as pl
from jax.experimental.pallas import tpu as pltpu

F32 = jnp.float32
BF16 = jnp.bfloat16
I32 = jnp.int32

MLA_HEADS = 8
MLA_NOPE = 64
MLA_ROPE = 32
MLA_V = 64
MLA_Q_RANK = 256
MLA_KV_RANK = 128
ROPE_BASE = 10000.0
DIFF_HEADS = 4
DIFF_HD = 64
DIFF_V = 128
N_EXPERTS = 256
TOP_K = 8
N_GROUPS = 8
GROUP_SIZE = N_EXPERTS // N_GROUPS
TOPK_GROUPS = 4
ROUTED_SCALE = 2.5
NORM_EPS = 1e-6
SUBLN_EPS = 1e-5
LOG2E = 1.4426950408889634

LANE = 128
HEAD_PAD = 128

TM_PRE = 512
TQ = 512
TK = 256
TM_POST = 512
TM_DISP = 256
TM_COMB = 128
BM = 128
VMEM_LIMIT = 48 * 1024 * 1024


def _cparams(*sem):
    return pltpu.CompilerParams(dimension_semantics=sem, vmem_limit_bytes=VMEM_LIMIT)


def _rms(x, eps):
    return x * lax.rsqrt(jnp.mean(x * x, axis=-1, keepdims=True) + eps)


def _sigmoid(x):
    return 1.0 / (1.0 + jnp.exp(-x))


def _dot(a, b):
    return jnp.dot(a, b, preferred_element_type=F32)


def _dot_nt(a, b):
    return lax.dot_general(a, b, (((1,), (1,)), ((), ())), preferred_element_type=F32)


def _ada_kernel(c_ref, w_ref, b_ref, o_ref):
    c = c_ref[...]
    a = c * _sigmoid(c)
    o_ref[...] = jnp.dot(a, w_ref[...], preferred_element_type=F32,
                         precision=lax.Precision.HIGHEST) + b_ref[...]


def _ada(c, w_ada, b_ada):
    B, D = c.shape
    n = w_ada.shape[1]
    return pl.pallas_call(
        _ada_kernel,
        out_shape=jax.ShapeDtypeStruct((B, n), F32),
        grid=(n // D,),
        in_specs=[pl.BlockSpec((B, D), lambda j: (0, 0)),
                  pl.BlockSpec((D, D), lambda j: (0, j)),
                  pl.BlockSpec((1, D), lambda j: (0, j))],
        out_specs=pl.BlockSpec((B, D), lambda j: (0, j)),
        compiler_params=_cparams("arbitrary"),
        name="ada",
    )(c, w_ada, b_ada.reshape(1, n))


_C_CQ = 0
_C_CKV = _C_CQ + MLA_Q_RANK
_C_KRA = _C_CKV + MLA_KV_RANK
_C_KRB = _C_KRA + LANE
_C_DQ = _C_KRB + LANE
_C_DK = _C_DQ + DIFF_HEADS * DIFF_V
_C_DV = _C_DK + DIFF_HEADS * DIFF_V
_C_END = _C_DV + DIFF_HEADS * DIFF_V


def _pre_attn_kernel(x_ref, mod_ref, pos_ref, inv_ref, g_ref, w1_ref, qg_ref, kvg_ref,
                     wqa_ref, wqb_ref, wkn_ref, wv_ref,
                     q_ref, k_ref, v_ref, dq_ref, dk_ref, dv_ref):
    x = x_ref[...]
    sh = mod_ref[0:1, :]
    sc = mod_ref[1:2, :]
    h = _rms(x, NORM_EPS) * g_ref[...]
    h = h * (1.0 + sc) + sh
    p = _dot(h.astype(BF16), w1_ref[...])

    ang = pos_ref[...] * inv_ref[...]
    lane = lax.broadcasted_iota(I32, ang.shape, 1)
    in_rope = (lane >= MLA_NOPE) & (lane < MLA_NOPE + MLA_ROPE)
    cos_r = jnp.where(in_rope, jnp.cos(ang), 0.0)
    sin_r = jnp.where(in_rope, jnp.sin(ang), 0.0)
    cos_q = jnp.where(lane < MLA_NOPE, 1.0, cos_r)

    cqn = (_rms(p[:, _C_CQ:_C_CKV], NORM_EPS) * qg_ref[...]).astype(BF16)
    qa = _dot(cqn, wqa_ref[...])
    qb = _dot(cqn, wqb_ref[...])
    q_scale = LOG2E / math.sqrt(MLA_NOPE + MLA_ROPE)
    cos_t = jnp.concatenate([cos_q] * MLA_HEADS, axis=1)
    sin_t = jnp.concatenate([sin_r] * MLA_HEADS, axis=1)
    q_ref[...] = ((qa * cos_t + qb * sin_t) * q_scale).astype(BF16)

    ckvn = (_rms(p[:, _C_CKV:_C_KRA], NORM_EPS) * kvg_ref[...]).astype(BF16)
    kn = _dot(ckvn, wkn_ref[...])
    kr = p[:, _C_KRA:_C_KRB] * cos_r + p[:, _C_KRB:_C_DQ] * sin_r
    k_ref[...] = (kn + jnp.concatenate([kr] * MLA_HEADS, axis=1)).astype(BF16)
    lane_t = lax.broadcasted_iota(I32, kn.shape, 1)
    ones_col = jnp.where(lane_t % HEAD_PAD == MLA_V, 1.0, 0.0)
    v_ref[...] = (_dot(ckvn, wv_ref[...]) + ones_col).astype(BF16)

    dq_ref[...] = (p[:, _C_DQ:_C_DK] * (LOG2E / math.sqrt(DIFF_HD))).astype(BF16)
    dk_ref[...] = p[:, _C_DK:_C_DV].astype(BF16)
    dv_ref[...] = p[:, _C_DV:_C_END].astype(BF16)


def _pre_attn(x2, mod3, pos_col, attn_pre_g, w_in, q_norm_g, kv_norm_g, w_uq, w_ukv, S):
    T, D = x2.shape
    tm = min(TM_PRE, S)
    tpb = S // tm
    f = lambda a: a.astype(BF16)
    d3 = DIFF_HEADS * DIFF_V
    a = MLA_Q_RANK
    b = a + MLA_KV_RANK
    c = b + MLA_ROPE

    def swap(r):
        hlf = MLA_ROPE // 2
        return jnp.concatenate([-r[..., hlf:], r[..., :hlf]], axis=-1)

    def pad_rope(r):
        return jnp.pad(r, ((0, 0), (MLA_NOPE, LANE - MLA_NOPE - MLA_ROPE)))

    w_kr = w_in[:, b:c]
    w1 = jnp.concatenate([w_in[:, :b], pad_rope(w_kr), pad_rope(swap(w_kr)), w_in[:, c:]], axis=1)
    assert w1.shape[1] == _C_END
    padq = HEAD_PAD - MLA_NOPE - MLA_ROPE
    wqa = jnp.pad(w_uq, ((0, 0), (0, 0), (0, padq))).reshape(MLA_Q_RANK, MLA_HEADS * HEAD_PAD)
    q_rope = w_uq[..., MLA_NOPE:]
    wqb = jnp.pad(swap(q_rope), ((0, 0), (0, 0), (MLA_NOPE, padq))).reshape(MLA_Q_RANK, MLA_HEADS * HEAD_PAD)
    wkn = jnp.pad(w_ukv[..., :MLA_NOPE], ((0, 0), (0, 0), (0, HEAD_PAD - MLA_NOPE))).reshape(
        MLA_KV_RANK, MLA_HEADS * HEAD_PAD)
    wv = jnp.pad(w_ukv[..., MLA_NOPE:], ((0, 0), (0, 0), (0, HEAD_PAD - MLA_V))).reshape(
        MLA_KV_RANK, MLA_HEADS * HEAD_PAD)

    inv = 1.0 / (ROPE_BASE ** (jnp.arange(0, MLA_ROPE, 2, dtype=F32) / MLA_ROPE))
    inv_lane = jnp.pad(jnp.concatenate([inv, inv]), (MLA_NOPE, LANE - MLA_NOPE - MLA_ROPE)).reshape(1, LANE)

    full = lambda arr: pl.BlockSpec(arr.shape, lambda i: (0,) * arr.ndim)
    row = lambda w: pl.BlockSpec((tm, w), lambda i: (i, 0))
    ins = [x2, mod3, pos_col, inv_lane, attn_pre_g.reshape(1, D), f(w1), q_norm_g.reshape(1, -1),
           kv_norm_g.reshape(1, -1), f(wqa), f(wqb), f(wkn), f(wv)]
    in_specs = [row(D), pl.BlockSpec((None, 6, D), lambda i: (i // tpb, 0, 0)), row(1)] + \
               [full(arr) for arr in ins[3:]]
    widths = [MLA_HEADS * HEAD_PAD, MLA_HEADS * HEAD_PAD, MLA_HEADS * HEAD_PAD, d3, d3, d3]
    return pl.pallas_call(
        _pre_attn_kernel,
        out_shape=[jax.ShapeDtypeStruct((T, w), BF16) for w in widths],
        grid=(T // tm,),
        in_specs=in_specs,
        out_specs=[row(w) for w in widths],
        compiler_params=_cparams("parallel"),
        name="pre_attn",
    )(*ins)


def _online_update(s, m, l, acc, v):
    m_new = jnp.maximum(m, jnp.max(s, axis=-1, keepdims=True))
    p = jnp.exp2(s - m_new)
    alpha = jnp.exp2(m - m_new)
    l_new = alpha * l + jnp.sum(p, axis=-1, keepdims=True)
    acc_new = alpha * acc + _dot(p.astype(BF16), v)
    return m_new, l_new, acc_new


MLA_HPS = 4


def _causal_blocks(i, tq, tk):
    nfull = (i * tq) // tk
    diag = [(nfull + j, i * tq - (nfull + j) * tk) for j in range(max(1, tq // tk))]
    return nfull, diag


def _keep(tq, tk, off):
    r = lax.broadcasted_iota(I32, (tq, tk), 0)
    c = lax.broadcasted_iota(I32, (tq, tk), 1)
    return r + off >= c


def _mla_attn_kernel(tk, q_ref, k_ref, v_ref, o_ref):
    i = pl.program_id(2)
    tq = q_ref.shape[0]
    sl = [slice(j * HEAD_PAD, (j + 1) * HEAD_PAD) for j in range(MLA_HPS)]
    qs = [q_ref[:, sl[j]] for j in range(MLA_HPS)]

    def step(kb, carry, off=None):
        r0 = pl.multiple_of(kb * tk, tk)
        out = []
        for j in range(MLA_HPS):
            m, acc = carry[j]
            s = _dot_nt(qs[j], k_ref[pl.ds(r0, tk), sl[j]])
            if off is not None:
                s = jnp.where(_keep(tq, tk, off), s, -jnp.inf)
            m_new = jnp.maximum(m, jnp.max(s, axis=-1, keepdims=True))
            p = jnp.exp2(s - m_new)
            acc = jnp.exp2(m - m_new) * acc + _dot(p.astype(BF16), v_ref[pl.ds(r0, tk), sl[j]])
            out.append((m_new, acc))
        return tuple(out)

    nfull, diag = _causal_blocks(i, tq, tk)
    init = tuple((jnp.full((tq, 1), -jnp.inf, F32), jnp.zeros((tq, HEAD_PAD), F32)) for _ in range(MLA_HPS))
    carry = lax.fori_loop(0, nfull, step, init)
    for kb, off in diag:
        carry = step(kb, carry, off)
    outs = [acc[:, :MLA_V] / acc[:, MLA_V:MLA_V + 1] for _, acc in carry]
    o_ref[...] = jnp.concatenate(outs, axis=1).astype(o_ref.dtype)


def _mla_attn(q, k, v, B, S):
    tq = min(TQ, S)
    tk = min(TK, S)
    q3 = q.reshape(B, S, -1)
    k3 = k.reshape(B, S, -1)
    v3 = v.reshape(B, S, -1)
    w = MLA_HPS * HEAD_PAD
    return pl.pallas_call(
        functools.partial(_mla_attn_kernel, tk),
        out_shape=jax.ShapeDtypeStruct((B, S, MLA_HEADS * MLA_V), BF16),
        grid=(B, MLA_HEADS // MLA_HPS, S // tq),
        in_specs=[pl.BlockSpec((None, tq, w), lambda b, h, i: (b, i, h)),
                  pl.BlockSpec((None, S, w), lambda b, h, i: (b, 0, h)),
                  pl.BlockSpec((None, S, w), lambda b, h, i: (b, 0, h))],
        out_specs=pl.BlockSpec((None, tq, MLA_HPS * MLA_V), lambda b, h, i: (b, i, h)),
        compiler_params=_cparams("parallel", "parallel", "arbitrary"),
        name="mla_attn",
    )(q3, k3, v3)


DIFF_HPS = 2


def _diff_attn_kernel(lambda_init, tk, q_ref, k_ref, v_ref, pc_ref, pr_ref, lam_ref, g_ref, o_ref):
    i = pl.program_id(2)
    tq = q_ref.shape[0]
    sl = [slice(j * DIFF_V, (j + 1) * DIFF_V) for j in range(DIFF_HPS)]
    lane = lax.broadcasted_iota(I32, (tq, DIFF_V), 1)
    qs, nslopes = [], []
    for j in range(DIFF_HPS):
        q = q_ref[:, sl[j]]
        zero = jnp.zeros_like(q)
        qs.append((jnp.where(lane < DIFF_HD, q, zero), jnp.where(lane >= DIFF_HD, q, zero)))
        hv = jnp.full((1, 1), pl.program_id(1) * DIFF_HPS + j, I32).astype(F32)
        nslopes.append(-LOG2E * jnp.exp2(-8.0 * (hv + 1.0) / DIFF_HEADS))
    pq = pc_ref[...]

    def step(kb, carry, off=None):
        r0 = pl.multiple_of(kb * tk, tk)
        dist = jnp.abs(pq - pr_ref[:, pl.ds(r0, tk)])
        keep = None if off is None else _keep(tq, tk, off)
        out = []
        for j in range(DIFF_HPS):
            kblk = k_ref[pl.ds(r0, tk), sl[j]]
            vblk = v_ref[pl.ds(r0, tk), sl[j]]
            bias = nslopes[j] * dist
            for c in range(2):
                s = _dot_nt(qs[j][c], kblk) + bias
                if keep is not None:
                    s = jnp.where(keep, s, -jnp.inf)
                out.append(_online_update(s, *carry[2 * j + c], vblk))
        return tuple(out)

    nfull, diag = _causal_blocks(i, tq, tk)
    init1 = (jnp.full((tq, 1), -jnp.inf, F32), jnp.zeros((tq, 1), F32), jnp.zeros((tq, DIFF_V), F32))
    carry = lax.fori_loop(0, nfull, step, (init1,) * (2 * DIFF_HPS))
    for kb, off in diag:
        carry = step(kb, carry, off)

    lv = lam_ref[...]
    lam = (jnp.exp(jnp.sum(lv[0:1] * lv[1:2], axis=-1, keepdims=True))
           - jnp.exp(jnp.sum(lv[2:3] * lv[3:4], axis=-1, keepdims=True)) + lambda_init)
    outs = []
    for j in range(DIFF_HPS):
        (_, l1, a1), (_, l2, a2) = carry[2 * j], carry[2 * j + 1]
        o = a1 / l1 - lam * (a2 / l2)
        outs.append(_rms(o, SUBLN_EPS) * g_ref[...] * (1.0 - lambda_init))
    o_ref[...] = jnp.concatenate(outs, axis=1).astype(o_ref.dtype)


def _diff_attn(dq, dk, dv, pos_col, pos_row, lam4, subln_g, lambda_init, B, S):
    tq = min(TQ, S)
    tk = min(TK, S)
    nq = S // tq
    q3 = dq.reshape(B, S, -1)
    k3 = dk.reshape(B, S, -1)
    v3 = dv.reshape(B, S, -1)
    w = DIFF_HPS * DIFF_V
    return pl.pallas_call(
        functools.partial(_diff_attn_kernel, lambda_init, tk),
        out_shape=jax.ShapeDtypeStruct((B, S, DIFF_HEADS * DIFF_V), BF16),
        grid=(B, DIFF_HEADS // DIFF_HPS, nq),
        in_specs=[pl.BlockSpec((None, tq, w), lambda b, h, i: (b, i, h)),
                  pl.BlockSpec((None, S, w), lambda b, h, i: (b, 0, h)),
                  pl.BlockSpec((None, S, w), lambda b, h, i: (b, 0, h)),
                  pl.BlockSpec((tq, 1), lambda b, h, i: (b * nq + i, 0)),
                  pl.BlockSpec((None, 1, S), lambda b, h, i: (b, 0, 0)),
                  pl.BlockSpec((4, DIFF_HD), lambda b, h, i: (0, 0)),
                  pl.BlockSpec((1, DIFF_V), lambda b, h, i: (0, 0))],
        out_specs=pl.BlockSpec((None, tq, w), lambda b, h, i: (b, i, h)),
        compiler_params=_cparams("parallel", "parallel", "arbitrary"),
        name="diff_attn",
    )(q3, k3, v3, pos_col, pos_row, lam4, subln_g.reshape(1, DIFF_V))


def _first_argmax(v, io, n, axis):
    m = jnp.max(v, axis=axis, keepdims=True)
    ix = jnp.min(jnp.where(v == m, io, n), axis=axis, keepdims=True)
    return m, ix


def _post_attn_kernel(x_ref, mla_ref, dif_ref, mod_ref, woa_ref, wob_ref, pg_ref, fg_ref,
                      rwt_ref, rb_ref,
                      x1_ref, h2_ref, idx_ref, wts_ref, rank_ref, cnt_ref, run_ref):
    step = pl.program_id(0)
    tm = x_ref.shape[0]
    E, G, GS = N_EXPERTS, N_GROUPS, GROUP_SIZE

    @pl.when(step == 0)
    def _():
        run_ref[...] = jnp.zeros_like(run_ref)

    g_a = mod_ref[2:3, :]
    sh_f = mod_ref[3:4, :]
    sc_f = mod_ref[4:5, :]
    y = _dot(mla_ref[...], woa_ref[...]) + _dot(dif_ref[...], wob_ref[...])
    x1 = x_ref[...] + g_a * (_rms(y, NORM_EPS) * pg_ref[...])
    x1_ref[...] = x1
    h2 = (_rms(x1, NORM_EPS) * fg_ref[...]) * (1.0 + sc_f) + sh_f
    h2_ref[...] = h2

    logits = lax.dot_general(rwt_ref[...], h2, (((1,), (1,)), ((), ())),
                             preferred_element_type=F32, precision=lax.Precision.HIGHEST)
    scores = _sigmoid(logits)
    sel = scores + rb_ref[...]

    sio = lax.broadcasted_iota(I32, (GS, tm), 0)
    gs_rows = []
    for g in range(G):
        blk = sel[g * GS:(g + 1) * GS, :]
        m1, i1 = _first_argmax(blk, sio, GS, 0)
        m2 = jnp.max(jnp.where(sio == i1, -jnp.inf, blk), axis=0, keepdims=True)
        gs_rows.append(m1 + m2)
    gs = jnp.concatenate(gs_rows, axis=0)

    gio = lax.broadcasted_iota(I32, (G, tm), 0)
    gkeep = jnp.zeros((G, tm), F32)
    for _ in range(TOPK_GROUPS):
        _, ix = _first_argmax(gs, gio, G, 0)
        pick = gio == ix
        gkeep = jnp.where(pick, 1.0, gkeep)
        gs = jnp.where(pick, -jnp.inf, gs)
    ekeep = jnp.concatenate([jnp.broadcast_to(gkeep[g:g + 1, :], (GS, tm)) for g in range(G)], axis=0)
    cand = jnp.where(ekeep > 0.0, sel, -jnp.inf)

    eio = lax.broadcasted_iota(I32, (E, tm), 0)
    idx_rows, w_rows = [], []
    for _ in range(TOP_K):
        _, ix = _first_argmax(cand, eio, E, 0)
        pick = eio == ix
        w_rows.append(jnp.sum(jnp.where(pick, scores, 0.0), axis=0, keepdims=True))
        cand = jnp.where(pick, -jnp.inf, cand)
        idx_rows.append(ix)
    idx = jnp.concatenate(idx_rows, axis=0)
    w = jnp.concatenate(w_rows, axis=0)
    wts_ref[...] = w / jnp.sum(w, axis=0, keepdims=True) * ROUTED_SCALE
    idx_ref[...] = idx

    onehot = jnp.zeros((E, tm), F32)
    for k in range(TOP_K):
        onehot = onehot + jnp.where(eio == idx_rows[k], 1.0, 0.0)
    tr = lax.broadcasted_iota(I32, (tm, tm), 0)
    tc = lax.broadcasted_iota(I32, (tm, tm), 1)
    before = jnp.where(tr < tc, 1.0, 0.0).astype(BF16)
    prior = _dot(onehot.astype(BF16), before) + run_ref[...]
    rank_rows = [jnp.sum(jnp.where(eio == idx_rows[k], prior, 0.0), axis=0, keepdims=True)
                 for k in range(TOP_K)]
    rank_ref[...] = jnp.concatenate(rank_rows, axis=0).astype(I32)
    run_ref[...] += jnp.sum(onehot, axis=1, keepdims=True)
    cnt_ref[...] = run_ref[...].astype(I32)


def _post_attn(x2, mla, dif, mod3, w_o, attn_post_g, ffn_pre_g, router_w, router_b, S):
    T, D = x2.shape
    tm = min(TM_POST, S)
    tpb = S // tm
    half = MLA_HEADS * MLA_V
    woa = w_o[:half].astype(BF16)
    wob = w_o[half:].astype(BF16)
    rwt = router_w.T
    full = lambda arr: pl.BlockSpec(arr.shape, lambda i: (0,) * arr.ndim)
    row = lambda w: pl.BlockSpec((tm, w), lambda i: (i, 0))
    col = lambda r: pl.BlockSpec((r, tm), lambda i: (0, i))
    ins = [x2, mla.reshape(T, -1), dif.reshape(T, -1), mod3, woa, wob, attn_post_g.reshape(1, D),
           ffn_pre_g.reshape(1, D), rwt, router_b.reshape(N_EXPERTS, 1)]
    in_specs = [row(D), row(half), row(D - half), pl.BlockSpec((None, 6, D), lambda i: (i // tpb, 0, 0))] + \
               [full(arr) for arr in ins[4:]]
    return pl.pallas_call(
        _post_attn_kernel,
        out_shape=[jax.ShapeDtypeStruct((T, D), F32), jax.ShapeDtypeStruct((T, D), F32),
                   jax.ShapeDtypeStruct((TOP_K, T), I32), jax.ShapeDtypeStruct((TOP_K, T), F32),
                   jax.ShapeDtypeStruct((TOP_K, T), I32), jax.ShapeDtypeStruct((N_EXPERTS, 1), I32)],
        grid=(T // tm,),
        in_specs=in_specs,
        out_specs=[row(D), row(D), col(TOP_K), col(TOP_K), col(TOP_K),
                   pl.BlockSpec((N_EXPERTS, 1), lambda i: (0, 0))],
        scratch_shapes=[pltpu.VMEM((N_EXPERTS, 1), F32)],
        compiler_params=_cparams("arbitrary"),
        name="post_attn",
    )(*ins)


def _dest_kernel(idx_ref, rank_ref, ps_ref, o_ref):
    idx = idx_ref[...]
    tm = idx.shape[1]
    eio = lax.broadcasted_iota(I32, (N_EXPERTS, tm), 0)
    ps = ps_ref[...]
    rows = [jnp.sum(jnp.where(eio == idx[k:k + 1, :], ps, 0.0), axis=0, keepdims=True)
            for k in range(TOP_K)]
    o_ref[...] = jnp.concatenate(rows, axis=0).astype(I32) + rank_ref[...]


def _dest(idx, rank, pstart):
    K, T = idx.shape
    tm = min(2048, T)
    col = pl.BlockSpec((K, tm), lambda i: (0, i))
    return pl.pallas_call(
        _dest_kernel,
        out_shape=jax.ShapeDtypeStruct((K, T), I32),
        grid=(T // tm,),
        in_specs=[col, col, pl.BlockSpec((N_EXPERTS, 1), lambda i: (0, 0))],
        out_specs=col,
        compiler_params=_cparams("parallel"),
        name="dest",
    )(idx, rank, pstart.astype(F32).reshape(N_EXPERTS, 1))


def _dispatch_kernel(bs_ref, cnt_ref, dest_ref, h_ref, xs_ref, zeros, sem, zsem):
    tm = h_ref.shape[0]
    nb = xs_ref.shape[0] // BM

    @pl.when(pl.program_id(0) == 0)
    def _():
        zeros[...] = jnp.zeros_like(zeros)

        def zcopy(g):
            return pltpu.make_async_copy(zeros, xs_ref.at[pl.ds(pl.multiple_of(g * BM, BM), BM), :], zsem)

        def has_pad(e):
            return cnt_ref[e] % BM != 0

        def fill(e, _):
            @pl.when(has_pad(e))
            def _():
                zcopy(bs_ref[e + 1] - 1).start()
            return 0

        def drain(e, _):
            @pl.when(has_pad(e))
            def _():
                zcopy(0).wait()
            return 0

        lax.fori_loop(0, N_EXPERTS, fill, 0)
        lax.fori_loop(bs_ref[N_EXPERTS], nb, lambda g, _: (zcopy(g).start(), 0)[1], 0)
        lax.fori_loop(0, N_EXPERTS, drain, 0)
        lax.fori_loop(bs_ref[N_EXPERTS], nb, lambda g, _: (zcopy(0).wait(), 0)[1], 0)

    def issue(t, _):
        for k in range(TOP_K):
            pltpu.make_async_copy(h_ref.at[pl.ds(t, 1), :],
                                  xs_ref.at[pl.ds(dest_ref[k, t], 1), :], sem).start()
        return 0

    lax.fori_loop(0, tm, issue, 0)
    for _ in range(TOP_K):
        pltpu.make_async_copy(h_ref, xs_ref.at[pl.ds(0, tm), :], sem).wait()


def _dispatch(h2, dest, bstart, cnt, P):
    T, D = h2.shape
    tm = min(TM_DISP, T)
    nt = T // tm
    dest3 = dest.reshape(TOP_K, nt, tm).transpose(1, 0, 2)
    return pl.pallas_call(
        _dispatch_kernel,
        out_shape=jax.ShapeDtypeStruct((P, D), F32),
        grid_spec=pltpu.PrefetchScalarGridSpec(
            num_scalar_prefetch=2,
            grid=(nt,),
            in_specs=[pl.BlockSpec((None, TOP_K, tm), lambda i, bs, cn: (i, 0, 0), memory_space=pltpu.SMEM),
                      pl.BlockSpec((tm, D), lambda i, bs, cn: (i, 0))],
            out_specs=pl.BlockSpec(memory_space=pl.ANY),
            scratch_shapes=[pltpu.VMEM((BM, D), F32), pltpu.SemaphoreType.DMA(()),
                            pltpu.SemaphoreType.DMA(())],
        ),
        compiler_params=_cparams("arbitrary"),
        name="dispatch",
    )(bstart, cnt, dest3, h2)


NBUF_X = 4
NBUF_Y = 3


def _experts_kernel(bs_ref, wg_ref, wu_ref, wd_ref, xs_ref, ys_ref, wgb, wub, wdb, xbuf, ybuf, xsem, ysem):
    e = pl.program_id(0)
    nb = xs_ref.shape[0] // BM
    nused = bs_ref[N_EXPERTS]
    g0 = bs_ref[e]
    g1 = bs_ref[e + 1]

    def rows(g):
        return pl.ds(pl.multiple_of(g * BM, BM), BM)

    def x_copy(g, slot):
        return pltpu.make_async_copy(xs_ref.at[rows(g), :], xbuf.at[slot], xsem.at[slot])

    def y_copy(g, slot):
        return pltpu.make_async_copy(ybuf.at[slot], ys_ref.at[rows(g), :], ysem.at[slot])

    @pl.when(e == 0)
    def _():
        for j in range(NBUF_X - 1):
            @pl.when(j < nused)
            def _():
                x_copy(j, j).start()

    @pl.when(g1 > g0)
    def _():
        wgb[...] = wg_ref[...].astype(BF16)
        wub[...] = wu_ref[...].astype(BF16)
        wdb[...] = wd_ref[...].astype(BF16)

        def body(g, _):
            slot = g % NBUF_X
            x_copy(g, slot).wait()
            nxt = g + NBUF_X - 1

            @pl.when(nxt < nused)
            def _():
                x_copy(nxt, nxt % NBUF_X).start()

            x = xbuf[slot].astype(BF16)
            gate = _dot(x, wgb[...])
            up = _dot(x, wub[...])
            a = (gate * _sigmoid(gate)) * up
            y = _dot(a.astype(BF16), wdb[...])
            ys = g % NBUF_Y

            @pl.when(g >= NBUF_Y)
            def _():
                y_copy(g - NBUF_Y, ys).wait()

            ybuf[ys] = y
            y_copy(g, ys).start()
            return 0

        lax.fori_loop(g0, g1, body, 0)

    @pl.when(e == N_EXPERTS - 1)
    def _():
        for j in range(NBUF_Y):
            @pl.when(nused - 1 - j >= 0)
            def _():
                y_copy(0, (nused - 1 - j) % NBUF_Y).wait()
        ybuf[0] = jnp.zeros((BM, ybuf.shape[2]), F32)
        lax.fori_loop(nused, nb, lambda g, _: (y_copy(g, 0).start(), 0)[1], 0)
        lax.fori_loop(nused, nb, lambda g, _: (y_copy(0, 0).wait(), 0)[1], 0)


def _experts(xs, bstart, w_gate, w_up, w_down):
    P, D = xs.shape
    E, _, F = w_gate.shape
    return pl.pallas_call(
        _experts_kernel,
        out_shape=jax.ShapeDtypeStruct((P, D), F32),
        grid_spec=pltpu.PrefetchScalarGridSpec(
            num_scalar_prefetch=1,
            grid=(E,),
            in_specs=[pl.BlockSpec((None, D, F), lambda e, bs: (e, 0, 0)),
                      pl.BlockSpec((None, D, F), lambda e, bs: (e, 0, 0)),
                      pl.BlockSpec((None, F, D), lambda e, bs: (e, 0, 0)),
                      pl.BlockSpec(memory_space=pl.ANY)],
            out_specs=pl.BlockSpec(memory_space=pl.ANY),
            scratch_shapes=[pltpu.VMEM((D, F), BF16), pltpu.VMEM((D, F), BF16), pltpu.VMEM((F, D), BF16),
                            pltpu.VMEM((NBUF_X, BM, D), F32), pltpu.VMEM((NBUF_Y, BM, D), F32),
                            pltpu.SemaphoreType.DMA((NBUF_X,)), pltpu.SemaphoreType.DMA((NBUF_Y,))],
        ),
        compiler_params=_cparams("arbitrary"),
        name="experts",
    )(bstart, w_gate, w_up, w_down, xs)


def _combine_kernel(dcur_ref, dnxt_ref, w_ref, x1_ref, h_ref, mod_ref, pg_ref, sg_ref, su_ref, sd_ref,
                    ys_ref, o_ref, rows, sem):
    i = pl.program_id(0)
    n = pl.num_programs(0)
    tm = x1_ref.shape[0]
    slot = i % 2

    def gather(dref, s):
        def issue(t, _):
            for k in range(TOP_K):
                pltpu.make_async_copy(ys_ref.at[pl.ds(dref[k, t], 1), :],
                                      rows.at[s, k, pl.ds(t, 1), :], sem.at[s]).start()
            return 0
        lax.fori_loop(0, tm, issue, 0)

    @pl.when(i == 0)
    def _():
        gather(dcur_ref, 0)

    @pl.when(i + 1 < n)
    def _():
        gather(dnxt_ref, 1 - slot)

    for k in range(TOP_K):
        pltpu.make_async_copy(ys_ref.at[pl.ds(0, tm), :], rows.at[slot, k], sem.at[slot]).wait()

    w = w_ref[...]
    routed = w[:, 0:1] * rows[slot, 0]
    for k in range(1, TOP_K):
        routed = routed + w[:, k:k + 1] * rows[slot, k]
    hb = h_ref[...].astype(BF16)
    g = _dot(hb, sg_ref[...])
    u = _dot(hb, su_ref[...])
    shared = _dot(((g * _sigmoid(g)) * u).astype(BF16), sd_ref[...])
    y = routed + shared
    g_f = mod_ref[5:6, :]
    o_ref[...] = x1_ref[...] + g_f * (_rms(y, NORM_EPS) * pg_ref[...])


def _combine(dest, wts, x1, h2, mod3, ffn_post_g, sw_gate, sw_up, sw_down, ys, S):
    T, D = x1.shape
    tm = min(TM_COMB, S)
    nt = T // tm
    tpb = S // tm
    dest3 = dest.reshape(TOP_K, nt, tm).transpose(1, 0, 2)
    w_tk = wts.T
    full = lambda arr: pl.BlockSpec(arr.shape, lambda i: (0,) * arr.ndim)
    row = lambda w: pl.BlockSpec((tm, w), lambda i: (i, 0))
    sg, su, sd = sw_gate.astype(BF16), sw_up.astype(BF16), sw_down.astype(BF16)
    pg = ffn_post_g.reshape(1, D)
    return pl.pallas_call(
        _combine_kernel,
        out_shape=jax.ShapeDtypeStruct((T, D), F32),
        grid=(nt,),
        in_specs=[pl.BlockSpec((None, TOP_K, tm), lambda i: (i, 0, 0), memory_space=pltpu.SMEM),
                  pl.BlockSpec((None, TOP_K, tm), lambda i: (jnp.minimum(i + 1, nt - 1), 0, 0),
                               memory_space=pltpu.SMEM),
                  row(TOP_K), row(D), row(D),
                  pl.BlockSpec((None, 6, D), lambda i: (i // tpb, 0, 0)),
                  full(pg), full(sg), full(su), full(sd),
                  pl.BlockSpec(memory_space=pl.ANY)],
        out_specs=row(D),
        scratch_shapes=[pltpu.VMEM((2, TOP_K, tm, D), F32), pltpu.SemaphoreType.DMA((2,))],
        compiler_params=_cparams("arbitrary"),
        name="combine",
    )(dest3, dest3, w_tk, x1, h2, mod3, pg, sg, su, sd, ys)


def _moe(h2, x1, idx, wts, rank, counts, mod3, ffn_post_g, exp_w_gate, exp_w_up, exp_w_down,
         sw_gate, sw_up, sw_down, S):
    T, D = h2.shape
    A = T * TOP_K
    P = A + N_EXPERTS * BM
    cnt = counts.reshape(N_EXPERTS)
    blocks = (cnt + BM - 1) // BM
    bstart = jnp.concatenate([jnp.zeros((1,), I32), jnp.cumsum(blocks).astype(I32)])
    pstart = bstart[:-1] * BM

    dest = _dest(idx, rank, pstart)
    xs = _dispatch(h2, dest, bstart, cnt, P)
    ys = _experts(xs, bstart, exp_w_gate, exp_w_up, exp_w_down)
    return _combine(dest, wts, x1, h2, mod3, ffn_post_g, sw_gate, sw_up, sw_down, ys, S)


def _layer(x, c, positions, lambda_init, w_ada, b_ada, attn_pre_g, attn_post_g, w_in, q_norm_g, kv_norm_g,
           w_uq, w_ukv, lam_q1, lam_k1, lam_q2, lam_k2, diff_subln_g, w_o, ffn_pre_g, ffn_post_g,
           router_w, router_b, exp_w_gate, exp_w_up, exp_w_down, sw_gate, sw_up, sw_down):
    B, S, D = x.shape
    T = B * S
    x2 = x.reshape(T, D)
    posf = positions.astype(F32)
    pos_col = posf.reshape(T, 1)
    pos_row = posf.reshape(B, 1, S)

    mod3 = _ada(c, w_ada, b_ada).reshape(B, 6, D)
    q, k, v, dq, dk, dv = _pre_attn(x2, mod3, pos_col, attn_pre_g, w_in, q_norm_g, kv_norm_g, w_uq, w_ukv, S)
    mla = _mla_attn(q, k, v, B, S)
    lam4 = jnp.stack([lam_q1, lam_k1, lam_q2, lam_k2])
    dif = _diff_attn(dq, dk, dv, pos_col, pos_row, lam4, diff_subln_g, lambda_init, B, S)
    x1, h2, idx, wts, rank, counts = _post_attn(x2, mla, dif, mod3, w_o, attn_post_g, ffn_pre_g,
                                                router_w, router_b, S)
    out = _moe(h2, x1, idx, wts, rank, counts, mod3, ffn_post_g, exp_w_gate, exp_w_up, exp_w_down,
               sw_gate, sw_up, sw_down, S)
    return out.reshape(B, S, D)


def kernel(x, c, positions, w_ada, b_ada, attn_pre_g, attn_post_g, w_in, q_norm_g, kv_norm_g, w_uq, w_ukv,
           lam_q1, lam_k1, lam_q2, lam_k2, diff_subln_g, w_o, ffn_pre_g, ffn_post_g, router_w, router_b,
           exp_w_gate, exp_w_up, exp_w_down, shared_w_gate, shared_w_up, shared_w_down):
    depth = w_ada.shape[0]
    for l in range(depth):
        lambda_init = 0.8 - 0.6 * math.exp(-0.3 * l)
        x = _layer(x, c, positions, lambda_init, w_ada[l], b_ada[l], attn_pre_g[l], attn_post_g[l], w_in[l],
                   q_norm_g[l], kv_norm_g[l], w_uq[l], w_ukv[l], lam_q1[l], lam_k1[l], lam_q2[l], lam_k2[l],
                   diff_subln_g[l], w_o[l], ffn_pre_g[l], ffn_post_g[l], router_w[l], router_b[l],
                   exp_w_gate[l], exp_w_up[l], exp_w_down[l], shared_w_gate[l], shared_w_up[l],
                   shared_w_down[l])
    return x
```

```python
import functools
import math

import jax
import jax.numpy as jnp
from jax import lax
from jax.experimental import pallas as pl
from jax.experimental.pallas import tpu as pltpu

F32 = jnp.float32
BF16 = jnp.bfloat16
I32 = jnp.int32

MLA_HEADS = 8
MLA_NOPE = 64
MLA_ROPE = 32
MLA_V = 64
MLA_Q_RANK = 256
MLA_KV_RANK = 128
ROPE_BASE = 10000.0
DIFF_HEADS = 4
DIFF_HD = 64
DIFF_V = 128
N_EXPERTS = 256
TOP_K = 8
N_GROUPS = 8
GROUP_SIZE = N_EXPERTS // N_GROUPS
TOPK_GROUPS = 4
ROUTED_SCALE = 2.5
NORM_EPS = 1e-6
SUBLN_EPS = 1e-5
LOG2E = 1.4426950408889634

LANE = 128
HEAD_PAD = 128

TM_PRE = 512
TQ = 512
TK = 256
TM_POST = 512
TM_DISP = 256
TM_COMB = 128
BM = 128
VMEM_LIMIT = 48 * 1024 * 1024


def _cparams(*sem):
    return pltpu.CompilerParams(dimension_semantics=sem, vmem_limit_bytes=VMEM_LIMIT)


def _rms(x, eps):
    return x * lax.rsqrt(jnp.mean(x * x, axis=-1, keepdims=True) + eps)


def _sigmoid(x):
    return 1.0 / (1.0 + jnp.exp(-x))


def _dot(a, b):
    return jnp.dot(a, b, preferred_element_type=F32)


RT = 8


def _rt_load(ref, n, *lead):
    return jnp.concatenate([ref[(*lead, pl.ds(j, n, stride=RT), slice(None))] for j in range(RT)], axis=1)


def _rt_store(ref, val, *lead):
    n = val.shape[0]
    for j in range(RT):
        ref[(*lead, pl.ds(j, n, stride=RT), slice(None))] = val[:, j * LANE:(j + 1) * LANE]


def _dot_nt(a, b):
    return lax.dot_general(a, b, (((1,), (1,)), ((), ())), preferred_element_type=F32)


def _ada_kernel(c_ref, w_ref, b_ref, o_ref):
    c = c_ref[...]
    a = c * _sigmoid(c)
    o_ref[...] = jnp.dot(a, w_ref[...], preferred_element_type=F32,
                         precision=lax.Precision.HIGHEST) + b_ref[...]


def _ada(c, w_ada, b_ada):
    B, D = c.shape
    n = w_ada.shape[1]
    return pl.pallas_call(
        _ada_kernel,
        out_shape=jax.ShapeDtypeStruct((B, n), F32),
        grid=(n // D,),
        in_specs=[pl.BlockSpec((B, D), lambda j: (0, 0)),
                  pl.BlockSpec((D, D), lambda j: (0, j)),
                  pl.BlockSpec((1, D), lambda j: (0, j))],
        out_specs=pl.BlockSpec((B, D), lambda j: (0, j)),
        compiler_params=_cparams("arbitrary"),
        name="ada",
    )(c, w_ada, b_ada.reshape(1, n))


_C_CQ = 0
_C_CKV = _C_CQ + MLA_Q_RANK
_C_KRA = _C_CKV + MLA_KV_RANK
_C_KRB = _C_KRA + LANE
_C_DQ = _C_KRB + LANE
_C_DK = _C_DQ + DIFF_HEADS * DIFF_V
_C_DV = _C_DK + DIFF_HEADS * DIFF_V
_C_END = _C_DV + DIFF_HEADS * DIFF_V


def _pre_attn_kernel(x_ref, mod_ref, pos_ref, inv_ref, g_ref, w1_ref, qg_ref, kvg_ref,
                     wqa_ref, wqb_ref, wkn_ref, wv_ref,
                     q_ref, k_ref, v_ref, dq_ref, dk_ref, dv_ref):
    x = x_ref[...]
    sh = mod_ref[0:1, :]
    sc = mod_ref[1:2, :]
    h = _rms(x, NORM_EPS) * g_ref[...]
    h = h * (1.0 + sc) + sh
    p = _dot(h.astype(BF16), w1_ref[...])

    ang = pos_ref[...] * inv_ref[...]
    lane = lax.broadcasted_iota(I32, ang.shape, 1)
    in_rope = (lane >= MLA_NOPE) & (lane < MLA_NOPE + MLA_ROPE)
    cos_r = jnp.where(in_rope, jnp.cos(ang), 0.0)
    sin_r = jnp.where(in_rope, jnp.sin(ang), 0.0)
    cos_q = jnp.where(lane < MLA_NOPE, 1.0, cos_r)

    cqn = (_rms(p[:, _C_CQ:_C_CKV], NORM_EPS) * qg_ref[...]).astype(BF16)
    qa = _dot(cqn, wqa_ref[...])
    qb = _dot(cqn, wqb_ref[...])
    q_scale = LOG2E / math.sqrt(MLA_NOPE + MLA_ROPE)
    cos_t = jnp.concatenate([cos_q] * MLA_HEADS, axis=1)
    sin_t = jnp.concatenate([sin_r] * MLA_HEADS, axis=1)
    q_ref[...] = ((qa * cos_t + qb * sin_t) * q_scale).astype(BF16)

    ckvn = (_rms(p[:, _C_CKV:_C_KRA], NORM_EPS) * kvg_ref[...]).astype(BF16)
    kn = _dot(ckvn, wkn_ref[...])
    kr = p[:, _C_KRA:_C_KRB] * cos_r + p[:, _C_KRB:_C_DQ] * sin_r
    k_ref[...] = (kn + jnp.concatenate([kr] * MLA_HEADS, axis=1)).astype(BF16)
    lane_t = lax.broadcasted_iota(I32, kn.shape, 1)
    ones_col = jnp.where(lane_t % HEAD_PAD == MLA_V, 1.0, 0.0)
    v_ref[...] = (_dot(ckvn, wv_ref[...]) + ones_col).astype(BF16)

    dq_ref[...] = (p[:, _C_DQ:_C_DK] * (LOG2E / math.sqrt(DIFF_HD))).astype(BF16)
    dk_ref[...] = p[:, _C_DK:_C_DV].astype(BF16)
    dv_ref[...] = p[:, _C_DV:_C_END].astype(BF16)


def _pre_attn(x2, mod3, pos_col, attn_pre_g, w_in, q_norm_g, kv_norm_g, w_uq, w_ukv, S):
    T, D = x2.shape
    tm = min(TM_PRE, S)
    tpb = S // tm
    f = lambda a: a.astype(BF16)
    d3 = DIFF_HEADS * DIFF_V
    a = MLA_Q_RANK
    b = a + MLA_KV_RANK
    c = b + MLA_ROPE

    def swap(r):
        hlf = MLA_ROPE // 2
        return jnp.concatenate([-r[..., hlf:], r[..., :hlf]], axis=-1)

    def pad_rope(r):
        return jnp.pad(r, ((0, 0), (MLA_NOPE, LANE - MLA_NOPE - MLA_ROPE)))

    w_kr = w_in[:, b:c]
    w1 = jnp.concatenate([w_in[:, :b], pad_rope(w_kr), pad_rope(swap(w_kr)), w_in[:, c:]], axis=1)
    assert w1.shape[1] == _C_END
    padq = HEAD_PAD - MLA_NOPE - MLA_ROPE
    wqa = jnp.pad(w_uq, ((0, 0), (0, 0), (0, padq))).reshape(MLA_Q_RANK, MLA_HEADS * HEAD_PAD)
    q_rope = w_uq[..., MLA_NOPE:]
    wqb = jnp.pad(swap(q_rope), ((0, 0), (0, 0), (MLA_NOPE, padq))).reshape(MLA_Q_RANK, MLA_HEADS * HEAD_PAD)
    wkn = jnp.pad(w_ukv[..., :MLA_NOPE], ((0, 0), (0, 0), (0, HEAD_PAD - MLA_NOPE))).reshape(
        MLA_KV_RANK, MLA_HEADS * HEAD_PAD)
    wv = jnp.pad(w_ukv[..., MLA_NOPE:], ((0, 0), (0, 0), (0, HEAD_PAD - MLA_V))).reshape(
        MLA_KV_RANK, MLA_HEADS * HEAD_PAD)

    inv = 1.0 / (ROPE_BASE ** (jnp.arange(0, MLA_ROPE, 2, dtype=F32) / MLA_ROPE))
    inv_lane = jnp.pad(jnp.concatenate([inv, inv]), (MLA_NOPE, LANE - MLA_NOPE - MLA_ROPE)).reshape(1, LANE)

    full = lambda arr: pl.BlockSpec(arr.shape, lambda i: (0,) * arr.ndim)
    row = lambda w: pl.BlockSpec((tm, w), lambda i: (i, 0))
    ins = [x2, mod3, pos_col, inv_lane, attn_pre_g.reshape(1, D), f(w1), q_norm_g.reshape(1, -1),
           kv_norm_g.reshape(1, -1), f(wqa), f(wqb), f(wkn), f(wv)]
    in_specs = [row(D), pl.BlockSpec((None, 6, D), lambda i: (i // tpb, 0, 0)), row(1)] + \
               [full(arr) for arr in ins[3:]]
    widths = [MLA_HEADS * HEAD_PAD, MLA_HEADS * HEAD_PAD, MLA_HEADS * HEAD_PAD, d3, d3, d3]
    return pl.pallas_call(
        _pre_attn_kernel,
        out_shape=[jax.ShapeDtypeStruct((T, w), BF16) for w in widths],
        grid=(T // tm,),
        in_specs=in_specs,
        out_specs=[row(w) for w in widths],
        compiler_params=_cparams("parallel"),
        name="pre_attn",
    )(*ins)


def _online_update(s, m, l, acc, v):
    m_new = jnp.maximum(m, jnp.max(s, axis=-1, keepdims=True))
    p = jnp.exp2(s - m_new)
    alpha = jnp.exp2(m - m_new)
    l_new = alpha * l + jnp.sum(p, axis=-1, keepdims=True)
    acc_new = alpha * acc + _dot(p.astype(BF16), v)
    return m_new, l_new, acc_new


MLA_HPS = 4


def _causal_blocks(i, tq, tk):
    nfull = (i * tq) // tk
    diag = [(nfull + j, i * tq - (nfull + j) * tk) for j in range(max(1, tq // tk))]
    return nfull, diag


def _keep(tq, tk, off):
    r = lax.broadcasted_iota(I32, (tq, tk), 0)
    c = lax.broadcasted_iota(I32, (tq, tk), 1)
    return r + off >= c


def _mla_attn_kernel(tk, q_ref, k_ref, v_ref, o_ref):
    i = pl.program_id(2)
    tq = q_ref.shape[0]
    sl = [slice(j * HEAD_PAD, (j + 1) * HEAD_PAD) for j in range(MLA_HPS)]
    qs = [q_ref[:, sl[j]] for j in range(MLA_HPS)]

    def step(kb, carry, off=None):
        r0 = pl.multiple_of(kb * tk, tk)
        out = []
        for j in range(MLA_HPS):
            m, acc = carry[j]
            s = _dot_nt(qs[j], k_ref[pl.ds(r0, tk), sl[j]])
            if off is not None:
                s = jnp.where(_keep(tq, tk, off), s, -jnp.inf)
            m_new = jnp.maximum(m, jnp.max(s, axis=-1, keepdims=True))
            p = jnp.exp2(s - m_new)
            acc = jnp.exp2(m - m_new) * acc + _dot(p.astype(BF16), v_ref[pl.ds(r0, tk), sl[j]])
            out.append((m_new, acc))
        return tuple(out)

    nfull, diag = _causal_blocks(i, tq, tk)
    init = tuple((jnp.full((tq, 1), -jnp.inf, F32), jnp.zeros((tq, HEAD_PAD), F32)) for _ in range(MLA_HPS))
    carry = lax.fori_loop(0, nfull, step, init)
    for kb, off in diag:
        carry = step(kb, carry, off)
    outs = [acc[:, :MLA_V] / acc[:, MLA_V:MLA_V + 1] for _, acc in carry]
    o_ref[...] = jnp.concatenate(outs, axis=1).astype(o_ref.dtype)


def _mla_attn(q, k, v, B, S):
    tq = min(TQ, S)
    tk = min(TK, S)
    q3 = q.reshape(B, S, -1)
    k3 = k.reshape(B, S, -1)
    v3 = v.reshape(B, S, -1)
    w = MLA_HPS * HEAD_PAD
    return pl.pallas_call(
        functools.partial(_mla_attn_kernel, tk),
        out_shape=jax.ShapeDtypeStruct((B, S, MLA_HEADS * MLA_V), BF16),
        grid=(B, MLA_HEADS // MLA_HPS, S // tq),
        in_specs=[pl.BlockSpec((None, tq, w), lambda b, h, i: (b, i, h)),
                  pl.BlockSpec((None, S, w), lambda b, h, i: (b, 0, h)),
                  pl.BlockSpec((None, S, w), lambda b, h, i: (b, 0, h))],
        out_specs=pl.BlockSpec((None, tq, MLA_HPS * MLA_V), lambda b, h, i: (b, i, h)),
        compiler_params=_cparams("parallel", "parallel", "arbitrary"),
        name="mla_attn",
    )(q3, k3, v3)


DIFF_HPS = 2


def _diff_attn_kernel(lambda_init, tk, q_ref, k_ref, v_ref, pc_ref, pr_ref, lam_ref, g_ref, o_ref):
    i = pl.program_id(2)
    tq = q_ref.shape[0]
    sl = [slice(j * DIFF_V, (j + 1) * DIFF_V) for j in range(DIFF_HPS)]
    lane = lax.broadcasted_iota(I32, (tq, DIFF_V), 1)
    qs, nslopes = [], []
    for j in range(DIFF_HPS):
        q = q_ref[:, sl[j]]
        zero = jnp.zeros_like(q)
        qs.append((jnp.where(lane < DIFF_HD, q, zero), jnp.where(lane >= DIFF_HD, q, zero)))
        hv = jnp.full((1, 1), pl.program_id(1) * DIFF_HPS + j, I32).astype(F32)
        nslopes.append(-LOG2E * jnp.exp2(-8.0 * (hv + 1.0) / DIFF_HEADS))
    pq = pc_ref[...]

    def step(kb, carry, off=None):
        r0 = pl.multiple_of(kb * tk, tk)
        dist = jnp.abs(pq - pr_ref[:, pl.ds(r0, tk)])
        keep = None if off is None else _keep(tq, tk, off)
        out = []
        for j in range(DIFF_HPS):
            kblk = k_ref[pl.ds(r0, tk), sl[j]]
            vblk = v_ref[pl.ds(r0, tk), sl[j]]
            bias = nslopes[j] * dist
            for c in range(2):
                s = _dot_nt(qs[j][c], kblk) + bias
                if keep is not None:
                    s = jnp.where(keep, s, -jnp.inf)
                out.append(_online_update(s, *carry[2 * j + c], vblk))
        return tuple(out)

    nfull, diag = _causal_blocks(i, tq, tk)
    init1 = (jnp.full((tq, 1), -jnp.inf, F32), jnp.zeros((tq, 1), F32), jnp.zeros((tq, DIFF_V), F32))
    carry = lax.fori_loop(0, nfull, step, (init1,) * (2 * DIFF_HPS))
    for kb, off in diag:
        carry = step(kb, carry, off)

    lv = lam_ref[...]
    lam = (jnp.exp(jnp.sum(lv[0:1] * lv[1:2], axis=-1, keepdims=True))
           - jnp.exp(jnp.sum(lv[2:3] * lv[3:4], axis=-1, keepdims=True)) + lambda_init)
    outs = []
    for j in range(DIFF_HPS):
        (_, l1, a1), (_, l2, a2) = carry[2 * j], carry[2 * j + 1]
        o = a1 / l1 - lam * (a2 / l2)
        outs.append(_rms(o, SUBLN_EPS) * g_ref[...] * (1.0 - lambda_init))
    o_ref[...] = jnp.concatenate(outs, axis=1).astype(o_ref.dtype)


def _diff_attn(dq, dk, dv, pos_col, pos_row, lam4, subln_g, lambda_init, B, S):
    tq = min(TQ, S)
    tk = min(TK, S)
    nq = S // tq
    q3 = dq.reshape(B, S, -1)
    k3 = dk.reshape(B, S, -1)
    v3 = dv.reshape(B, S, -1)
    w = DIFF_HPS * DIFF_V
    return pl.pallas_call(
        functools.partial(_diff_attn_kernel, lambda_init, tk),
        out_shape=jax.ShapeDtypeStruct((B, S, DIFF_HEADS * DIFF_V), BF16),
        grid=(B, DIFF_HEADS // DIFF_HPS, nq),
        in_specs=[pl.BlockSpec((None, tq, w), lambda b, h, i: (b, i, h)),
                  pl.BlockSpec((None, S, w), lambda b, h, i: (b, 0, h)),
                  pl.BlockSpec((None, S, w), lambda b, h, i: (b, 0, h)),
                  pl.BlockSpec((tq, 1), lambda b, h, i: (b * nq + i, 0)),
                  pl.BlockSpec((None, 1, S), lambda b, h, i: (b, 0, 0)),
                  pl.BlockSpec((4, DIFF_HD), lambda b, h, i: (0, 0)),
                  pl.BlockSpec((1, DIFF_V), lambda b, h, i: (0, 0))],
        out_specs=pl.BlockSpec((None, tq, w), lambda b, h, i: (b, i, h)),
        compiler_params=_cparams("parallel", "parallel", "arbitrary"),
        name="diff_attn",
    )(q3, k3, v3, pos_col, pos_row, lam4, subln_g.reshape(1, DIFF_V))


def _first_argmax(v, io, n, axis):
    m = jnp.max(v, axis=axis, keepdims=True)
    ix = jnp.min(jnp.where(v == m, io, n), axis=axis, keepdims=True)
    return m, ix


def _post_attn_kernel(x_ref, mla_ref, dif_ref, mod_ref, woa_ref, wob_ref, pg_ref, fg_ref,
                      rwt_ref, rb_ref,
                      x1_ref, h2_ref, idx_ref, wts_ref, rank_ref, cnt_ref, run_ref):
    step = pl.program_id(0)
    tm = x_ref.shape[0]
    E, G, GS = N_EXPERTS, N_GROUPS, GROUP_SIZE

    @pl.when(step == 0)
    def _():
        run_ref[...] = jnp.zeros_like(run_ref)

    g_a = mod_ref[2:3, :]
    sh_f = mod_ref[3:4, :]
    sc_f = mod_ref[4:5, :]
    y = _dot(mla_ref[...], woa_ref[...]) + _dot(dif_ref[...], wob_ref[...])
    x1 = x_ref[...] + g_a * (_rms(y, NORM_EPS) * pg_ref[...])
    x1_ref[...] = x1
    h2 = (_rms(x1, NORM_EPS) * fg_ref[...]) * (1.0 + sc_f) + sh_f
    _rt_store(h2_ref, h2)

    logits = lax.dot_general(rwt_ref[...], h2, (((1,), (1,)), ((), ())),
                             preferred_element_type=F32, precision=lax.Precision.HIGHEST)
    scores = _sigmoid(logits)
    sel = scores + rb_ref[...]

    sio = lax.broadcasted_iota(I32, (GS, tm), 0)
    gs_rows = []
    for g in range(G):
        blk = sel[g * GS:(g + 1) * GS, :]
        m1, i1 = _first_argmax(blk, sio, GS, 0)
        m2 = jnp.max(jnp.where(sio == i1, -jnp.inf, blk), axis=0, keepdims=True)
        gs_rows.append(m1 + m2)
    gs = jnp.concatenate(gs_rows, axis=0)

    gio = lax.broadcasted_iota(I32, (G, tm), 0)
    gkeep = jnp.zeros((G, tm), F32)
    for _ in range(TOPK_GROUPS):
        _, ix = _first_argmax(gs, gio, G, 0)
        pick = gio == ix
        gkeep = jnp.where(pick, 1.0, gkeep)
        gs = jnp.where(pick, -jnp.inf, gs)
    ekeep = jnp.concatenate([jnp.broadcast_to(gkeep[g:g + 1, :], (GS, tm)) for g in range(G)], axis=0)
    cand = jnp.where(ekeep > 0.0, sel, -jnp.inf)

    eio = lax.broadcasted_iota(I32, (E, tm), 0)
    idx_rows, w_rows = [], []
    for _ in range(TOP_K):
        _, ix = _first_argmax(cand, eio, E, 0)
        pick = eio == ix
        w_rows.append(jnp.sum(jnp.where(pick, scores, 0.0), axis=0, keepdims=True))
        cand = jnp.where(pick, -jnp.inf, cand)
        idx_rows.append(ix)
    idx = jnp.concatenate(idx_rows, axis=0)
    w = jnp.concatenate(w_rows, axis=0)
    wts_ref[...] = w / jnp.sum(w, axis=0, keepdims=True) * ROUTED_SCALE
    idx_ref[...] = idx

    onehot = jnp.zeros((E, tm), F32)
    for k in range(TOP_K):
        onehot = onehot + jnp.where(eio == idx_rows[k], 1.0, 0.0)
    tr = lax.broadcasted_iota(I32, (tm, tm), 0)
    tc = lax.broadcasted_iota(I32, (tm, tm), 1)
    before = jnp.where(tr < tc, 1.0, 0.0).astype(BF16)
    prior = _dot(onehot.astype(BF16), before) + run_ref[...]
    rank_rows = [jnp.sum(jnp.where(eio == idx_rows[k], prior, 0.0), axis=0, keepdims=True)
                 for k in range(TOP_K)]
    rank_ref[...] = jnp.concatenate(rank_rows, axis=0).astype(I32)
    run_ref[...] += jnp.sum(onehot, axis=1, keepdims=True)
    cnt_ref[...] = run_ref[...].astype(I32)


def _post_attn(x2, mla, dif, mod3, w_o, attn_post_g, ffn_pre_g, router_w, router_b, S):
    T, D = x2.shape
    tm = min(TM_POST, S)
    tpb = S // tm
    half = MLA_HEADS * MLA_V
    woa = w_o[:half].astype(BF16)
    wob = w_o[half:].astype(BF16)
    rwt = router_w.T
    full = lambda arr: pl.BlockSpec(arr.shape, lambda i: (0,) * arr.ndim)
    row = lambda w: pl.BlockSpec((tm, w), lambda i: (i, 0))
    col = lambda r: pl.BlockSpec((r, tm), lambda i: (0, i))
    ins = [x2, mla.reshape(T, -1), dif.reshape(T, -1), mod3, woa, wob, attn_post_g.reshape(1, D),
           ffn_pre_g.reshape(1, D), rwt, router_b.reshape(N_EXPERTS, 1)]
    in_specs = [row(D), row(half), row(D - half), pl.BlockSpec((None, 6, D), lambda i: (i // tpb, 0, 0))] + \
               [full(arr) for arr in ins[4:]]
    return pl.pallas_call(
        _post_attn_kernel,
        out_shape=[jax.ShapeDtypeStruct((T, D), F32), jax.ShapeDtypeStruct((T * RT, LANE), F32),
                   jax.ShapeDtypeStruct((TOP_K, T), I32), jax.ShapeDtypeStruct((TOP_K, T), F32),
                   jax.ShapeDtypeStruct((TOP_K, T), I32), jax.ShapeDtypeStruct((N_EXPERTS, 1), I32)],
        grid=(T // tm,),
        in_specs=in_specs,
        out_specs=[row(D), pl.BlockSpec((tm * RT, LANE), lambda i: (i, 0)), col(TOP_K), col(TOP_K), col(TOP_K),
                   pl.BlockSpec((N_EXPERTS, 1), lambda i: (0, 0))],
        scratch_shapes=[pltpu.VMEM((N_EXPERTS, 1), F32)],
        compiler_params=_cparams("arbitrary"),
        name="post_attn",
    )(*ins)


def _dest_kernel(idx_ref, rank_ref, ps_ref, o_ref):
    idx = idx_ref[...]
    tm = idx.shape[1]
    eio = lax.broadcasted_iota(I32, (N_EXPERTS, tm), 0)
    ps = ps_ref[...]
    rows = [jnp.sum(jnp.where(eio == idx[k:k + 1, :], ps, 0.0), axis=0, keepdims=True)
            for k in range(TOP_K)]
    o_ref[...] = jnp.concatenate(rows, axis=0).astype(I32) + rank_ref[...]


def _dest(idx, rank, pstart):
    K, T = idx.shape
    tm = min(2048, T)
    col = pl.BlockSpec((K, tm), lambda i: (0, i))
    return pl.pallas_call(
        _dest_kernel,
        out_shape=jax.ShapeDtypeStruct((K, T), I32),
        grid=(T // tm,),
        in_specs=[col, col, pl.BlockSpec((N_EXPERTS, 1), lambda i: (0, 0))],
        out_specs=col,
        compiler_params=_cparams("parallel"),
        name="dest",
    )(idx, rank, pstart.astype(F32).reshape(N_EXPERTS, 1))


def _dispatch_kernel(bs_ref, cnt_ref, dest_ref, h_ref, xs_ref, zeros, sem, zsem):
    tm = h_ref.shape[0] // RT
    blk = BM * RT
    nb = xs_ref.shape[0] // blk

    @pl.when(pl.program_id(0) == 0)
    def _():
        zeros[...] = jnp.zeros_like(zeros)

        def zcopy(g):
            return pltpu.make_async_copy(zeros, xs_ref.at[pl.ds(pl.multiple_of(g * blk, blk), blk), :], zsem)

        def has_pad(e):
            return cnt_ref[e] % BM != 0

        def fill(e, _):
            @pl.when(has_pad(e))
            def _():
                zcopy(bs_ref[e + 1] - 1).start()
            return 0

        def drain(e, _):
            @pl.when(has_pad(e))
            def _():
                zcopy(0).wait()
            return 0

        lax.fori_loop(0, N_EXPERTS, fill, 0)
        lax.fori_loop(bs_ref[N_EXPERTS], nb, lambda g, _: (zcopy(g).start(), 0)[1], 0)
        lax.fori_loop(0, N_EXPERTS, drain, 0)
        lax.fori_loop(bs_ref[N_EXPERTS], nb, lambda g, _: (zcopy(0).wait(), 0)[1], 0)

    def tile(ref, r):
        return ref.at[pl.ds(pl.multiple_of(r * RT, RT), RT), :]

    def issue(t, _):
        for k in range(TOP_K):
            pltpu.make_async_copy(tile(h_ref, t), tile(xs_ref, dest_ref[k, t]), sem).start()
        return 0

    lax.fori_loop(0, tm, issue, 0)
    for _ in range(TOP_K):
        pltpu.make_async_copy(h_ref, xs_ref.at[pl.ds(0, tm * RT), :], sem).wait()


def _dispatch(h2rt, dest, bstart, cnt, P):
    T = h2rt.shape[0] // RT
    tm = min(TM_DISP, T)
    nt = T // tm
    dest3 = dest.reshape(TOP_K, nt, tm).transpose(1, 0, 2)
    return pl.pallas_call(
        _dispatch_kernel,
        out_shape=jax.ShapeDtypeStruct((P * RT, LANE), F32),
        grid_spec=pltpu.PrefetchScalarGridSpec(
            num_scalar_prefetch=2,
            grid=(nt,),
            in_specs=[pl.BlockSpec((None, TOP_K, tm), lambda i, bs, cn: (i, 0, 0), memory_space=pltpu.SMEM),
                      pl.BlockSpec((tm * RT, LANE), lambda i, bs, cn: (i, 0))],
            out_specs=pl.BlockSpec(memory_space=pl.ANY),
            scratch_shapes=[pltpu.VMEM((BM * RT, LANE), F32), pltpu.SemaphoreType.DMA(()),
                            pltpu.SemaphoreType.DMA(())],
        ),
        compiler_params=_cparams("arbitrary"),
        name="dispatch",
    )(bstart, cnt, dest3, h2rt)


NBUF_X = 4
NBUF_Y = 3


def _experts_kernel(bs_ref, wg_ref, wu_ref, wd_ref, xs_ref, ys_ref, wgb, wub, wdb, xbuf, ybuf, xsem, ysem):
    e = pl.program_id(0)
    blk = BM * RT
    nb = xs_ref.shape[0] // blk
    nused = bs_ref[N_EXPERTS]
    g0 = bs_ref[e]
    g1 = bs_ref[e + 1]

    def rows(g):
        return pl.ds(pl.multiple_of(g * blk, blk), blk)

    def x_copy(g, slot):
        return pltpu.make_async_copy(xs_ref.at[rows(g), :], xbuf.at[slot], xsem.at[slot])

    def y_copy(g, slot):
        return pltpu.make_async_copy(ybuf.at[slot], ys_ref.at[rows(g), :], ysem.at[slot])

    @pl.when(e == 0)
    def _():
        for j in range(NBUF_X - 1):
            @pl.when(j < nused)
            def _():
                x_copy(j, j).start()

    @pl.when(g1 > g0)
    def _():
        wgb[...] = wg_ref[...].astype(BF16)
        wub[...] = wu_ref[...].astype(BF16)
        wdb[...] = wd_ref[...].astype(BF16)

        def body(g, _):
            slot = g % NBUF_X
            x_copy(g, slot).wait()
            nxt = g + NBUF_X - 1

            @pl.when(nxt < nused)
            def _():
                x_copy(nxt, nxt % NBUF_X).start()

            x = _rt_load(xbuf, BM, slot).astype(BF16)
            gate = _dot(x, wgb[...])
            up = _dot(x, wub[...])
            a = (gate * _sigmoid(gate)) * up
            y = _dot(a.astype(BF16), wdb[...])
            ys = g % NBUF_Y

            @pl.when(g >= NBUF_Y)
            def _():
                y_copy(g - NBUF_Y, ys).wait()

            _rt_store(ybuf, y, ys)
            y_copy(g, ys).start()
            return 0

        lax.fori_loop(g0, g1, body, 0)

    @pl.when(e == N_EXPERTS - 1)
    def _():
        for j in range(NBUF_Y):
            @pl.when(nused - 1 - j >= 0)
            def _():
                y_copy(0, (nused - 1 - j) % NBUF_Y).wait()
        ybuf[0] = jnp.zeros(ybuf.shape[1:], F32)
        lax.fori_loop(nused, nb, lambda g, _: (y_copy(g, 0).start(), 0)[1], 0)
        lax.fori_loop(nused, nb, lambda g, _: (y_copy(0, 0).wait(), 0)[1], 0)


def _experts(xs, bstart, w_gate, w_up, w_down):
    E, D, F = w_gate.shape
    return pl.pallas_call(
        _experts_kernel,
        out_shape=jax.ShapeDtypeStruct(xs.shape, F32),
        grid_spec=pltpu.PrefetchScalarGridSpec(
            num_scalar_prefetch=1,
            grid=(E,),
            in_specs=[pl.BlockSpec((None, D, F), lambda e, bs: (e, 0, 0)),
                      pl.BlockSpec((None, D, F), lambda e, bs: (e, 0, 0)),
                      pl.BlockSpec((None, F, D), lambda e, bs: (e, 0, 0)),
                      pl.BlockSpec(memory_space=pl.ANY)],
            out_specs=pl.BlockSpec(memory_space=pl.ANY),
            scratch_shapes=[pltpu.VMEM((D, F), BF16), pltpu.VMEM((D, F), BF16), pltpu.VMEM((F, D), BF16),
                            pltpu.VMEM((NBUF_X, BM * RT, LANE), F32), pltpu.VMEM((NBUF_Y, BM * RT, LANE), F32),
                            pltpu.SemaphoreType.DMA((NBUF_X,)), pltpu.SemaphoreType.DMA((NBUF_Y,))],
        ),
        compiler_params=_cparams("arbitrary"),
        name="experts",
    )(bstart, w_gate, w_up, w_down, xs)


def _combine_kernel(dcur_ref, dnxt_ref, w_ref, x1_ref, h_ref, mod_ref, pg_ref, sg_ref, su_ref, sd_ref,
                    ys_ref, o_ref, rows, sem):
    i = pl.program_id(0)
    n = pl.num_programs(0)
    tm = x1_ref.shape[0]
    slot = i % 2

    def tile(r):
        return pl.ds(pl.multiple_of(r * RT, RT), RT)

    def gather(dref, s):
        def issue(t, _):
            for k in range(TOP_K):
                pltpu.make_async_copy(ys_ref.at[tile(dref[k, t]), :],
                                      rows.at[s, k, tile(t), :], sem.at[s]).start()
            return 0
        lax.fori_loop(0, tm, issue, 0)

    @pl.when(i == 0)
    def _():
        gather(dcur_ref, 0)

    @pl.when(i + 1 < n)
    def _():
        gather(dnxt_ref, 1 - slot)

    for k in range(TOP_K):
        pltpu.make_async_copy(ys_ref.at[pl.ds(0, tm * RT), :], rows.at[slot, k], sem.at[slot]).wait()

    w = w_ref[...]
    routed = w[:, 0:1] * _rt_load(rows, tm, slot, 0)
    for k in range(1, TOP_K):
        routed = routed + w[:, k:k + 1] * _rt_load(rows, tm, slot, k)
    hb = _rt_load(h_ref, tm).astype(BF16)
    g = _dot(hb, sg_ref[...])
    u = _dot(hb, su_ref[...])
    shared = _dot(((g * _sigmoid(g)) * u).astype(BF16), sd_ref[...])
    y = routed + shared
    g_f = mod_ref[5:6, :]
    o_ref[...] = x1_ref[...] + g_f * (_rms(y, NORM_EPS) * pg_ref[...])


def _combine(dest, wts, x1, h2, mod3, ffn_post_g, sw_gate, sw_up, sw_down, ys, S):
    T, D = x1.shape
    tm = min(TM_COMB, S)
    nt = T // tm
    tpb = S // tm
    dest3 = dest.reshape(TOP_K, nt, tm).transpose(1, 0, 2)
    w_tk = wts.T
    full = lambda arr: pl.BlockSpec(arr.shape, lambda i: (0,) * arr.ndim)
    row = lambda w: pl.BlockSpec((tm, w), lambda i: (i, 0))
    sg, su, sd = sw_gate.astype(BF16), sw_up.astype(BF16), sw_down.astype(BF16)
    pg = ffn_post_g.reshape(1, D)
    return pl.pallas_call(
        _combine_kernel,
        out_shape=jax.ShapeDtypeStruct((T, D), F32),
        grid=(nt,),
        in_specs=[pl.BlockSpec((None, TOP_K, tm), lambda i: (i, 0, 0), memory_space=pltpu.SMEM),
                  pl.BlockSpec((None, TOP_K, tm), lambda i: (jnp.minimum(i + 1, nt - 1), 0, 0),
                               memory_space=pltpu.SMEM),
                  row(TOP_K), row(D), pl.BlockSpec((tm * RT, LANE), lambda i: (i, 0)),
                  pl.BlockSpec((None, 6, D), lambda i: (i // tpb, 0, 0)),
                  full(pg), full(sg), full(su), full(sd),
                  pl.BlockSpec(memory_space=pl.ANY)],
        out_specs=row(D),
        scratch_shapes=[pltpu.VMEM((2, TOP_K, tm * RT, LANE), F32), pltpu.SemaphoreType.DMA((2,))],
        compiler_params=_cparams("arbitrary"),
        name="combine",
    )(dest3, dest3, w_tk, x1, h2, mod3, pg, sg, su, sd, ys)


def _moe(h2, x1, idx, wts, rank, counts, mod3, ffn_post_g, exp_w_gate, exp_w_up, exp_w_down,
         sw_gate, sw_up, sw_down, S):
    T, D = x1.shape
    A = T * TOP_K
    P = A + N_EXPERTS * BM
    cnt = counts.reshape(N_EXPERTS)
    blocks = (cnt + BM - 1) // BM
    bstart = jnp.concatenate([jnp.zeros((1,), I32), jnp.cumsum(blocks).astype(I32)])
    pstart = bstart[:-1] * BM

    dest = _dest(idx, rank, pstart)
    xs = _dispatch(h2, dest, bstart, cnt, P)
    ys = _experts(xs, bstart, exp_w_gate, exp_w_up, exp_w_down)
    return _combine(dest, wts, x1, h2, mod3, ffn_post_g, sw_gate, sw_up, sw_down, ys, S)


def _layer(x, c, positions, lambda_init, w_ada, b_ada, attn_pre_g, attn_post_g, w_in, q_norm_g, kv_norm_g,
           w_uq, w_ukv, lam_q1, lam_k1, lam_q2, lam_k2, diff_subln_g, w_o, ffn_pre_g, ffn_post_g,
           router_w, router_b, exp_w_gate, exp_w_up, exp_w_down, sw_gate, sw_up, sw_down):
    B, S, D = x.shape
    T = B * S
    x2 = x.reshape(T, D)
    posf = positions.astype(F32)
    pos_col = posf.reshape(T, 1)
    pos_row = posf.reshape(B, 1, S)

    mod3 = _ada(c, w_ada, b_ada).reshape(B, 6, D)
    q, k, v, dq, dk, dv = _pre_attn(x2, mod3, pos_col, attn_pre_g, w_in, q_norm_g, kv_norm_g, w_uq, w_ukv, S)
    mla = _mla_attn(q, k, v, B, S)
    lam4 = jnp.stack([lam_q1, lam_k1, lam_q2, lam_k2])
    dif = _diff_attn(dq, dk, dv, pos_col, pos_row, lam4, diff_subln_g, lambda_init, B, S)
    x1, h2, idx, wts, rank, counts = _post_attn(x2, mla, dif, mod3, w_o, attn_post_g, ffn_pre_g,
                                                router_w, router_b, S)
    out = _moe(h2, x1, idx, wts, rank, counts, mod3, ffn_post_g, exp_w_gate, exp_w_up, exp_w_down,
               sw_gate, sw_up, sw_down, S)
    return out.reshape(B, S, D)


def kernel(x, c, positions, w_ada, b_ada, attn_pre_g, attn_post_g, w_in, q_norm_g, kv_norm_g, w_uq, w_ukv,
           lam_q1, lam_k1, lam_q2, lam_k2, diff_subln_g, w_o, ffn_pre_g, ffn_post_g, router_w, router_b,
           exp_w_gate, exp_w_up, exp_w_down, shared_w_gate, shared_w_up, shared_w_down):
    depth = w_ada.shape[0]
    for l in range(depth):
        lambda_init = 0.8 - 0.6 * math.exp(-0.3 * l)
        x = _layer(x, c, positions, lambda_init, w_ada[l], b_ada[l], attn_pre_g[l], attn_post_g[l], w_in[l],
                   q_norm_g[l], kv_norm_g[l], w_uq[l], w_ukv[l], lam_q1[l], lam_k1[l], lam_q2[l], lam_k2[l],
                   diff_subln_g[l], w_o[l], ffn_pre_g[l], ffn_post_g[l], router_w[l], router_b[l],
                   exp_w_gate[l], exp_w_up[l], exp_w_down[l], shared_w_gate[l], shared_w_up[l],
                   shared_w_down[l])
    return x
```

```python
import functools
import math

import jax
import jax.numpy as jnp
from jax import lax
from jax.experimental import pallas as pl
from jax.experimental.pallas import tpu as pltpu

F32 = jnp.float32
BF16 = jnp.bfloat16
I32 = jnp.int32

MLA_HEADS = 8
MLA_NOPE = 64
MLA_ROPE = 32
MLA_V = 64
MLA_Q_RANK = 256
MLA_KV_RANK = 128
ROPE_BASE = 10000.0
DIFF_HEADS = 4
DIFF_HD = 64
DIFF_V = 128
N_EXPERTS = 256
TOP_K = 8
N_GROUPS = 8
GROUP_SIZE = N_EXPERTS // N_GROUPS
TOPK_GROUPS = 4
ROUTED_SCALE = 2.5
NORM_EPS = 1e-6
SUBLN_EPS = 1e-5
LOG2E = 1.4426950408889634

LANE = 128
HEAD_PAD = 128

TM_PRE = 512
TQ = 512
TK = 512
TM_POST = 512
TM_DISP = 256
TM_COMB = 128
BM = 128
VMEM_LIMIT = 48 * 1024 * 1024


def _cparams(*sem):
    return pltpu.CompilerParams(dimension_semantics=sem, vmem_limit_bytes=VMEM_LIMIT)


def _rms(x, eps):
    return x * lax.rsqrt(jnp.mean(x * x, axis=-1, keepdims=True) + eps)


def _sigmoid(x):
    return 1.0 / (1.0 + jnp.exp(-x))


def _dot(a, b):
    return jnp.dot(a, b, preferred_element_type=F32)


RT = 8


def _rt_load(ref, n, *lead):
    return jnp.concatenate([ref[(*lead, pl.ds(j, n, stride=RT), slice(None))] for j in range(RT)], axis=1)


def _rt_store(ref, val, *lead):
    n = val.shape[0]
    for j in range(RT):
        ref[(*lead, pl.ds(j, n, stride=RT), slice(None))] = val[:, j * LANE:(j + 1) * LANE]


def _dot_nt(a, b):
    return lax.dot_general(a, b, (((1,), (1,)), ((), ())), preferred_element_type=F32)


def _ada_kernel(c_ref, w_ref, b_ref, o_ref):
    c = c_ref[...]
    a = c * _sigmoid(c)
    o_ref[...] = jnp.dot(a, w_ref[...], preferred_element_type=F32,
                         precision=lax.Precision.HIGHEST) + b_ref[...]


def _ada(c, w_ada, b_ada):
    B, D = c.shape
    n = w_ada.shape[1]
    return pl.pallas_call(
        _ada_kernel,
        out_shape=jax.ShapeDtypeStruct((B, n), F32),
        grid=(n // D,),
        in_specs=[pl.BlockSpec((B, D), lambda j: (0, 0)),
                  pl.BlockSpec((D, D), lambda j: (0, j)),
                  pl.BlockSpec((1, D), lambda j: (0, j))],
        out_specs=pl.BlockSpec((B, D), lambda j: (0, j)),
        compiler_params=_cparams("arbitrary"),
        name="ada",
    )(c, w_ada, b_ada.reshape(1, n))


_C_CQ = 0
_C_CKV = _C_CQ + MLA_Q_RANK
_C_KRA = _C_CKV + MLA_KV_RANK
_C_KRB = _C_KRA + LANE
_C_DQ = _C_KRB + LANE
_C_DK = _C_DQ + DIFF_HEADS * DIFF_V
_C_DV = _C_DK + DIFF_HEADS * DIFF_V
_C_END = _C_DV + DIFF_HEADS * DIFF_V


def _pre_attn_kernel(x_ref, mod_ref, pos_ref, inv_ref, g_ref, w1_ref, qg_ref, kvg_ref,
                     wqa_ref, wqb_ref, wkn_ref, wv_ref,
                     q_ref, k_ref, v_ref, dq_ref, dk_ref, dv_ref):
    x = x_ref[...]
    sh = mod_ref[0:1, :]
    sc = mod_ref[1:2, :]
    h = _rms(x, NORM_EPS) * g_ref[...]
    h = h * (1.0 + sc) + sh
    p = _dot(h.astype(BF16), w1_ref[...])

    ang = pos_ref[...] * inv_ref[...]
    lane = lax.broadcasted_iota(I32, ang.shape, 1)
    in_rope = (lane >= MLA_NOPE) & (lane < MLA_NOPE + MLA_ROPE)
    cos_r = jnp.where(in_rope, jnp.cos(ang), 0.0)
    sin_r = jnp.where(in_rope, jnp.sin(ang), 0.0)
    cos_q = jnp.where(lane < MLA_NOPE, 1.0, cos_r)

    cqn = (_rms(p[:, _C_CQ:_C_CKV], NORM_EPS) * qg_ref[...]).astype(BF16)
    qa = _dot(cqn, wqa_ref[...])
    qb = _dot(cqn, wqb_ref[...])
    q_scale = LOG2E / math.sqrt(MLA_NOPE + MLA_ROPE)
    cos_t = jnp.concatenate([cos_q] * MLA_HEADS, axis=1)
    sin_t = jnp.concatenate([sin_r] * MLA_HEADS, axis=1)
    q_ref[...] = ((qa * cos_t + qb * sin_t) * q_scale).astype(BF16)

    ckvn = (_rms(p[:, _C_CKV:_C_KRA], NORM_EPS) * kvg_ref[...]).astype(BF16)
    kn = _dot(ckvn, wkn_ref[...])
    kr = p[:, _C_KRA:_C_KRB] * cos_r + p[:, _C_KRB:_C_DQ] * sin_r
    k_ref[...] = (kn + jnp.concatenate([kr] * MLA_HEADS, axis=1)).astype(BF16)
    lane_t = lax.broadcasted_iota(I32, kn.shape, 1)
    ones_col = jnp.where(lane_t % HEAD_PAD == MLA_V, 1.0, 0.0)
    v_ref[...] = (_dot(ckvn, wv_ref[...]) + ones_col).astype(BF16)

    dq_ref[...] = (p[:, _C_DQ:_C_DK] * (LOG2E / math.sqrt(DIFF_HD))).astype(BF16)
    dk_ref[...] = p[:, _C_DK:_C_DV].astype(BF16)
    dv_ref[...] = p[:, _C_DV:_C_END].astype(BF16)


def _pre_attn(x2, mod3, pos_col, attn_pre_g, w_in, q_norm_g, kv_norm_g, w_uq, w_ukv, S):
    T, D = x2.shape
    tm = min(TM_PRE, S)
    tpb = S // tm
    f = lambda a: a.astype(BF16)
    d3 = DIFF_HEADS * DIFF_V
    a = MLA_Q_RANK
    b = a + MLA_KV_RANK
    c = b + MLA_ROPE

    def swap(r):
        hlf = MLA_ROPE // 2
        return jnp.concatenate([-r[..., hlf:], r[..., :hlf]], axis=-1)

    def pad_rope(r):
        return jnp.pad(r, ((0, 0), (MLA_NOPE, LANE - MLA_NOPE - MLA_ROPE)))

    w_kr = w_in[:, b:c]
    w1 = jnp.concatenate([w_in[:, :b], pad_rope(w_kr), pad_rope(swap(w_kr)), w_in[:, c:]], axis=1)
    assert w1.shape[1] == _C_END
    padq = HEAD_PAD - MLA_NOPE - MLA_ROPE
    wqa = jnp.pad(w_uq, ((0, 0), (0, 0), (0, padq))).reshape(MLA_Q_RANK, MLA_HEADS * HEAD_PAD)
    q_rope = w_uq[..., MLA_NOPE:]
    wqb = jnp.pad(swap(q_rope), ((0, 0), (0, 0), (MLA_NOPE, padq))).reshape(MLA_Q_RANK, MLA_HEADS * HEAD_PAD)
    wkn = jnp.pad(w_ukv[..., :MLA_NOPE], ((0, 0), (0, 0), (0, HEAD_PAD - MLA_NOPE))).reshape(
        MLA_KV_RANK, MLA_HEADS * HEAD_PAD)
    wv = jnp.pad(w_ukv[..., MLA_NOPE:], ((0, 0), (0, 0), (0, HEAD_PAD - MLA_V))).reshape(
        MLA_KV_RANK, MLA_HEADS * HEAD_PAD)

    inv = 1.0 / (ROPE_BASE ** (jnp.arange(0, MLA_ROPE, 2, dtype=F32) / MLA_ROPE))
    inv_lane = jnp.pad(jnp.concatenate([inv, inv]), (MLA_NOPE, LANE - MLA_NOPE - MLA_ROPE)).reshape(1, LANE)

    full = lambda arr: pl.BlockSpec(arr.shape, lambda i: (0,) * arr.ndim)
    row = lambda w: pl.BlockSpec((tm, w), lambda i: (i, 0))
    ins = [x2, mod3, pos_col, inv_lane, attn_pre_g.reshape(1, D), f(w1), q_norm_g.reshape(1, -1),
           kv_norm_g.reshape(1, -1), f(wqa), f(wqb), f(wkn), f(wv)]
    in_specs = [row(D), pl.BlockSpec((None, 6, D), lambda i: (i // tpb, 0, 0)), row(1)] + \
               [full(arr) for arr in ins[3:]]
    widths = [MLA_HEADS * HEAD_PAD, MLA_HEADS * HEAD_PAD, MLA_HEADS * HEAD_PAD, d3, d3, d3]
    return pl.pallas_call(
        _pre_attn_kernel,
        out_shape=[jax.ShapeDtypeStruct((T, w), BF16) for w in widths],
        grid=(T // tm,),
        in_specs=in_specs,
        out_specs=[row(w) for w in widths],
        compiler_params=_cparams("parallel"),
        name="pre_attn",
    )(*ins)


def _online_update(s, m, l, acc, v):
    m_new = jnp.maximum(m, jnp.max(s, axis=-1, keepdims=True))
    p = jnp.exp2(s - m_new)
    alpha = jnp.exp2(m - m_new)
    l_new = alpha * l + jnp.sum(p, axis=-1, keepdims=True)
    acc_new = alpha * acc + _dot(p.astype(BF16), v)
    return m_new, l_new, acc_new


MLA_HPS = 4


def _causal_blocks(i, tq, tk):
    nfull = (i * tq) // tk
    diag = [(nfull + j, i * tq - (nfull + j) * tk) for j in range(max(1, tq // tk))]
    return nfull, diag


def _keep(tq, tk, off):
    r = lax.broadcasted_iota(I32, (tq, tk), 0)
    c = lax.broadcasted_iota(I32, (tq, tk), 1)
    return r + off >= c


def _mla_attn_kernel(tk, q_ref, k_ref, v_ref, o_ref):
    i = pl.program_id(2)
    tq = q_ref.shape[0]
    sl = [slice(j * HEAD_PAD, (j + 1) * HEAD_PAD) for j in range(MLA_HPS)]
    qs = [q_ref[:, sl[j]] for j in range(MLA_HPS)]

    def step(kb, carry, off=None):
        r0 = pl.multiple_of(kb * tk, tk)
        out = []
        for j in range(MLA_HPS):
            m, acc = carry[j]
            s = _dot_nt(qs[j], k_ref[pl.ds(r0, tk), sl[j]])
            if off is not None:
                s = jnp.where(_keep(tq, tk, off), s, -jnp.inf)
            m_new = jnp.maximum(m, jnp.max(s, axis=-1, keepdims=True))
            p = jnp.exp2(s - m_new)
            acc = jnp.exp2(m - m_new) * acc + _dot(p.astype(BF16), v_ref[pl.ds(r0, tk), sl[j]])
            out.append((m_new, acc))
        return tuple(out)

    nfull, diag = _causal_blocks(i, tq, tk)
    init = tuple((jnp.full((tq, 1), -jnp.inf, F32), jnp.zeros((tq, HEAD_PAD), F32)) for _ in range(MLA_HPS))
    carry = lax.fori_loop(0, nfull, step, init)
    for kb, off in diag:
        carry = step(kb, carry, off)
    outs = [acc[:, :MLA_V] / acc[:, MLA_V:MLA_V + 1] for _, acc in carry]
    o_ref[...] = jnp.concatenate(outs, axis=1).astype(o_ref.dtype)


def _mla_attn(q, k, v, B, S):
    tq = min(TQ, S)
    tk = min(TK, S)
    q3 = q.reshape(B, S, -1)
    k3 = k.reshape(B, S, -1)
    v3 = v.reshape(B, S, -1)
    w = MLA_HPS * HEAD_PAD
    return pl.pallas_call(
        functools.partial(_mla_attn_kernel, tk),
        out_shape=jax.ShapeDtypeStruct((B, S, MLA_HEADS * MLA_V), BF16),
        grid=(B, MLA_HEADS // MLA_HPS, S // tq),
        in_specs=[pl.BlockSpec((None, tq, w), lambda b, h, i: (b, i, h)),
                  pl.BlockSpec((None, S, w), lambda b, h, i: (b, 0, h)),
                  pl.BlockSpec((None, S, w), lambda b, h, i: (b, 0, h))],
        out_specs=pl.BlockSpec((None, tq, MLA_HPS * MLA_V), lambda b, h, i: (b, i, h)),
        compiler_params=_cparams("parallel", "parallel", "arbitrary"),
        name="mla_attn",
    )(q3, k3, v3)


DIFF_HPS = 2


def _diff_attn_kernel(lambda_init, tk, q_ref, k_ref, v_ref, pc_ref, pr_ref, lam_ref, g_ref, o_ref):
    i = pl.program_id(2)
    tq = q_ref.shape[0]
    sl = [slice(j * DIFF_V, (j + 1) * DIFF_V) for j in range(DIFF_HPS)]
    lane = lax.broadcasted_iota(I32, (tq, DIFF_V), 1)
    qs, nslopes = [], []
    for j in range(DIFF_HPS):
        q = q_ref[:, sl[j]]
        zero = jnp.zeros_like(q)
        qs.append((jnp.where(lane < DIFF_HD, q, zero), jnp.where(lane >= DIFF_HD, q, zero)))
        hv = jnp.full((1, 1), pl.program_id(1) * DIFF_HPS + j, I32).astype(F32)
        nslopes.append(-LOG2E * jnp.exp2(-8.0 * (hv + 1.0) / DIFF_HEADS))
    pq = pc_ref[...]

    def step(kb, carry, off=None):
        r0 = pl.multiple_of(kb * tk, tk)
        dist = jnp.abs(pq - pr_ref[:, pl.ds(r0, tk)])
        keep = None if off is None else _keep(tq, tk, off)
        out = []
        for j in range(DIFF_HPS):
            kblk = k_ref[pl.ds(r0, tk), sl[j]]
            vblk = v_ref[pl.ds(r0, tk), sl[j]]
            bias = nslopes[j] * dist
            for c in range(2):
                s = _dot_nt(qs[j][c], kblk) + bias
                if keep is not None:
                    s = jnp.where(keep, s, -jnp.inf)
                out.append(_online_update(s, *carry[2 * j + c], vblk))
        return tuple(out)

    nfull, diag = _causal_blocks(i, tq, tk)
    init1 = (jnp.full((tq, 1), -jnp.inf, F32), jnp.zeros((tq, 1), F32), jnp.zeros((tq, DIFF_V), F32))
    carry = lax.fori_loop(0, nfull, step, (init1,) * (2 * DIFF_HPS))
    for kb, off in diag:
        carry = step(kb, carry, off)

    lv = lam_ref[...]
    lam = (jnp.exp(jnp.sum(lv[0:1] * lv[1:2], axis=-1, keepdims=True))
           - jnp.exp(jnp.sum(lv[2:3] * lv[3:4], axis=-1, keepdims=True)) + lambda_init)
    outs = []
    for j in range(DIFF_HPS):
        (_, l1, a1), (_, l2, a2) = carry[2 * j], carry[2 * j + 1]
        o = a1 / l1 - lam * (a2 / l2)
        outs.append(_rms(o, SUBLN_EPS) * g_ref[...] * (1.0 - lambda_init))
    o_ref[...] = jnp.concatenate(outs, axis=1).astype(o_ref.dtype)


def _diff_attn(dq, dk, dv, pos_col, pos_row, lam4, subln_g, lambda_init, B, S):
    tq = min(TQ, S)
    tk = min(TK, S)
    nq = S // tq
    q3 = dq.reshape(B, S, -1)
    k3 = dk.reshape(B, S, -1)
    v3 = dv.reshape(B, S, -1)
    w = DIFF_HPS * DIFF_V
    return pl.pallas_call(
        functools.partial(_diff_attn_kernel, lambda_init, tk),
        out_shape=jax.ShapeDtypeStruct((B, S, DIFF_HEADS * DIFF_V), BF16),
        grid=(B, DIFF_HEADS // DIFF_HPS, nq),
        in_specs=[pl.BlockSpec((None, tq, w), lambda b, h, i: (b, i, h)),
                  pl.BlockSpec((None, S, w), lambda b, h, i: (b, 0, h)),
                  pl.BlockSpec((None, S, w), lambda b, h, i: (b, 0, h)),
                  pl.BlockSpec((tq, 1), lambda b, h, i: (b * nq + i, 0)),
                  pl.BlockSpec((None, 1, S), lambda b, h, i: (b, 0, 0)),
                  pl.BlockSpec((4, DIFF_HD), lambda b, h, i: (0, 0)),
                  pl.BlockSpec((1, DIFF_V), lambda b, h, i: (0, 0))],
        out_specs=pl.BlockSpec((None, tq, w), lambda b, h, i: (b, i, h)),
        compiler_params=_cparams("parallel", "parallel", "arbitrary"),
        name="diff_attn",
    )(q3, k3, v3, pos_col, pos_row, lam4, subln_g.reshape(1, DIFF_V))


def _first_argmax(v, io, n, axis):
    m = jnp.max(v, axis=axis, keepdims=True)
    ix = jnp.min(jnp.where(v == m, io, n), axis=axis, keepdims=True)
    return m, ix


def _post_attn_kernel(x_ref, mla_ref, dif_ref, mod_ref, woa_ref, wob_ref, pg_ref, fg_ref,
                      rwt_ref, rb_ref,
                      x1_ref, h2_ref, idx_ref, wts_ref, rank_ref, cnt_ref, run_ref):
    step = pl.program_id(0)
    tm = x_ref.shape[0]
    E, G, GS = N_EXPERTS, N_GROUPS, GROUP_SIZE

    @pl.when(step == 0)
    def _():
        run_ref[...] = jnp.zeros_like(run_ref)

    g_a = mod_ref[2:3, :]
    sh_f = mod_ref[3:4, :]
    sc_f = mod_ref[4:5, :]
    y = _dot(mla_ref[...], woa_ref[...]) + _dot(dif_ref[...], wob_ref[...])
    x1 = x_ref[...] + g_a * (_rms(y, NORM_EPS) * pg_ref[...])
    x1_ref[...] = x1
    h2 = (_rms(x1, NORM_EPS) * fg_ref[...]) * (1.0 + sc_f) + sh_f
    _rt_store(h2_ref, h2)

    logits = lax.dot_general(rwt_ref[...], h2, (((1,), (1,)), ((), ())),
                             preferred_element_type=F32, precision=lax.Precision.HIGHEST)
    scores = _sigmoid(logits)
    sel = scores + rb_ref[...]

    sio = lax.broadcasted_iota(I32, (GS, tm), 0)
    gs_rows = []
    for g in range(G):
        blk = sel[g * GS:(g + 1) * GS, :]
        m1, i1 = _first_argmax(blk, sio, GS, 0)
        m2 = jnp.max(jnp.where(sio == i1, -jnp.inf, blk), axis=0, keepdims=True)
        gs_rows.append(m1 + m2)
    gs = jnp.concatenate(gs_rows, axis=0)

    gio = lax.broadcasted_iota(I32, (G, tm), 0)
    gkeep = jnp.zeros((G, tm), F32)
    for _ in range(TOPK_GROUPS):
        _, ix = _first_argmax(gs, gio, G, 0)
        pick = gio == ix
        gkeep = jnp.where(pick, 1.0, gkeep)
        gs = jnp.where(pick, -jnp.inf, gs)
    ekeep = jnp.concatenate([jnp.broadcast_to(gkeep[g:g + 1, :], (GS, tm)) for g in range(G)], axis=0)
    cand = jnp.where(ekeep > 0.0, sel, -jnp.inf)

    eio = lax.broadcasted_iota(I32, (E, tm), 0)
    idx_rows, w_rows = [], []
    for _ in range(TOP_K):
        _, ix = _first_argmax(cand, eio, E, 0)
        pick = eio == ix
        w_rows.append(jnp.sum(jnp.where(pick, scores, 0.0), axis=0, keepdims=True))
        cand = jnp.where(pick, -jnp.inf, cand)
        idx_rows.append(ix)
    idx = jnp.concatenate(idx_rows, axis=0)
    w = jnp.concatenate(w_rows, axis=0)
    wts_ref[...] = w / jnp.sum(w, axis=0, keepdims=True) * ROUTED_SCALE
    idx_ref[...] = idx

    onehot = jnp.zeros((E, tm), F32)
    for k in range(TOP_K):
        onehot = onehot + jnp.where(eio == idx_rows[k], 1.0, 0.0)
    tr = lax.broadcasted_iota(I32, (tm, tm), 0)
    tc = lax.broadcasted_iota(I32, (tm, tm), 1)
    before = jnp.where(tr < tc, 1.0, 0.0).astype(BF16)
    prior = _dot(onehot.astype(BF16), before) + run_ref[...]
    rank_rows = [jnp.sum(jnp.where(eio == idx_rows[k], prior, 0.0), axis=0, keepdims=True)
                 for k in range(TOP_K)]
    rank_ref[...] = jnp.concatenate(rank_rows, axis=0).astype(I32)
    run_ref[...] += jnp.sum(onehot, axis=1, keepdims=True)
    cnt_ref[...] = run_ref[...].astype(I32)


def _post_attn(x2, mla, dif, mod3, w_o, attn_post_g, ffn_pre_g, router_w, router_b, S):
    T, D = x2.shape
    tm = min(TM_POST, S)
    tpb = S // tm
    half = MLA_HEADS * MLA_V
    woa = w_o[:half].astype(BF16)
    wob = w_o[half:].astype(BF16)
    rwt = router_w.T
    full = lambda arr: pl.BlockSpec(arr.shape, lambda i: (0,) * arr.ndim)
    row = lambda w: pl.BlockSpec((tm, w), lambda i: (i, 0))
    col = lambda r: pl.BlockSpec((r, tm), lambda i: (0, i))
    ins = [x2, mla.reshape(T, -1), dif.reshape(T, -1), mod3, woa, wob, attn_post_g.reshape(1, D),
           ffn_pre_g.reshape(1, D), rwt, router_b.reshape(N_EXPERTS, 1)]
    in_specs = [row(D), row(half), row(D - half), pl.BlockSpec((None, 6, D), lambda i: (i // tpb, 0, 0))] + \
               [full(arr) for arr in ins[4:]]
    return pl.pallas_call(
        _post_attn_kernel,
        out_shape=[jax.ShapeDtypeStruct((T, D), F32), jax.ShapeDtypeStruct((T * RT, LANE), F32),
                   jax.ShapeDtypeStruct((TOP_K, T), I32), jax.ShapeDtypeStruct((TOP_K, T), F32),
                   jax.ShapeDtypeStruct((TOP_K, T), I32), jax.ShapeDtypeStruct((N_EXPERTS, 1), I32)],
        grid=(T // tm,),
        in_specs=in_specs,
        out_specs=[row(D), pl.BlockSpec((tm * RT, LANE), lambda i: (i, 0)), col(TOP_K), col(TOP_K), col(TOP_K),
                   pl.BlockSpec((N_EXPERTS, 1), lambda i: (0, 0))],
        scratch_shapes=[pltpu.VMEM((N_EXPERTS, 1), F32)],
        compiler_params=_cparams("arbitrary"),
        name="post_attn",
    )(*ins)


def _dest_kernel(idx_ref, rank_ref, ps_ref, o_ref):
    idx = idx_ref[...]
    tm = idx.shape[1]
    eio = lax.broadcasted_iota(I32, (N_EXPERTS, tm), 0)
    ps = ps_ref[...]
    rows = [jnp.sum(jnp.where(eio == idx[k:k + 1, :], ps, 0.0), axis=0, keepdims=True)
            for k in range(TOP_K)]
    o_ref[...] = jnp.concatenate(rows, axis=0).astype(I32) + rank_ref[...]


def _dest(idx, rank, pstart):
    K, T = idx.shape
    tm = min(2048, T)
    col = pl.BlockSpec((K, tm), lambda i: (0, i))
    return pl.pallas_call(
        _dest_kernel,
        out_shape=jax.ShapeDtypeStruct((K, T), I32),
        grid=(T // tm,),
        in_specs=[col, col, pl.BlockSpec((N_EXPERTS, 1), lambda i: (0, 0))],
        out_specs=col,
        compiler_params=_cparams("parallel"),
        name="dest",
    )(idx, rank, pstart.astype(F32).reshape(N_EXPERTS, 1))


def _dispatch_kernel(bs_ref, cnt_ref, dest_ref, h_ref, xs_ref, zeros, sem, zsem):
    tm = h_ref.shape[0] // RT
    blk = BM * RT
    nb = xs_ref.shape[0] // blk

    @pl.when(pl.program_id(0) == 0)
    def _():
        zeros[...] = jnp.zeros_like(zeros)

        def zcopy(g):
            return pltpu.make_async_copy(zeros, xs_ref.at[pl.ds(pl.multiple_of(g * blk, blk), blk), :], zsem)

        def has_pad(e):
            return cnt_ref[e] % BM != 0

        def fill(e, _):
            @pl.when(has_pad(e))
            def _():
                zcopy(bs_ref[e + 1] - 1).start()
            return 0

        def drain(e, _):
            @pl.when(has_pad(e))
            def _():
                zcopy(0).wait()
            return 0

        lax.fori_loop(0, N_EXPERTS, fill, 0)
        lax.fori_loop(bs_ref[N_EXPERTS], nb, lambda g, _: (zcopy(g).start(), 0)[1], 0)
        lax.fori_loop(0, N_EXPERTS, drain, 0)
        lax.fori_loop(bs_ref[N_EXPERTS], nb, lambda g, _: (zcopy(0).wait(), 0)[1], 0)

    def tile(ref, r):
        return ref.at[pl.ds(pl.multiple_of(r * RT, RT), RT), :]

    def issue(t, _):
        for k in range(TOP_K):
            pltpu.make_async_copy(tile(h_ref, t), tile(xs_ref, dest_ref[k, t]), sem).start()
        return 0

    lax.fori_loop(0, tm, issue, 0)
    for _ in range(TOP_K):
        pltpu.make_async_copy(h_ref, xs_ref.at[pl.ds(0, tm * RT), :], sem).wait()


def _dispatch(h2rt, dest, bstart, cnt, P):
    T = h2rt.shape[0] // RT
    tm = min(TM_DISP, T)
    nt = T // tm
    dest3 = dest.reshape(TOP_K, nt, tm).transpose(1, 0, 2)
    return pl.pallas_call(
        _dispatch_kernel,
        out_shape=jax.ShapeDtypeStruct((P * RT, LANE), F32),
        grid_spec=pltpu.PrefetchScalarGridSpec(
            num_scalar_prefetch=2,
            grid=(nt,),
            in_specs=[pl.BlockSpec((None, TOP_K, tm), lambda i, bs, cn: (i, 0, 0), memory_space=pltpu.SMEM),
                      pl.BlockSpec((tm * RT, LANE), lambda i, bs, cn: (i, 0))],
            out_specs=pl.BlockSpec(memory_space=pl.ANY),
            scratch_shapes=[pltpu.VMEM((BM * RT, LANE), F32), pltpu.SemaphoreType.DMA(()),
                            pltpu.SemaphoreType.DMA(())],
        ),
        compiler_params=_cparams("arbitrary"),
        name="dispatch",
    )(bstart, cnt, dest3, h2rt)


NBUF_X = 6
X_AHEAD = NBUF_X - 2
NBUF_Y = 4


def _experts_kernel(bs_ref, wg_ref, wu_ref, wd_ref, xs_ref, ys_ref, wgb, wub, wdb, xbuf, ybuf, xsem, ysem):
    e = pl.program_id(0)
    blk = BM * RT
    nb = xs_ref.shape[0] // blk
    nused = bs_ref[N_EXPERTS]
    g0 = bs_ref[e]
    g1 = bs_ref[e + 1]

    def rows(g):
        return pl.ds(pl.multiple_of(g * blk, blk), blk)

    def x_copy(g, slot):
        return pltpu.make_async_copy(xs_ref.at[rows(g), :], xbuf.at[slot], xsem.at[slot])

    def y_copy(g, slot):
        return pltpu.make_async_copy(ybuf.at[slot], ys_ref.at[rows(g), :], ysem.at[slot])

    @pl.when(e == 0)
    def _():
        for j in range(X_AHEAD):
            @pl.when(j < nused)
            def _():
                x_copy(j, j).start()

    def step(gs):
        for g in gs:
            x_copy(g, g % NBUF_X).wait()
            nxt = g + X_AHEAD

            @pl.when(nxt < nused)
            def _():
                x_copy(nxt, nxt % NBUF_X).start()

        x = jnp.concatenate([_rt_load(xbuf, BM, g % NBUF_X) for g in gs], axis=0).astype(BF16)
        gate = _dot(x, wgb[...])
        up = _dot(x, wub[...])
        a = (gate * _sigmoid(gate)) * up
        y = _dot(a.astype(BF16), wdb[...])
        for j, g in enumerate(gs):
            ys = g % NBUF_Y

            @pl.when(g >= NBUF_Y)
            def _():
                y_copy(g - NBUF_Y, ys).wait()

            _rt_store(ybuf, y[j * BM:(j + 1) * BM], ys)
            y_copy(g, ys).start()

    @pl.when(g1 > g0)
    def _():
        wgb[...] = wg_ref[...].astype(BF16)
        wub[...] = wu_ref[...].astype(BF16)
        wdb[...] = wd_ref[...].astype(BF16)

        def pair(i, _):
            g = g0 + 2 * i
            step([g, g + 1])
            return 0

        lax.fori_loop(0, (g1 - g0) // 2, pair, 0)

        @pl.when((g1 - g0) % 2 == 1)
        def _():
            step([g1 - 1])

    @pl.when(e == N_EXPERTS - 1)
    def _():
        for j in range(NBUF_Y):
            @pl.when(nused - 1 - j >= 0)
            def _():
                y_copy(0, (nused - 1 - j) % NBUF_Y).wait()
        ybuf[0] = jnp.zeros(ybuf.shape[1:], F32)
        lax.fori_loop(nused, nb, lambda g, _: (y_copy(g, 0).start(), 0)[1], 0)
        lax.fori_loop(nused, nb, lambda g, _: (y_copy(0, 0).wait(), 0)[1], 0)


def _experts(xs, bstart, w_gate, w_up, w_down):
    E, D, F = w_gate.shape
    return pl.pallas_call(
        _experts_kernel,
        out_shape=jax.ShapeDtypeStruct(xs.shape, F32),
        grid_spec=pltpu.PrefetchScalarGridSpec(
            num_scalar_prefetch=1,
            grid=(E,),
            in_specs=[pl.BlockSpec((None, D, F), lambda e, bs: (e, 0, 0)),
                      pl.BlockSpec((None, D, F), lambda e, bs: (e, 0, 0)),
                      pl.BlockSpec((None, F, D), lambda e, bs: (e, 0, 0)),
                      pl.BlockSpec(memory_space=pl.ANY)],
            out_specs=pl.BlockSpec(memory_space=pl.ANY),
            scratch_shapes=[pltpu.VMEM((D, F), BF16), pltpu.VMEM((D, F), BF16), pltpu.VMEM((F, D), BF16),
                            pltpu.VMEM((NBUF_X, BM * RT, LANE), F32), pltpu.VMEM((NBUF_Y, BM * RT, LANE), F32),
                            pltpu.SemaphoreType.DMA((NBUF_X,)), pltpu.SemaphoreType.DMA((NBUF_Y,))],
        ),
        compiler_params=_cparams("arbitrary"),
        name="experts",
    )(bstart, w_gate, w_up, w_down, xs)


def _combine_kernel(dcur_ref, dnxt_ref, w_ref, x1_ref, h_ref, mod_ref, pg_ref, sg_ref, su_ref, sd_ref,
                    ys_ref, o_ref, rows, sem):
    i = pl.program_id(0)
    n = pl.num_programs(0)
    tm = x1_ref.shape[0]
    slot = i % 2

    def tile(r):
        return pl.ds(pl.multiple_of(r * RT, RT), RT)

    def gather(dref, s):
        def issue(t, _):
            for k in range(TOP_K):
                pltpu.make_async_copy(ys_ref.at[tile(dref[k, t]), :],
                                      rows.at[s, k, tile(t), :], sem.at[s]).start()
            return 0
        lax.fori_loop(0, tm, issue, 0)

    @pl.when(i == 0)
    def _():
        gather(dcur_ref, 0)

    @pl.when(i + 1 < n)
    def _():
        gather(dnxt_ref, 1 - slot)

    for k in range(TOP_K):
        pltpu.make_async_copy(ys_ref.at[pl.ds(0, tm * RT), :], rows.at[slot, k], sem.at[slot]).wait()

    w = w_ref[...]
    routed = w[:, 0:1] * _rt_load(rows, tm, slot, 0)
    for k in range(1, TOP_K):
        routed = routed + w[:, k:k + 1] * _rt_load(rows, tm, slot, k)
    hb = _rt_load(h_ref, tm).astype(BF16)
    g = _dot(hb, sg_ref[...])
    u = _dot(hb, su_ref[...])
    shared = _dot(((g * _sigmoid(g)) * u).astype(BF16), sd_ref[...])
    y = routed + shared
    g_f = mod_ref[5:6, :]
    o_ref[...] = x1_ref[...] + g_f * (_rms(y, NORM_EPS) * pg_ref[...])


def _combine(dest, wts, x1, h2, mod3, ffn_post_g, sw_gate, sw_up, sw_down, ys, S):
    T, D = x1.shape
    tm = min(TM_COMB, S)
    nt = T // tm
    tpb = S // tm
    dest3 = dest.reshape(TOP_K, nt, tm).transpose(1, 0, 2)
    w_tk = wts.T
    full = lambda arr: pl.BlockSpec(arr.shape, lambda i: (0,) * arr.ndim)
    row = lambda w: pl.BlockSpec((tm, w), lambda i: (i, 0))
    sg, su, sd = sw_gate.astype(BF16), sw_up.astype(BF16), sw_down.astype(BF16)
    pg = ffn_post_g.reshape(1, D)
    return pl.pallas_call(
        _combine_kernel,
        out_shape=jax.ShapeDtypeStruct((T, D), F32),
        grid=(nt,),
        in_specs=[pl.BlockSpec((None, TOP_K, tm), lambda i: (i, 0, 0), memory_space=pltpu.SMEM),
                  pl.BlockSpec((None, TOP_K, tm), lambda i: (jnp.minimum(i + 1, nt - 1), 0, 0),
                               memory_space=pltpu.SMEM),
                  row(TOP_K), row(D), pl.BlockSpec((tm * RT, LANE), lambda i: (i, 0)),
                  pl.BlockSpec((None, 6, D), lambda i: (i // tpb, 0, 0)),
                  full(pg), full(sg), full(su), full(sd),
                  pl.BlockSpec(memory_space=pl.ANY)],
        out_specs=row(D),
        scratch_shapes=[pltpu.VMEM((2, TOP_K, tm * RT, LANE), F32), pltpu.SemaphoreType.DMA((2,))],
        compiler_params=_cparams("arbitrary"),
        name="combine",
    )(dest3, dest3, w_tk, x1, h2, mod3, pg, sg, su, sd, ys)


def _moe(h2, x1, idx, wts, rank, counts, mod3, ffn_post_g, exp_w_gate, exp_w_up, exp_w_down,
         sw_gate, sw_up, sw_down, S):
    T, D = x1.shape
    A = T * TOP_K
    P = A + N_EXPERTS * BM
    cnt = counts.reshape(N_EXPERTS)
    blocks = (cnt + BM - 1) // BM
    bstart = jnp.concatenate([jnp.zeros((1,), I32), jnp.cumsum(blocks).astype(I32)])
    pstart = bstart[:-1] * BM

    dest = _dest(idx, rank, pstart)
    xs = _dispatch(h2, dest, bstart, cnt, P)
    ys = _experts(xs, bstart, exp_w_gate, exp_w_up, exp_w_down)
    return _combine(dest, wts, x1, h2, mod3, ffn_post_g, sw_gate, sw_up, sw_down, ys, S)


def _layer(x, c, positions, lambda_init, w_ada, b_ada, attn_pre_g, attn_post_g, w_in, q_norm_g, kv_norm_g,
           w_uq, w_ukv, lam_q1, lam_k1, lam_q2, lam_k2, diff_subln_g, w_o, ffn_pre_g, ffn_post_g,
           router_w, router_b, exp_w_gate, exp_w_up, exp_w_down, sw_gate, sw_up, sw_down):
    B, S, D = x.shape
    T = B * S
    x2 = x.reshape(T, D)
    posf = positions.astype(F32)
    pos_col = posf.reshape(T, 1)
    pos_row = posf.reshape(B, 1, S)

    mod3 = _ada(c, w_ada, b_ada).reshape(B, 6, D)
    q, k, v, dq, dk, dv = _pre_attn(x2, mod3, pos_col, attn_pre_g, w_in, q_norm_g, kv_norm_g, w_uq, w_ukv, S)
    mla = _mla_attn(q, k, v, B, S)
    lam4 = jnp.stack([lam_q1, lam_k1, lam_q2, lam_k2])
    dif = _diff_attn(dq, dk, dv, pos_col, pos_row, lam4, diff_subln_g, lambda_init, B, S)
    x1, h2, idx, wts, rank, counts = _post_attn(x2, mla, dif, mod3, w_o, attn_post_g, ffn_pre_g,
                                                router_w, router_b, S)
    out = _moe(h2, x1, idx, wts, rank, counts, mod3, ffn_post_g, exp_w_gate, exp_w_up, exp_w_down,
               sw_gate, sw_up, sw_down, S)
    return out.reshape(B, S, D)


def kernel(x, c, positions, w_ada, b_ada, attn_pre_g, attn_post_g, w_in, q_norm_g, kv_norm_g, w_uq, w_ukv,
           lam_q1, lam_k1, lam_q2, lam_k2, diff_subln_g, w_o, ffn_pre_g, ffn_post_g, router_w, router_b,
           exp_w_gate, exp_w_up, exp_w_down, shared_w_gate, shared_w_up, shared_w_down):
    depth = w_ada.shape[0]
    for l in range(depth):
        lambda_init = 0.8 - 0.6 * math.exp(-0.3 * l)
        x = _layer(x, c, positions, lambda_init, w_ada[l], b_ada[l], attn_pre_g[l], attn_post_g[l], w_in[l],
                   q_norm_g[l], kv_norm_g[l], w_uq[l], w_ukv[l], lam_q1[l], lam_k1[l], lam_q2[l], lam_k2[l],
                   diff_subln_g[l], w_o[l], ffn_pre_g[l], ffn_post_g[l], router_w[l], router_b[l],
                   exp_w_gate[l], exp_w_up[l], exp_w_down[l], shared_w_gate[l], shared_w_up[l],
                   shared_w_down[l])
    return x
```

```python
import functools
import math

import jax
import jax.numpy as jnp
from jax import lax
from jax.experimental import pallas as pl
from jax.experimental.pallas import tpu as pltpu

F32 = jnp.float32
BF16 = jnp.bfloat16
I32 = jnp.int32

MLA_HEADS = 8
MLA_NOPE = 64
MLA_ROPE = 32
MLA_V = 64
MLA_Q_RANK = 256
MLA_KV_RANK = 128
ROPE_BASE = 10000.0
DIFF_HEADS = 4
DIFF_HD = 64
DIFF_V = 128
N_EXPERTS = 256
TOP_K = 8
N_GROUPS = 8
GROUP_SIZE = N_EXPERTS // N_GROUPS
TOPK_GROUPS = 4
ROUTED_SCALE = 2.5
NORM_EPS = 1e-6
SUBLN_EPS = 1e-5
LOG2E = 1.4426950408889634

LANE = 128
HEAD_PAD = 128

TM_PRE = 512
TQ = 512
TK = 512
TM_POST = 512
TM_DISP = 256
TM_COMB = 128
BM = 128
VMEM_LIMIT = 48 * 1024 * 1024


def _cparams(*sem):
    return pltpu.CompilerParams(dimension_semantics=sem, vmem_limit_bytes=VMEM_LIMIT)


def _rms(x, eps):
    return x * lax.rsqrt(jnp.mean(x * x, axis=-1, keepdims=True) + eps)


def _sigmoid(x):
    return 1.0 / (1.0 + jnp.exp(-x))


def _dot(a, b):
    return jnp.dot(a, b, preferred_element_type=F32)


RT = 8


def _rt_load(ref, n, *lead):
    return jnp.concatenate([ref[(*lead, pl.ds(j, n, stride=RT), slice(None))] for j in range(RT)], axis=1)


def _rt_store(ref, val, *lead):
    n = val.shape[0]
    for j in range(RT):
        ref[(*lead, pl.ds(j, n, stride=RT), slice(None))] = val[:, j * LANE:(j + 1) * LANE]


def _dot_nt(a, b):
    return lax.dot_general(a, b, (((1,), (1,)), ((), ())), preferred_element_type=F32)


def _ada_kernel(c_ref, w_ref, b_ref, o_ref):
    c = c_ref[...]
    a = c * _sigmoid(c)
    o_ref[...] = jnp.dot(a, w_ref[...], preferred_element_type=F32,
                         precision=lax.Precision.HIGHEST) + b_ref[...]


def _ada(c, w_ada, b_ada):
    B, D = c.shape
    n = w_ada.shape[1]
    return pl.pallas_call(
        _ada_kernel,
        out_shape=jax.ShapeDtypeStruct((B, n), F32),
        grid=(n // D,),
        in_specs=[pl.BlockSpec((B, D), lambda j: (0, 0)),
                  pl.BlockSpec((D, D), lambda j: (0, j)),
                  pl.BlockSpec((1, D), lambda j: (0, j))],
        out_specs=pl.BlockSpec((B, D), lambda j: (0, j)),
        compiler_params=_cparams("arbitrary"),
        name="ada",
    )(c, w_ada, b_ada.reshape(1, n))


_C_CQ = 0
_C_CKV = _C_CQ + MLA_Q_RANK
_C_KRA = _C_CKV + MLA_KV_RANK
_C_KRB = _C_KRA + LANE
_C_DQ = _C_KRB + LANE
_C_DK = _C_DQ + DIFF_HEADS * DIFF_V
_C_DV = _C_DK + DIFF_HEADS * DIFF_V
_C_END = _C_DV + DIFF_HEADS * DIFF_V


def _pre_attn_kernel(x_ref, mod_ref, pos_ref, inv_ref, g_ref, w1_ref, qg_ref, kvg_ref,
                     wqa_ref, wqb_ref, wkn_ref, wv_ref,
                     q_ref, k_ref, v_ref, dq_ref, dk_ref, dv_ref):
    x = x_ref[...]
    sh = mod_ref[0:1, :]
    sc = mod_ref[1:2, :]
    h = _rms(x, NORM_EPS) * g_ref[...]
    h = h * (1.0 + sc) + sh
    p = _dot(h.astype(BF16), w1_ref[...])

    ang = pos_ref[...] * inv_ref[...]
    lane = lax.broadcasted_iota(I32, ang.shape, 1)
    in_rope = (lane >= MLA_NOPE) & (lane < MLA_NOPE + MLA_ROPE)
    cos_r = jnp.where(in_rope, jnp.cos(ang), 0.0)
    sin_r = jnp.where(in_rope, jnp.sin(ang), 0.0)
    cos_q = jnp.where(lane < MLA_NOPE, 1.0, cos_r)

    cqn = (_rms(p[:, _C_CQ:_C_CKV], NORM_EPS) * qg_ref[...]).astype(BF16)
    qa = _dot(cqn, wqa_ref[...])
    qb = _dot(cqn, wqb_ref[...])
    q_scale = LOG2E / math.sqrt(MLA_NOPE + MLA_ROPE)
    cos_t = jnp.concatenate([cos_q] * MLA_HEADS, axis=1)
    sin_t = jnp.concatenate([sin_r] * MLA_HEADS, axis=1)
    q_ref[...] = ((qa * cos_t + qb * sin_t) * q_scale).astype(BF16)

    ckvn = (_rms(p[:, _C_CKV:_C_KRA], NORM_EPS) * kvg_ref[...]).astype(BF16)
    kn = _dot(ckvn, wkn_ref[...])
    kr = p[:, _C_KRA:_C_KRB] * cos_r + p[:, _C_KRB:_C_DQ] * sin_r
    k_ref[...] = (kn + jnp.concatenate([kr] * MLA_HEADS, axis=1)).astype(BF16)
    lane_t = lax.broadcasted_iota(I32, kn.shape, 1)
    ones_col = jnp.where(lane_t % HEAD_PAD == MLA_V, 1.0, 0.0)
    v_ref[...] = (_dot(ckvn, wv_ref[...]) + ones_col).astype(BF16)

    dq_ref[...] = (p[:, _C_DQ:_C_DK] * (LOG2E / math.sqrt(DIFF_HD))).astype(BF16)
    dk_ref[...] = p[:, _C_DK:_C_DV].astype(BF16)
    dv_ref[...] = p[:, _C_DV:_C_END].astype(BF16)


def _pre_attn(x2, mod3, pos_col, attn_pre_g, w_in, q_norm_g, kv_norm_g, w_uq, w_ukv, S):
    T, D = x2.shape
    tm = min(TM_PRE, S)
    tpb = S // tm
    f = lambda a: a.astype(BF16)
    d3 = DIFF_HEADS * DIFF_V
    a = MLA_Q_RANK
    b = a + MLA_KV_RANK
    c = b + MLA_ROPE

    def swap(r):
        hlf = MLA_ROPE // 2
        return jnp.concatenate([-r[..., hlf:], r[..., :hlf]], axis=-1)

    def pad_rope(r):
        return jnp.pad(r, ((0, 0), (MLA_NOPE, LANE - MLA_NOPE - MLA_ROPE)))

    w_kr = w_in[:, b:c]
    w1 = jnp.concatenate([w_in[:, :b], pad_rope(w_kr), pad_rope(swap(w_kr)), w_in[:, c:]], axis=1)
    assert w1.shape[1] == _C_END
    padq = HEAD_PAD - MLA_NOPE - MLA_ROPE
    wqa = jnp.pad(w_uq, ((0, 0), (0, 0), (0, padq))).reshape(MLA_Q_RANK, MLA_HEADS * HEAD_PAD)
    q_rope = w_uq[..., MLA_NOPE:]
    wqb = jnp.pad(swap(q_rope), ((0, 0), (0, 0), (MLA_NOPE, padq))).reshape(MLA_Q_RANK, MLA_HEADS * HEAD_PAD)
    wkn = jnp.pad(w_ukv[..., :MLA_NOPE], ((0, 0), (0, 0), (0, HEAD_PAD - MLA_NOPE))).reshape(
        MLA_KV_RANK, MLA_HEADS * HEAD_PAD)
    wv = jnp.pad(w_ukv[..., MLA_NOPE:], ((0, 0), (0, 0), (0, HEAD_PAD - MLA_V))).reshape(
        MLA_KV_RANK, MLA_HEADS * HEAD_PAD)

    inv = 1.0 / (ROPE_BASE ** (jnp.arange(0, MLA_ROPE, 2, dtype=F32) / MLA_ROPE))
    inv_lane = jnp.pad(jnp.concatenate([inv, inv]), (MLA_NOPE, LANE - MLA_NOPE - MLA_ROPE)).reshape(1, LANE)

    full = lambda arr: pl.BlockSpec(arr.shape, lambda i: (0,) * arr.ndim)
    row = lambda w: pl.BlockSpec((tm, w), lambda i: (i, 0))
    ins = [x2, mod3, pos_col, inv_lane, attn_pre_g.reshape(1, D), f(w1), q_norm_g.reshape(1, -1),
           kv_norm_g.reshape(1, -1), f(wqa), f(wqb), f(wkn), f(wv)]
    in_specs = [row(D), pl.BlockSpec((None, 6, D), lambda i: (i // tpb, 0, 0)), row(1)] + \
               [full(arr) for arr in ins[3:]]
    widths = [MLA_HEADS * HEAD_PAD, MLA_HEADS * HEAD_PAD, MLA_HEADS * HEAD_PAD, d3, d3, d3]
    return pl.pallas_call(
        _pre_attn_kernel,
        out_shape=[jax.ShapeDtypeStruct((T, w), BF16) for w in widths],
        grid=(T // tm,),
        in_specs=in_specs,
        out_specs=[row(w) for w in widths],
        compiler_params=_cparams("parallel"),
        name="pre_attn",
    )(*ins)


def _online_update(s, m, l, acc, v):
    m_new = jnp.maximum(m, jnp.max(s, axis=-1, keepdims=True))
    p = jnp.exp2(s - m_new)
    alpha = jnp.exp2(m - m_new)
    l_new = alpha * l + jnp.sum(p, axis=-1, keepdims=True)
    acc_new = alpha * acc + _dot(p.astype(BF16), v)
    return m_new, l_new, acc_new


MLA_HPS = 4


def _causal_blocks(i, tq, tk):
    nfull = (i * tq) // tk
    diag = [(nfull + j, i * tq - (nfull + j) * tk) for j in range(max(1, tq // tk))]
    return nfull, diag


def _keep(tq, tk, off):
    r = lax.broadcasted_iota(I32, (tq, tk), 0)
    c = lax.broadcasted_iota(I32, (tq, tk), 1)
    return r + off >= c


def _mla_attn_kernel(tk, q_ref, k_ref, v_ref, o_ref):
    i = pl.program_id(2)
    tq = q_ref.shape[0]
    sl = [slice(j * HEAD_PAD, (j + 1) * HEAD_PAD) for j in range(MLA_HPS)]
    qs = [q_ref[:, sl[j]] for j in range(MLA_HPS)]

    def step(kb, carry, off=None):
        r0 = pl.multiple_of(kb * tk, tk)
        out = []
        for j in range(MLA_HPS):
            m, acc = carry[j]
            s = _dot_nt(qs[j], k_ref[pl.ds(r0, tk), sl[j]])
            if off is not None:
                s = jnp.where(_keep(tq, tk, off), s, -jnp.inf)
            m_new = jnp.maximum(m, jnp.max(s, axis=-1, keepdims=True))
            p = jnp.exp2(s - m_new)
            acc = jnp.exp2(m - m_new) * acc + _dot(p.astype(BF16), v_ref[pl.ds(r0, tk), sl[j]])
            out.append((m_new, acc))
        return tuple(out)

    nfull, diag = _causal_blocks(i, tq, tk)
    init = tuple((jnp.full((tq, 1), -jnp.inf, F32), jnp.zeros((tq, HEAD_PAD), F32)) for _ in range(MLA_HPS))
    carry = lax.fori_loop(0, nfull, step, init)
    for kb, off in diag:
        carry = step(kb, carry, off)
    outs = [acc[:, :MLA_V] / acc[:, MLA_V:MLA_V + 1] for _, acc in carry]
    o_ref[...] = jnp.concatenate(outs, axis=1).astype(o_ref.dtype)


def _mla_attn(q, k, v, B, S):
    tq = min(TQ, S)
    tk = min(TK, S)
    q3 = q.reshape(B, S, -1)
    k3 = k.reshape(B, S, -1)
    v3 = v.reshape(B, S, -1)
    w = MLA_HPS * HEAD_PAD
    return pl.pallas_call(
        functools.partial(_mla_attn_kernel, tk),
        out_shape=jax.ShapeDtypeStruct((B, S, MLA_HEADS * MLA_V), BF16),
        grid=(B, MLA_HEADS // MLA_HPS, S // tq),
        in_specs=[pl.BlockSpec((None, tq, w), lambda b, h, i: (b, i, h)),
                  pl.BlockSpec((None, S, w), lambda b, h, i: (b, 0, h)),
                  pl.BlockSpec((None, S, w), lambda b, h, i: (b, 0, h))],
        out_specs=pl.BlockSpec((None, tq, MLA_HPS * MLA_V), lambda b, h, i: (b, i, h)),
        compiler_params=_cparams("parallel", "parallel", "arbitrary"),
        name="mla_attn",
    )(q3, k3, v3)


DIFF_HPS = 2


def _diff_attn_kernel(lambda_init, tk, q_ref, k_ref, v_ref, pc_ref, pr_ref, lam_ref, g_ref, o_ref):
    i = pl.program_id(2)
    tq = q_ref.shape[0]
    sl = [slice(j * DIFF_V, (j + 1) * DIFF_V) for j in range(DIFF_HPS)]
    lane = lax.broadcasted_iota(I32, (tq, DIFF_V), 1)
    qs, nslopes = [], []
    for j in range(DIFF_HPS):
        q = q_ref[:, sl[j]]
        zero = jnp.zeros_like(q)
        qs.append((jnp.where(lane < DIFF_HD, q, zero), jnp.where(lane >= DIFF_HD, q, zero)))
        hv = jnp.full((1, 1), pl.program_id(1) * DIFF_HPS + j, I32).astype(F32)
        nslopes.append(-LOG2E * jnp.exp2(-8.0 * (hv + 1.0) / DIFF_HEADS))
    pq = pc_ref[...]

    def step(kb, carry, off=None):
        r0 = pl.multiple_of(kb * tk, tk)
        dist = jnp.abs(pq - pr_ref[:, pl.ds(r0, tk)])
        keep = None if off is None else _keep(tq, tk, off)
        out = []
        for j in range(DIFF_HPS):
            kblk = k_ref[pl.ds(r0, tk), sl[j]]
            vblk = v_ref[pl.ds(r0, tk), sl[j]]
            bias = nslopes[j] * dist
            for c in range(2):
                s = _dot_nt(qs[j][c], kblk) + bias
                if keep is not None:
                    s = jnp.where(keep, s, -jnp.inf)
                out.append(_online_update(s, *carry[2 * j + c], vblk))
        return tuple(out)

    nfull, diag = _causal_blocks(i, tq, tk)
    init1 = (jnp.full((tq, 1), -jnp.inf, F32), jnp.zeros((tq, 1), F32), jnp.zeros((tq, DIFF_V), F32))
    carry = lax.fori_loop(0, nfull, step, (init1,) * (2 * DIFF_HPS))
    for kb, off in diag:
        carry = step(kb, carry, off)

    lv = lam_ref[...]
    lam = (jnp.exp(jnp.sum(lv[0:1] * lv[1:2], axis=-1, keepdims=True))
           - jnp.exp(jnp.sum(lv[2:3] * lv[3:4], axis=-1, keepdims=True)) + lambda_init)
    outs = []
    for j in range(DIFF_HPS):
        (_, l1, a1), (_, l2, a2) = carry[2 * j], carry[2 * j + 1]
        o = a1 / l1 - lam * (a2 / l2)
        outs.append(_rms(o, SUBLN_EPS) * g_ref[...] * (1.0 - lambda_init))
    o_ref[...] = jnp.concatenate(outs, axis=1).astype(o_ref.dtype)


def _diff_attn(dq, dk, dv, pos_col, pos_row, lam4, subln_g, lambda_init, B, S):
    tq = min(TQ, S)
    tk = min(TK, S)
    nq = S // tq
    q3 = dq.reshape(B, S, -1)
    k3 = dk.reshape(B, S, -1)
    v3 = dv.reshape(B, S, -1)
    w = DIFF_HPS * DIFF_V
    return pl.pallas_call(
        functools.partial(_diff_attn_kernel, lambda_init, tk),
        out_shape=jax.ShapeDtypeStruct((B, S, DIFF_HEADS * DIFF_V), BF16),
        grid=(B, DIFF_HEADS // DIFF_HPS, nq),
        in_specs=[pl.BlockSpec((None, tq, w), lambda b, h, i: (b, i, h)),
                  pl.BlockSpec((None, S, w), lambda b, h, i: (b, 0, h)),
                  pl.BlockSpec((None, S, w), lambda b, h, i: (b, 0, h)),
                  pl.BlockSpec((tq, 1), lambda b, h, i: (b * nq + i, 0)),
                  pl.BlockSpec((None, 1, S), lambda b, h, i: (b, 0, 0)),
                  pl.BlockSpec((4, DIFF_HD), lambda b, h, i: (0, 0)),
                  pl.BlockSpec((1, DIFF_V), lambda b, h, i: (0, 0))],
        out_specs=pl.BlockSpec((None, tq, w), lambda b, h, i: (b, i, h)),
        compiler_params=_cparams("parallel", "parallel", "arbitrary"),
        name="diff_attn",
    )(q3, k3, v3, pos_col, pos_row, lam4, subln_g.reshape(1, DIFF_V))


def _first_argmax(v, io, n, axis):
    m = jnp.max(v, axis=axis, keepdims=True)
    ix = jnp.min(jnp.where(v == m, io, n), axis=axis, keepdims=True)
    return m, ix


def _post_attn_kernel(x_ref, mla_ref, dif_ref, mod_ref, woa_ref, wob_ref, pg_ref, fg_ref,
                      rwt_ref, rb_ref,
                      x1_ref, h2_ref, idx_ref, wts_ref, rank_ref, cnt_ref, run_ref):
    step = pl.program_id(0)
    tm = x_ref.shape[0]
    E, G, GS = N_EXPERTS, N_GROUPS, GROUP_SIZE

    @pl.when(step == 0)
    def _():
        run_ref[...] = jnp.zeros_like(run_ref)

    g_a = mod_ref[2:3, :]
    sh_f = mod_ref[3:4, :]
    sc_f = mod_ref[4:5, :]
    y = _dot(mla_ref[...], woa_ref[...]) + _dot(dif_ref[...], wob_ref[...])
    x1 = x_ref[...] + g_a * (_rms(y, NORM_EPS) * pg_ref[...])
    x1_ref[...] = x1
    h2 = (_rms(x1, NORM_EPS) * fg_ref[...]) * (1.0 + sc_f) + sh_f
    _rt_store(h2_ref, h2)

    logits = lax.dot_general(rwt_ref[...], h2, (((1,), (1,)), ((), ())),
                             preferred_element_type=F32, precision=lax.Precision.HIGHEST)
    scores = _sigmoid(logits)
    sel = scores + rb_ref[...]

    sio = lax.broadcasted_iota(I32, (GS, tm), 0)
    gs_rows = []
    for g in range(G):
        blk = sel[g * GS:(g + 1) * GS, :]
        m1, i1 = _first_argmax(blk, sio, GS, 0)
        m2 = jnp.max(jnp.where(sio == i1, -jnp.inf, blk), axis=0, keepdims=True)
        gs_rows.append(m1 + m2)
    gs = jnp.concatenate(gs_rows, axis=0)

    gio = lax.broadcasted_iota(I32, (G, tm), 0)
    gkeep = jnp.zeros((G, tm), F32)
    for _ in range(TOPK_GROUPS):
        _, ix = _first_argmax(gs, gio, G, 0)
        pick = gio == ix
        gkeep = jnp.where(pick, 1.0, gkeep)
        gs = jnp.where(pick, -jnp.inf, gs)
    ekeep = jnp.concatenate([jnp.broadcast_to(gkeep[g:g + 1, :], (GS, tm)) for g in range(G)], axis=0)
    cand = jnp.where(ekeep > 0.0, sel, -jnp.inf)

    eio = lax.broadcasted_iota(I32, (E, tm), 0)
    idx_rows, w_rows = [], []
    for _ in range(TOP_K):
        _, ix = _first_argmax(cand, eio, E, 0)
        pick = eio == ix
        w_rows.append(jnp.sum(jnp.where(pick, scores, 0.0), axis=0, keepdims=True))
        cand = jnp.where(pick, -jnp.inf, cand)
        idx_rows.append(ix)
    idx = jnp.concatenate(idx_rows, axis=0)
    w = jnp.concatenate(w_rows, axis=0)
    wts_ref[...] = w / jnp.sum(w, axis=0, keepdims=True) * ROUTED_SCALE
    idx_ref[...] = idx

    onehot = jnp.zeros((E, tm), F32)
    for k in range(TOP_K):
        onehot = onehot + jnp.where(eio == idx_rows[k], 1.0, 0.0)
    tr = lax.broadcasted_iota(I32, (tm, tm), 0)
    tc = lax.broadcasted_iota(I32, (tm, tm), 1)
    before = jnp.where(tr < tc, 1.0, 0.0).astype(BF16)
    prior = _dot(onehot.astype(BF16), before) + run_ref[...]
    rank_rows = [jnp.sum(jnp.where(eio == idx_rows[k], prior, 0.0), axis=0, keepdims=True)
                 for k in range(TOP_K)]
    rank_ref[...] = jnp.concatenate(rank_rows, axis=0).astype(I32)
    run_ref[...] += jnp.sum(onehot, axis=1, keepdims=True)
    cnt_ref[...] = run_ref[...].astype(I32)


def _post_attn(x2, mla, dif, mod3, w_o, attn_post_g, ffn_pre_g, router_w, router_b, S):
    T, D = x2.shape
    tm = min(TM_POST, S)
    tpb = S // tm
    half = MLA_HEADS * MLA_V
    woa = w_o[:half].astype(BF16)
    wob = w_o[half:].astype(BF16)
    rwt = router_w.T
    full = lambda arr: pl.BlockSpec(arr.shape, lambda i: (0,) * arr.ndim)
    row = lambda w: pl.BlockSpec((tm, w), lambda i: (i, 0))
    col = lambda r: pl.BlockSpec((r, tm), lambda i: (0, i))
    ins = [x2, mla.reshape(T, -1), dif.reshape(T, -1), mod3, woa, wob, attn_post_g.reshape(1, D),
           ffn_pre_g.reshape(1, D), rwt, router_b.reshape(N_EXPERTS, 1)]
    in_specs = [row(D), row(half), row(D - half), pl.BlockSpec((None, 6, D), lambda i: (i // tpb, 0, 0))] + \
               [full(arr) for arr in ins[4:]]
    return pl.pallas_call(
        _post_attn_kernel,
        out_shape=[jax.ShapeDtypeStruct((T, D), F32), jax.ShapeDtypeStruct((T * RT, LANE), F32),
                   jax.ShapeDtypeStruct((TOP_K, T), I32), jax.ShapeDtypeStruct((TOP_K, T), F32),
                   jax.ShapeDtypeStruct((TOP_K, T), I32), jax.ShapeDtypeStruct((N_EXPERTS, 1), I32)],
        grid=(T // tm,),
        in_specs=in_specs,
        out_specs=[row(D), pl.BlockSpec((tm * RT, LANE), lambda i: (i, 0)), col(TOP_K), col(TOP_K), col(TOP_K),
                   pl.BlockSpec((N_EXPERTS, 1), lambda i: (0, 0))],
        scratch_shapes=[pltpu.VMEM((N_EXPERTS, 1), F32)],
        compiler_params=_cparams("arbitrary"),
        name="post_attn",
    )(*ins)


def _dest_kernel(idx_ref, rank_ref, ps_ref, o_ref):
    idx = idx_ref[...]
    tm = idx.shape[1]
    eio = lax.broadcasted_iota(I32, (N_EXPERTS, tm), 0)
    ps = ps_ref[...]
    rows = [jnp.sum(jnp.where(eio == idx[k:k + 1, :], ps, 0.0), axis=0, keepdims=True)
            for k in range(TOP_K)]
    o_ref[...] = jnp.concatenate(rows, axis=0).astype(I32) + rank_ref[...]


def _dest(idx, rank, pstart):
    K, T = idx.shape
    tm = min(2048, T)
    col = pl.BlockSpec((K, tm), lambda i: (0, i))
    return pl.pallas_call(
        _dest_kernel,
        out_shape=jax.ShapeDtypeStruct((K, T), I32),
        grid=(T // tm,),
        in_specs=[col, col, pl.BlockSpec((N_EXPERTS, 1), lambda i: (0, 0))],
        out_specs=col,
        compiler_params=_cparams("parallel"),
        name="dest",
    )(idx, rank, pstart.astype(F32).reshape(N_EXPERTS, 1))


def _dispatch_kernel(bs_ref, cnt_ref, dest_ref, h_ref, xs_ref, zeros, sem, zsem):
    tm = h_ref.shape[0] // RT
    blk = BM * RT
    nb = xs_ref.shape[0] // blk

    @pl.when(pl.program_id(0) == 0)
    def _():
        zeros[...] = jnp.zeros_like(zeros)

        def zcopy(g):
            return pltpu.make_async_copy(zeros, xs_ref.at[pl.ds(pl.multiple_of(g * blk, blk), blk), :], zsem)

        def has_pad(e):
            return cnt_ref[e] % BM != 0

        def fill(e, _):
            @pl.when(has_pad(e))
            def _():
                zcopy(bs_ref[e + 1] - 1).start()
            return 0

        def drain(e, _):
            @pl.when(has_pad(e))
            def _():
                zcopy(0).wait()
            return 0

        lax.fori_loop(0, N_EXPERTS, fill, 0)
        lax.fori_loop(bs_ref[N_EXPERTS], nb, lambda g, _: (zcopy(g).start(), 0)[1], 0)
        lax.fori_loop(0, N_EXPERTS, drain, 0)
        lax.fori_loop(bs_ref[N_EXPERTS], nb, lambda g, _: (zcopy(0).wait(), 0)[1], 0)

    def tile(ref, r):
        return ref.at[pl.ds(pl.multiple_of(r * RT, RT), RT), :]

    def issue(t, _):
        for k in range(TOP_K):
            pltpu.make_async_copy(tile(h_ref, t), tile(xs_ref, dest_ref[k, t]), sem).start(priority=k % 2)
        return 0

    lax.fori_loop(0, tm, issue, 0)
    for _ in range(TOP_K):
        pltpu.make_async_copy(h_ref, xs_ref.at[pl.ds(0, tm * RT), :], sem).wait()


def _dispatch(h2rt, dest, bstart, cnt, P):
    T = h2rt.shape[0] // RT
    tm = min(TM_DISP, T)
    nt = T // tm
    dest3 = dest.reshape(TOP_K, nt, tm).transpose(1, 0, 2)
    return pl.pallas_call(
        _dispatch_kernel,
        out_shape=jax.ShapeDtypeStruct((P * RT, LANE), F32),
        grid_spec=pltpu.PrefetchScalarGridSpec(
            num_scalar_prefetch=2,
            grid=(nt,),
            in_specs=[pl.BlockSpec((None, TOP_K, tm), lambda i, bs, cn: (i, 0, 0), memory_space=pltpu.SMEM),
                      pl.BlockSpec((tm * RT, LANE), lambda i, bs, cn: (i, 0))],
            out_specs=pl.BlockSpec(memory_space=pl.ANY),
            scratch_shapes=[pltpu.VMEM((BM * RT, LANE), F32), pltpu.SemaphoreType.DMA(()),
                            pltpu.SemaphoreType.DMA(())],
        ),
        compiler_params=_cparams("arbitrary"),
        name="dispatch",
    )(bstart, cnt, dest3, h2rt)


NBUF_X = 6
X_AHEAD = NBUF_X - 2
NBUF_Y = 4


def _experts_kernel(bs_ref, wg_ref, wu_ref, wd_ref, xs_ref, ys_ref, wgb, wub, wdb, xbuf, ybuf, xsem, ysem):
    e = pl.program_id(0)
    blk = BM * RT
    nb = xs_ref.shape[0] // blk
    nused = bs_ref[N_EXPERTS]
    g0 = bs_ref[e]
    g1 = bs_ref[e + 1]

    def rows(g):
        return pl.ds(pl.multiple_of(g * blk, blk), blk)

    def x_copy(g, slot):
        return pltpu.make_async_copy(xs_ref.at[rows(g), :], xbuf.at[slot], xsem.at[slot])

    def y_copy(g, slot):
        return pltpu.make_async_copy(ybuf.at[slot], ys_ref.at[rows(g), :], ysem.at[slot])

    @pl.when(e == 0)
    def _():
        for j in range(X_AHEAD):
            @pl.when(j < nused)
            def _():
                x_copy(j, j).start()

    def step(gs):
        for g in gs:
            x_copy(g, g % NBUF_X).wait()
            nxt = g + X_AHEAD

            @pl.when(nxt < nused)
            def _():
                x_copy(nxt, nxt % NBUF_X).start()

        x = jnp.concatenate([_rt_load(xbuf, BM, g % NBUF_X) for g in gs], axis=0).astype(BF16)
        gate = _dot(x, wgb[...])
        up = _dot(x, wub[...])
        a = (gate * _sigmoid(gate)) * up
        y = _dot(a.astype(BF16), wdb[...])
        for j, g in enumerate(gs):
            ys = g % NBUF_Y

            @pl.when(g >= NBUF_Y)
            def _():
                y_copy(g - NBUF_Y, ys).wait()

            _rt_store(ybuf, y[j * BM:(j + 1) * BM], ys)
            y_copy(g, ys).start()

    @pl.when(g1 > g0)
    def _():
        wgb[...] = wg_ref[...].astype(BF16)
        wub[...] = wu_ref[...].astype(BF16)
        wdb[...] = wd_ref[...].astype(BF16)

        def pair(i, _):
            g = g0 + 2 * i
            step([g, g + 1])
            return 0

        lax.fori_loop(0, (g1 - g0) // 2, pair, 0)

        @pl.when((g1 - g0) % 2 == 1)
        def _():
            step([g1 - 1])

    @pl.when(e == N_EXPERTS - 1)
    def _():
        for j in range(NBUF_Y):
            @pl.when(nused - 1 - j >= 0)
            def _():
                y_copy(0, (nused - 1 - j) % NBUF_Y).wait()
        ybuf[0] = jnp.zeros(ybuf.shape[1:], F32)
        lax.fori_loop(nused, nb, lambda g, _: (y_copy(g, 0).start(), 0)[1], 0)
        lax.fori_loop(nused, nb, lambda g, _: (y_copy(0, 0).wait(), 0)[1], 0)


def _experts(xs, bstart, w_gate, w_up, w_down):
    E, D, F = w_gate.shape
    return pl.pallas_call(
        _experts_kernel,
        out_shape=jax.ShapeDtypeStruct(xs.shape, F32),
        grid_spec=pltpu.PrefetchScalarGridSpec(
            num_scalar_prefetch=1,
            grid=(E,),
            in_specs=[pl.BlockSpec((None, D, F), lambda e, bs: (e, 0, 0)),
                      pl.BlockSpec((None, D, F), lambda e, bs: (e, 0, 0)),
                      pl.BlockSpec((None, F, D), lambda e, bs: (e, 0, 0)),
                      pl.BlockSpec(memory_space=pl.ANY)],
            out_specs=pl.BlockSpec(memory_space=pl.ANY),
            scratch_shapes=[pltpu.VMEM((D, F), BF16), pltpu.VMEM((D, F), BF16), pltpu.VMEM((F, D), BF16),
                            pltpu.VMEM((NBUF_X, BM * RT, LANE), F32), pltpu.VMEM((NBUF_Y, BM * RT, LANE), F32),
                            pltpu.SemaphoreType.DMA((NBUF_X,)), pltpu.SemaphoreType.DMA((NBUF_Y,))],
        ),
        compiler_params=_cparams("arbitrary"),
        name="experts",
    )(bstart, w_gate, w_up, w_down, xs)


def _combine_kernel(dcur_ref, dnxt_ref, w_ref, x1_ref, h_ref, mod_ref, pg_ref, sg_ref, su_ref, sd_ref,
                    ys_ref, o_ref, rows, sem):
    i = pl.program_id(0)
    n = pl.num_programs(0)
    tm = x1_ref.shape[0]
    slot = i % 2

    def tile(r):
        return pl.ds(pl.multiple_of(r * RT, RT), RT)

    def gather(dref, s):
        def issue(t, _):
            for k in range(TOP_K):
                pltpu.make_async_copy(ys_ref.at[tile(dref[k, t]), :],
                                      rows.at[s, k, tile(t), :], sem.at[s]).start(priority=k % 2)
            return 0
        lax.fori_loop(0, tm, issue, 0)

    @pl.when(i == 0)
    def _():
        gather(dcur_ref, 0)

    @pl.when(i + 1 < n)
    def _():
        gather(dnxt_ref, 1 - slot)

    for k in range(TOP_K):
        pltpu.make_async_copy(ys_ref.at[pl.ds(0, tm * RT), :], rows.at[slot, k], sem.at[slot]).wait()

    w = w_ref[...]
    routed = w[:, 0:1] * _rt_load(rows, tm, slot, 0)
    for k in range(1, TOP_K):
        routed = routed + w[:, k:k + 1] * _rt_load(rows, tm, slot, k)
    hb = _rt_load(h_ref, tm).astype(BF16)
    g = _dot(hb, sg_ref[...])
    u = _dot(hb, su_ref[...])
    shared = _dot(((g * _sigmoid(g)) * u).astype(BF16), sd_ref[...])
    y = routed + shared
    g_f = mod_ref[5:6, :]
    o_ref[...] = x1_ref[...] + g_f * (_rms(y, NORM_EPS) * pg_ref[...])


def _combine(dest, wts, x1, h2, mod3, ffn_post_g, sw_gate, sw_up, sw_down, ys, S):
    T, D = x1.shape
    tm = min(TM_COMB, S)
    nt = T // tm
    tpb = S // tm
    dest3 = dest.reshape(TOP_K, nt, tm).transpose(1, 0, 2)
    w_tk = wts.T
    full = lambda arr: pl.BlockSpec(arr.shape, lambda i: (0,) * arr.ndim)
    row = lambda w: pl.BlockSpec((tm, w), lambda i: (i, 0))
    sg, su, sd = sw_gate.astype(BF16), sw_up.astype(BF16), sw_down.astype(BF16)
    pg = ffn_post_g.reshape(1, D)
    return pl.pallas_call(
        _combine_kernel,
        out_shape=jax.ShapeDtypeStruct((T, D), F32),
        grid=(nt,),
        in_specs=[pl.BlockSpec((None, TOP_K, tm), lambda i: (i, 0, 0), memory_space=pltpu.SMEM),
                  pl.BlockSpec((None, TOP_K, tm), lambda i: (jnp.minimum(i + 1, nt - 1), 0, 0),
                               memory_space=pltpu.SMEM),
                  row(TOP_K), row(D), pl.BlockSpec((tm * RT, LANE), lambda i: (i, 0)),
                  pl.BlockSpec((None, 6, D), lambda i: (i // tpb, 0, 0)),
                  full(pg), full(sg), full(su), full(sd),
                  pl.BlockSpec(memory_space=pl.ANY)],
        out_specs=row(D),
        scratch_shapes=[pltpu.VMEM((2, TOP_K, tm * RT, LANE), F32), pltpu.SemaphoreType.DMA((2,))],
        compiler_params=_cparams("arbitrary"),
        name="combine",
    )(dest3, dest3, w_tk, x1, h2, mod3, pg, sg, su, sd, ys)


def _moe(h2, x1, idx, wts, rank, counts, mod3, ffn_post_g, exp_w_gate, exp_w_up, exp_w_down,
         sw_gate, sw_up, sw_down, S):
    T, D = x1.shape
    A = T * TOP_K
    P = A + N_EXPERTS * BM
    cnt = counts.reshape(N_EXPERTS)
    blocks = (cnt + BM - 1) // BM
    bstart = jnp.concatenate([jnp.zeros((1,), I32), jnp.cumsum(blocks).astype(I32)])
    pstart = bstart[:-1] * BM

    dest = _dest(idx, rank, pstart)
    xs = _dispatch(h2, dest, bstart, cnt, P)
    ys = _experts(xs, bstart, exp_w_gate, exp_w_up, exp_w_down)
    return _combine(dest, wts, x1, h2, mod3, ffn_post_g, sw_gate, sw_up, sw_down, ys, S)


def _layer(x, c, positions, lambda_init, w_ada, b_ada, attn_pre_g, attn_post_g, w_in, q_norm_g, kv_norm_g,
           w_uq, w_ukv, lam_q1, lam_k1, lam_q2, lam_k2, diff_subln_g, w_o, ffn_pre_g, ffn_post_g,
           router_w, router_b, exp_w_gate, exp_w_up, exp_w_down, sw_gate, sw_up, sw_down):
    B, S, D = x.shape
    T = B * S
    x2 = x.reshape(T, D)
    posf = positions.astype(F32)
    pos_col = posf.reshape(T, 1)
    pos_row = posf.reshape(B, 1, S)

    mod3 = _ada(c, w_ada, b_ada).reshape(B, 6, D)
    q, k, v, dq, dk, dv = _pre_attn(x2, mod3, pos_col, attn_pre_g, w_in, q_norm_g, kv_norm_g, w_uq, w_ukv, S)
    mla = _mla_attn(q, k, v, B, S)
    lam4 = jnp.stack([lam_q1, lam_k1, lam_q2, lam_k2])
    dif = _diff_attn(dq, dk, dv, pos_col, pos_row, lam4, diff_subln_g, lambda_init, B, S)
    x1, h2, idx, wts, rank, counts = _post_attn(x2, mla, dif, mod3, w_o, attn_post_g, ffn_pre_g,
                                                router_w, router_b, S)
    out = _moe(h2, x1, idx, wts, rank, counts, mod3, ffn_post_g, exp_w_gate, exp_w_up, exp_w_down,
               sw_gate, sw_up, sw_down, S)
    return out.reshape(B, S, D)


def kernel(x, c, positions, w_ada, b_ada, attn_pre_g, attn_post_g, w_in, q_norm_g, kv_norm_g, w_uq, w_ukv,
           lam_q1, lam_k1, lam_q2, lam_k2, diff_subln_g, w_o, ffn_pre_g, ffn_post_g, router_w, router_b,
           exp_w_gate, exp_w_up, exp_w_down, shared_w_gate, shared_w_up, shared_w_down):
    depth = w_ada.shape[0]
    for l in range(depth):
        lambda_init = 0.8 - 0.6 * math.exp(-0.3 * l)
        x = _layer(x, c, positions, lambda_init, w_ada[l], b_ada[l], attn_pre_g[l], attn_post_g[l], w_in[l],
                   q_norm_g[l], kv_norm_g[l], w_uq[l], w_ukv[l], lam_q1[l], lam_k1[l], lam_q2[l], lam_k2[l],
                   diff_subln_g[l], w_o[l], ffn_pre_g[l], ffn_post_g[l], router_w[l], router_b[l],
                   exp_w_gate[l], exp_w_up[l], exp_w_down[l], shared_w_gate[l], shared_w_up[l],
                   shared_w_down[l])
    return x
```

```python
import functools
import math

import jax
import jax.numpy as jnp
from jax import lax
from jax.experimental import pallas as pl
from jax.experimental.pallas import tpu as pltpu

F32 = jnp.float32
BF16 = jnp.bfloat16
I32 = jnp.int32

MLA_HEADS = 8
MLA_NOPE = 64
MLA_ROPE = 32
MLA_V = 64
MLA_Q_RANK = 256
MLA_KV_RANK = 128
ROPE_BASE = 10000.0
DIFF_HEADS = 4
DIFF_HD = 64
DIFF_V = 128
N_EXPERTS = 256
TOP_K = 8
N_GROUPS = 8
GROUP_SIZE = N_EXPERTS // N_GROUPS
TOPK_GROUPS = 4
ROUTED_SCALE = 2.5
NORM_EPS = 1e-6
SUBLN_EPS = 1e-5
LOG2E = 1.4426950408889634

LANE = 128
HEAD_PAD = 128

TM_PRE = 512
TQ = 512
TK = 512
TM_POST = 512
TM_DISP = 256
TM_COMB = 128
BM = 128
VMEM_LIMIT = 48 * 1024 * 1024


def _cparams(*sem):
    return pltpu.CompilerParams(dimension_semantics=sem, vmem_limit_bytes=VMEM_LIMIT)


def _rms(x, eps):
    return x * lax.rsqrt(jnp.mean(x * x, axis=-1, keepdims=True) + eps)


def _sigmoid(x):
    return 1.0 / (1.0 + jnp.exp(-x))


def _dot(a, b):
    return jnp.dot(a, b, preferred_element_type=F32)


RT = 8


def _rt_load(ref, n, *lead):
    return jnp.concatenate([ref[(*lead, pl.ds(j, n, stride=RT), slice(None))] for j in range(RT)], axis=1)


def _rt_store(ref, val, *lead):
    n = val.shape[0]
    for j in range(RT):
        ref[(*lead, pl.ds(j, n, stride=RT), slice(None))] = val[:, j * LANE:(j + 1) * LANE]


def _dot_nt(a, b):
    return lax.dot_general(a, b, (((1,), (1,)), ((), ())), preferred_element_type=F32)


def _ada_kernel(c_ref, w_ref, b_ref, o_ref):
    c = c_ref[...]
    a = c * _sigmoid(c)
    o_ref[...] = jnp.dot(a, w_ref[...], preferred_element_type=F32,
                         precision=lax.Precision.HIGHEST) + b_ref[...]


def _ada(c, w_ada, b_ada):
    B, D = c.shape
    n = w_ada.shape[1]
    return pl.pallas_call(
        _ada_kernel,
        out_shape=jax.ShapeDtypeStruct((B, n), F32),
        grid=(n // D,),
        in_specs=[pl.BlockSpec((B, D), lambda j: (0, 0)),
                  pl.BlockSpec((D, D), lambda j: (0, j)),
                  pl.BlockSpec((1, D), lambda j: (0, j))],
        out_specs=pl.BlockSpec((B, D), lambda j: (0, j)),
        compiler_params=_cparams("arbitrary"),
        name="ada",
    )(c, w_ada, b_ada.reshape(1, n))


_C_CQ = 0
_C_CKV = _C_CQ + MLA_Q_RANK
_C_KRA = _C_CKV + MLA_KV_RANK
_C_KRB = _C_KRA + LANE
_C_DQ = _C_KRB + LANE
_C_DK = _C_DQ + DIFF_HEADS * DIFF_V
_C_DV = _C_DK + DIFF_HEADS * DIFF_V
_C_END = _C_DV + DIFF_HEADS * DIFF_V


def _pre_attn_kernel(x_ref, mod_ref, pos_ref, inv_ref, g_ref, w1_ref, qg_ref, kvg_ref,
                     wqa_ref, wqb_ref, wkn_ref, wv_ref,
                     q_ref, k_ref, v_ref, dq_ref, dk_ref, dv_ref):
    x = x_ref[...]
    sh = mod_ref[0:1, :]
    sc = mod_ref[1:2, :]
    h = _rms(x, NORM_EPS) * g_ref[...]
    h = h * (1.0 + sc) + sh
    p = _dot(h.astype(BF16), w1_ref[...])

    ang = pos_ref[...] * inv_ref[...]
    lane = lax.broadcasted_iota(I32, ang.shape, 1)
    in_rope = (lane >= MLA_NOPE) & (lane < MLA_NOPE + MLA_ROPE)
    cos_r = jnp.where(in_rope, jnp.cos(ang), 0.0)
    sin_r = jnp.where(in_rope, jnp.sin(ang), 0.0)
    cos_q = jnp.where(lane < MLA_NOPE, 1.0, cos_r)

    cqn = (_rms(p[:, _C_CQ:_C_CKV], NORM_EPS) * qg_ref[...]).astype(BF16)
    qa = _dot(cqn, wqa_ref[...])
    qb = _dot(cqn, wqb_ref[...])
    q_scale = LOG2E / math.sqrt(MLA_NOPE + MLA_ROPE)
    cos_t = jnp.concatenate([cos_q] * MLA_HEADS, axis=1)
    sin_t = jnp.concatenate([sin_r] * MLA_HEADS, axis=1)
    q_ref[...] = ((qa * cos_t + qb * sin_t) * q_scale).astype(BF16)

    ckvn = (_rms(p[:, _C_CKV:_C_KRA], NORM_EPS) * kvg_ref[...]).astype(BF16)
    kn = _dot(ckvn, wkn_ref[...])
    kr = p[:, _C_KRA:_C_KRB] * cos_r + p[:, _C_KRB:_C_DQ] * sin_r
    k_ref[...] = (kn + jnp.concatenate([kr] * MLA_HEADS, axis=1)).astype(BF16)
    lane_t = lax.broadcasted_iota(I32, kn.shape, 1)
    ones_col = jnp.where(lane_t % HEAD_PAD == MLA_V, 1.0, 0.0)
    v_ref[...] = (_dot(ckvn, wv_ref[...]) + ones_col).astype(BF16)

    dq_ref[...] = (p[:, _C_DQ:_C_DK] * (LOG2E / math.sqrt(DIFF_HD))).astype(BF16)
    dk_ref[...] = p[:, _C_DK:_C_DV].astype(BF16)
    dv_ref[...] = p[:, _C_DV:_C_END].astype(BF16)


def _pre_attn(x2, mod3, pos_col, attn_pre_g, w_in, q_norm_g, kv_norm_g, w_uq, w_ukv, S):
    T, D = x2.shape
    tm = min(TM_PRE, S)
    tpb = S // tm
    f = lambda a: a.astype(BF16)
    d3 = DIFF_HEADS * DIFF_V
    a = MLA_Q_RANK
    b = a + MLA_KV_RANK
    c = b + MLA_ROPE

    def swap(r):
        hlf = MLA_ROPE // 2
        return jnp.concatenate([-r[..., hlf:], r[..., :hlf]], axis=-1)

    def pad_rope(r):
        return jnp.pad(r, ((0, 0), (MLA_NOPE, LANE - MLA_NOPE - MLA_ROPE)))

    w_kr = w_in[:, b:c]
    w1 = jnp.concatenate([w_in[:, :b], pad_rope(w_kr), pad_rope(swap(w_kr)), w_in[:, c:]], axis=1)
    assert w1.shape[1] == _C_END
    padq = HEAD_PAD - MLA_NOPE - MLA_ROPE
    wqa = jnp.pad(w_uq, ((0, 0), (0, 0), (0, padq))).reshape(MLA_Q_RANK, MLA_HEADS * HEAD_PAD)
    q_rope = w_uq[..., MLA_NOPE:]
    wqb = jnp.pad(swap(q_rope), ((0, 0), (0, 0), (MLA_NOPE, padq))).reshape(MLA_Q_RANK, MLA_HEADS * HEAD_PAD)
    wkn = jnp.pad(w_ukv[..., :MLA_NOPE], ((0, 0), (0, 0), (0, HEAD_PAD - MLA_NOPE))).reshape(
        MLA_KV_RANK, MLA_HEADS * HEAD_PAD)
    wv = jnp.pad(w_ukv[..., MLA_NOPE:], ((0, 0), (0, 0), (0, HEAD_PAD - MLA_V))).reshape(
        MLA_KV_RANK, MLA_HEADS * HEAD_PAD)

    inv = 1.0 / (ROPE_BASE ** (jnp.arange(0, MLA_ROPE, 2, dtype=F32) / MLA_ROPE))
    inv_lane = jnp.pad(jnp.concatenate([inv, inv]), (MLA_NOPE, LANE - MLA_NOPE - MLA_ROPE)).reshape(1, LANE)

    full = lambda arr: pl.BlockSpec(arr.shape, lambda i: (0,) * arr.ndim)
    row = lambda w: pl.BlockSpec((tm, w), lambda i: (i, 0))
    ins = [x2, mod3, pos_col, inv_lane, attn_pre_g.reshape(1, D), f(w1), q_norm_g.reshape(1, -1),
           kv_norm_g.reshape(1, -1), f(wqa), f(wqb), f(wkn), f(wv)]
    in_specs = [row(D), pl.BlockSpec((None, 6, D), lambda i: (i // tpb, 0, 0)), row(1)] + \
               [full(arr) for arr in ins[3:]]
    widths = [MLA_HEADS * HEAD_PAD, MLA_HEADS * HEAD_PAD, MLA_HEADS * HEAD_PAD, d3, d3, d3]
    return pl.pallas_call(
        _pre_attn_kernel,
        out_shape=[jax.ShapeDtypeStruct((T, w), BF16) for w in widths],
        grid=(T // tm,),
        in_specs=in_specs,
        out_specs=[row(w) for w in widths],
        compiler_params=_cparams("parallel"),
        name="pre_attn",
    )(*ins)


def _online_update(s, m, l, acc, v):
    m_new = jnp.maximum(m, jnp.max(s, axis=-1, keepdims=True))
    p = jnp.exp2(s - m_new)
    alpha = jnp.exp2(m - m_new)
    l_new = alpha * l + jnp.sum(p, axis=-1, keepdims=True)
    acc_new = alpha * acc + _dot(p.astype(BF16), v)
    return m_new, l_new, acc_new


MLA_HPS = 4


def _causal_blocks(i, tq, tk):
    nfull = (i * tq) // tk
    diag = [(nfull + j, i * tq - (nfull + j) * tk) for j in range(max(1, tq // tk))]
    return nfull, diag


def _keep(tq, tk, off):
    r = lax.broadcasted_iota(I32, (tq, tk), 0)
    c = lax.broadcasted_iota(I32, (tq, tk), 1)
    return r + off >= c


def _mla_attn_kernel(tk, q_ref, k_ref, v_ref, o_ref):
    i = pl.program_id(2)
    tq = q_ref.shape[0]
    sl = [slice(j * HEAD_PAD, (j + 1) * HEAD_PAD) for j in range(MLA_HPS)]
    qs = [q_ref[:, sl[j]] for j in range(MLA_HPS)]

    def step(kb, carry, off=None):
        r0 = pl.multiple_of(kb * tk, tk)
        out = []
        for j in range(MLA_HPS):
            m, acc = carry[j]
            s = _dot_nt(qs[j], k_ref[pl.ds(r0, tk), sl[j]])
            if off is not None:
                s = jnp.where(_keep(tq, tk, off), s, -jnp.inf)
            m_new = jnp.maximum(m, jnp.max(s, axis=-1, keepdims=True))
            p = jnp.exp2(s - m_new)
            acc = jnp.exp2(m - m_new) * acc + _dot(p.astype(BF16), v_ref[pl.ds(r0, tk), sl[j]])
            out.append((m_new, acc))
        return tuple(out)

    nfull, diag = _causal_blocks(i, tq, tk)
    init = tuple((jnp.full((tq, 1), -jnp.inf, F32), jnp.zeros((tq, HEAD_PAD), F32)) for _ in range(MLA_HPS))
    carry = lax.fori_loop(0, nfull, step, init)
    for kb, off in diag:
        carry = step(kb, carry, off)
    outs = [acc[:, :MLA_V] / acc[:, MLA_V:MLA_V + 1] for _, acc in carry]
    o_ref[...] = jnp.concatenate(outs, axis=1).astype(o_ref.dtype)


def _mla_attn(q, k, v, B, S):
    tq = min(TQ, S)
    tk = min(TK, S)
    q3 = q.reshape(B, S, -1)
    k3 = k.reshape(B, S, -1)
    v3 = v.reshape(B, S, -1)
    w = MLA_HPS * HEAD_PAD
    return pl.pallas_call(
        functools.partial(_mla_attn_kernel, tk),
        out_shape=jax.ShapeDtypeStruct((B, S, MLA_HEADS * MLA_V), BF16),
        grid=(B, MLA_HEADS // MLA_HPS, S // tq),
        in_specs=[pl.BlockSpec((None, tq, w), lambda b, h, i: (b, i, h)),
                  pl.BlockSpec((None, S, w), lambda b, h, i: (b, 0, h)),
                  pl.BlockSpec((None, S, w), lambda b, h, i: (b, 0, h))],
        out_specs=pl.BlockSpec((None, tq, MLA_HPS * MLA_V), lambda b, h, i: (b, i, h)),
        compiler_params=_cparams("parallel", "parallel", "arbitrary"),
        name="mla_attn",
    )(q3, k3, v3)


DIFF_HPS = 2


def _diff_attn_kernel(lambda_init, tk, q_ref, k_ref, v_ref, pc_ref, pr_ref, lam_ref, g_ref, o_ref):
    i = pl.program_id(2)
    tq = q_ref.shape[0]
    sl = [slice(j * DIFF_V, (j + 1) * DIFF_V) for j in range(DIFF_HPS)]
    lane = lax.broadcasted_iota(I32, (tq, DIFF_V), 1)
    qs, nslopes = [], []
    for j in range(DIFF_HPS):
        q = q_ref[:, sl[j]]
        zero = jnp.zeros_like(q)
        qs.append((jnp.where(lane < DIFF_HD, q, zero), jnp.where(lane >= DIFF_HD, q, zero)))
        hv = jnp.full((1, 1), pl.program_id(1) * DIFF_HPS + j, I32).astype(F32)
        nslopes.append(-LOG2E * jnp.exp2(-8.0 * (hv + 1.0) / DIFF_HEADS))
    pq = pc_ref[...]

    def step(kb, carry, off=None):
        r0 = pl.multiple_of(kb * tk, tk)
        dist = jnp.abs(pq - pr_ref[:, pl.ds(r0, tk)])
        keep = None if off is None else _keep(tq, tk, off)
        out = []
        for j in range(DIFF_HPS):
            kblk = k_ref[pl.ds(r0, tk), sl[j]]
            vblk = v_ref[pl.ds(r0, tk), sl[j]]
            bias = nslopes[j] * dist
            for c in range(2):
                s = _dot_nt(qs[j][c], kblk) + bias
                if keep is not None:
                    s = jnp.where(keep, s, -jnp.inf)
                out.append(_online_update(s, *carry[2 * j + c], vblk))
        return tuple(out)

    nfull, diag = _causal_blocks(i, tq, tk)
    init1 = (jnp.full((tq, 1), -jnp.inf, F32), jnp.zeros((tq, 1), F32), jnp.zeros((tq, DIFF_V), F32))
    carry = lax.fori_loop(0, nfull, step, (init1,) * (2 * DIFF_HPS))
    for kb, off in diag:
        carry = step(kb, carry, off)

    lv = lam_ref[...]
    lam = (jnp.exp(jnp.sum(lv[0:1] * lv[1:2], axis=-1, keepdims=True))
           - jnp.exp(jnp.sum(lv[2:3] * lv[3:4], axis=-1, keepdims=True)) + lambda_init)
    outs = []
    for j in range(DIFF_HPS):
        (_, l1, a1), (_, l2, a2) = carry[2 * j], carry[2 * j + 1]
        o = a1 / l1 - lam * (a2 / l2)
        outs.append(_rms(o, SUBLN_EPS) * g_ref[...] * (1.0 - lambda_init))
    o_ref[...] = jnp.concatenate(outs, axis=1).astype(o_ref.dtype)


def _diff_attn(dq, dk, dv, pos_col, pos_row, lam4, subln_g, lambda_init, B, S):
    tq = min(TQ, S)
    tk = min(TK, S)
    nq = S // tq
    q3 = dq.reshape(B, S, -1)
    k3 = dk.reshape(B, S, -1)
    v3 = dv.reshape(B, S, -1)
    w = DIFF_HPS * DIFF_V
    return pl.pallas_call(
        functools.partial(_diff_attn_kernel, lambda_init, tk),
        out_shape=jax.ShapeDtypeStruct((B, S, DIFF_HEADS * DIFF_V), BF16),
        grid=(B, DIFF_HEADS // DIFF_HPS, nq),
        in_specs=[pl.BlockSpec((None, tq, w), lambda b, h, i: (b, i, h)),
                  pl.BlockSpec((None, S, w), lambda b, h, i: (b, 0, h)),
                  pl.BlockSpec((None, S, w), lambda b, h, i: (b, 0, h)),
                  pl.BlockSpec((tq, 1), lambda b, h, i: (b * nq + i, 0)),
                  pl.BlockSpec((None, 1, S), lambda b, h, i: (b, 0, 0)),
                  pl.BlockSpec((4, DIFF_HD), lambda b, h, i: (0, 0)),
                  pl.BlockSpec((1, DIFF_V), lambda b, h, i: (0, 0))],
        out_specs=pl.BlockSpec((None, tq, w), lambda b, h, i: (b, i, h)),
        compiler_params=_cparams("parallel", "parallel", "arbitrary"),
        name="diff_attn",
    )(q3, k3, v3, pos_col, pos_row, lam4, subln_g.reshape(1, DIFF_V))


def _first_argmax(v, io, n, axis):
    m = jnp.max(v, axis=axis, keepdims=True)
    ix = jnp.min(jnp.where(v == m, io, n), axis=axis, keepdims=True)
    return m, ix


def _post_attn_kernel(x_ref, mla_ref, dif_ref, mod_ref, woa_ref, wob_ref, pg_ref, fg_ref,
                      rwt_ref, rb_ref,
                      x1_ref, h2_ref, idx_ref, wts_ref, rank_ref, cnt_ref, run_ref):
    step = pl.program_id(0)
    tm = x_ref.shape[0]
    E, G, GS = N_EXPERTS, N_GROUPS, GROUP_SIZE

    @pl.when(step == 0)
    def _():
        run_ref[...] = jnp.zeros_like(run_ref)

    g_a = mod_ref[2:3, :]
    sh_f = mod_ref[3:4, :]
    sc_f = mod_ref[4:5, :]
    y = _dot(mla_ref[...], woa_ref[...]) + _dot(dif_ref[...], wob_ref[...])
    x1 = x_ref[...] + g_a * (_rms(y, NORM_EPS) * pg_ref[...])
    x1_ref[...] = x1
    h2 = (_rms(x1, NORM_EPS) * fg_ref[...]) * (1.0 + sc_f) + sh_f
    _rt_store(h2_ref, h2)

    logits = lax.dot_general(rwt_ref[...], h2, (((1,), (1,)), ((), ())),
                             preferred_element_type=F32, precision=lax.Precision.HIGHEST)
    scores = _sigmoid(logits)
    sel = scores + rb_ref[...]

    sio = lax.broadcasted_iota(I32, (GS, tm), 0)
    gs_rows = []
    for g in range(G):
        blk = sel[g * GS:(g + 1) * GS, :]
        m1, i1 = _first_argmax(blk, sio, GS, 0)
        m2 = jnp.max(jnp.where(sio == i1, -jnp.inf, blk), axis=0, keepdims=True)
        gs_rows.append(m1 + m2)
    gs = jnp.concatenate(gs_rows, axis=0)

    gio = lax.broadcasted_iota(I32, (G, tm), 0)
    gkeep = jnp.zeros((G, tm), F32)
    for _ in range(TOPK_GROUPS):
        _, ix = _first_argmax(gs, gio, G, 0)
        pick = gio == ix
        gkeep = jnp.where(pick, 1.0, gkeep)
        gs = jnp.where(pick, -jnp.inf, gs)
    ekeep = jnp.concatenate([jnp.broadcast_to(gkeep[g:g + 1, :], (GS, tm)) for g in range(G)], axis=0)
    cand = jnp.where(ekeep > 0.0, sel, -jnp.inf)

    eio = lax.broadcasted_iota(I32, (E, tm), 0)
    idx_rows, w_rows = [], []
    for _ in range(TOP_K):
        _, ix = _first_argmax(cand, eio, E, 0)
        pick = eio == ix
        w_rows.append(jnp.sum(jnp.where(pick, scores, 0.0), axis=0, keepdims=True))
        cand = jnp.where(pick, -jnp.inf, cand)
        idx_rows.append(ix)
    idx = jnp.concatenate(idx_rows, axis=0)
    w = jnp.concatenate(w_rows, axis=0)
    wts_ref[...] = w / jnp.sum(w, axis=0, keepdims=True) * ROUTED_SCALE
    idx_ref[...] = idx

    onehot = jnp.zeros((E, tm), F32)
    for k in range(TOP_K):
        onehot = onehot + jnp.where(eio == idx_rows[k], 1.0, 0.0)
    tr = lax.broadcasted_iota(I32, (tm, tm), 0)
    tc = lax.broadcasted_iota(I32, (tm, tm), 1)
    before = jnp.where(tr < tc, 1.0, 0.0).astype(BF16)
    prior = _dot(onehot.astype(BF16), before) + run_ref[...]
    rank_rows = [jnp.sum(jnp.where(eio == idx_rows[k], prior, 0.0), axis=0, keepdims=True)
                 for k in range(TOP_K)]
    rank_ref[...] = jnp.concatenate(rank_rows, axis=0).astype(I32)
    run_ref[...] += jnp.sum(onehot, axis=1, keepdims=True)
    cnt_ref[...] = run_ref[...].astype(I32)


def _post_attn(x2, mla, dif, mod3, w_o, attn_post_g, ffn_pre_g, router_w, router_b, S):
    T, D = x2.shape
    tm = min(TM_POST, S)
    tpb = S // tm
    half = MLA_HEADS * MLA_V
    woa = w_o[:half].astype(BF16)
    wob = w_o[half:].astype(BF16)
    rwt = router_w.T
    full = lambda arr: pl.BlockSpec(arr.shape, lambda i: (0,) * arr.ndim)
    row = lambda w: pl.BlockSpec((tm, w), lambda i: (i, 0))
    col = lambda r: pl.BlockSpec((r, tm), lambda i: (0, i))
    ins = [x2, mla.reshape(T, -1), dif.reshape(T, -1), mod3, woa, wob, attn_post_g.reshape(1, D),
           ffn_pre_g.reshape(1, D), rwt, router_b.reshape(N_EXPERTS, 1)]
    in_specs = [row(D), row(half), row(D - half), pl.BlockSpec((None, 6, D), lambda i: (i // tpb, 0, 0))] + \
               [full(arr) for arr in ins[4:]]
    return pl.pallas_call(
        _post_attn_kernel,
        out_shape=[jax.ShapeDtypeStruct((T, D), F32), jax.ShapeDtypeStruct((T * RT, LANE), F32),
                   jax.ShapeDtypeStruct((TOP_K, T), I32), jax.ShapeDtypeStruct((TOP_K, T), F32),
                   jax.ShapeDtypeStruct((TOP_K, T), I32), jax.ShapeDtypeStruct((N_EXPERTS, 1), I32)],
        grid=(T // tm,),
        in_specs=in_specs,
        out_specs=[row(D), pl.BlockSpec((tm * RT, LANE), lambda i: (i, 0)), col(TOP_K), col(TOP_K), col(TOP_K),
                   pl.BlockSpec((N_EXPERTS, 1), lambda i: (0, 0))],
        scratch_shapes=[pltpu.VMEM((N_EXPERTS, 1), F32)],
        compiler_params=_cparams("arbitrary"),
        name="post_attn",
    )(*ins)


def _dest_kernel(idx_ref, rank_ref, ps_ref, o_ref):
    idx = idx_ref[...]
    tm = idx.shape[1]
    eio = lax.broadcasted_iota(I32, (N_EXPERTS, tm), 0)
    ps = ps_ref[...]
    rows = [jnp.sum(jnp.where(eio == idx[k:k + 1, :], ps, 0.0), axis=0, keepdims=True)
            for k in range(TOP_K)]
    o_ref[...] = jnp.concatenate(rows, axis=0).astype(I32) + rank_ref[...]


def _dest(idx, rank, pstart):
    K, T = idx.shape
    tm = min(2048, T)
    col = pl.BlockSpec((K, tm), lambda i: (0, i))
    return pl.pallas_call(
        _dest_kernel,
        out_shape=jax.ShapeDtypeStruct((K, T), I32),
        grid=(T // tm,),
        in_specs=[col, col, pl.BlockSpec((N_EXPERTS, 1), lambda i: (0, 0))],
        out_specs=col,
        compiler_params=_cparams("parallel"),
        name="dest",
    )(idx, rank, pstart.astype(F32).reshape(N_EXPERTS, 1))


def _dispatch_kernel(bs_ref, cnt_ref, dest_ref, h_ref, xs_ref, zeros, sem, zsem):
    tm = h_ref.shape[0] // RT
    blk = BM * RT
    nb = xs_ref.shape[0] // blk

    @pl.when(pl.program_id(0) == 0)
    def _():
        zeros[...] = jnp.zeros_like(zeros)

        def zcopy(g):
            return pltpu.make_async_copy(zeros, xs_ref.at[pl.ds(pl.multiple_of(g * blk, blk), blk), :], zsem)

        def has_pad(e):
            return cnt_ref[e] % BM != 0

        def fill(e, _):
            @pl.when(has_pad(e))
            def _():
                zcopy(bs_ref[e + 1] - 1).start()
            return 0

        def drain(e, _):
            @pl.when(has_pad(e))
            def _():
                zcopy(0).wait()
            return 0

        lax.fori_loop(0, N_EXPERTS, fill, 0)
        lax.fori_loop(bs_ref[N_EXPERTS], nb, lambda g, _: (zcopy(g).start(), 0)[1], 0)
        lax.fori_loop(0, N_EXPERTS, drain, 0)
        lax.fori_loop(bs_ref[N_EXPERTS], nb, lambda g, _: (zcopy(0).wait(), 0)[1], 0)

    def tile(ref, r):
        return ref.at[pl.ds(pl.multiple_of(r * RT, RT), RT), :]

    def issue(t, _):
        for k in range(TOP_K):
            pltpu.make_async_copy(tile(h_ref, t), tile(xs_ref, dest_ref[k, t]), sem).start(priority=k % 2)
        return 0

    lax.fori_loop(0, tm, issue, 0)
    for _ in range(TOP_K):
        pltpu.make_async_copy(h_ref, xs_ref.at[pl.ds(0, tm * RT), :], sem).wait()


def _dispatch(h2rt, dest, bstart, cnt, P):
    T = h2rt.shape[0] // RT
    tm = min(TM_DISP, T)
    nt = T // tm
    dest3 = dest.reshape(TOP_K, nt, tm).transpose(1, 0, 2)
    return pl.pallas_call(
        _dispatch_kernel,
        out_shape=jax.ShapeDtypeStruct((P * RT, LANE), F32),
        grid_spec=pltpu.PrefetchScalarGridSpec(
            num_scalar_prefetch=2,
            grid=(nt,),
            in_specs=[pl.BlockSpec((None, TOP_K, tm), lambda i, bs, cn: (i, 0, 0), memory_space=pltpu.SMEM),
                      pl.BlockSpec((tm * RT, LANE), lambda i, bs, cn: (i, 0))],
            out_specs=pl.BlockSpec(memory_space=pl.ANY),
            scratch_shapes=[pltpu.VMEM((BM * RT, LANE), F32), pltpu.SemaphoreType.DMA(()),
                            pltpu.SemaphoreType.DMA(())],
        ),
        compiler_params=_cparams("arbitrary"),
        name="dispatch",
    )(bstart, cnt, dest3, h2rt)


NBUF_X = 6
X_AHEAD = NBUF_X - 2
NBUF_Y = 4


def _experts_kernel(bs_ref, wg_ref, wu_ref, wd_ref, xs_ref, ys_ref, wgb, wub, wdb, xbuf, ybuf, xsem, ysem):
    e = pl.program_id(0)
    blk = BM * RT
    nb = xs_ref.shape[0] // blk
    nused = bs_ref[N_EXPERTS]
    g0 = bs_ref[e]
    g1 = bs_ref[e + 1]

    def rows(g):
        return pl.ds(pl.multiple_of(g * blk, blk), blk)

    def x_copy(g, slot):
        return pltpu.make_async_copy(xs_ref.at[rows(g), :], xbuf.at[slot], xsem.at[slot])

    def y_copy(g, slot):
        return pltpu.make_async_copy(ybuf.at[slot], ys_ref.at[rows(g), :], ysem.at[slot])

    @pl.when(e == 0)
    def _():
        for j in range(X_AHEAD):
            @pl.when(j < nused)
            def _():
                x_copy(j, j).start()

    def step(gs):
        for g in gs:
            x_copy(g, g % NBUF_X).wait()
            nxt = g + X_AHEAD

            @pl.when(nxt < nused)
            def _():
                x_copy(nxt, nxt % NBUF_X).start()

        x = jnp.concatenate([_rt_load(xbuf, BM, g % NBUF_X) for g in gs], axis=0).astype(BF16)
        gate = _dot(x, wgb[...])
        up = _dot(x, wub[...])
        a = (gate * _sigmoid(gate)) * up
        y = _dot(a.astype(BF16), wdb[...])
        for j, g in enumerate(gs):
            ys = g % NBUF_Y

            @pl.when(g >= NBUF_Y)
            def _():
                y_copy(g - NBUF_Y, ys).wait()

            _rt_store(ybuf, y[j * BM:(j + 1) * BM], ys)
            y_copy(g, ys).start(priority=1)

    @pl.when(g1 > g0)
    def _():
        wgb[...] = wg_ref[...].astype(BF16)
        wub[...] = wu_ref[...].astype(BF16)
        wdb[...] = wd_ref[...].astype(BF16)

        def pair(i, _):
            g = g0 + 2 * i
            step([g, g + 1])
            return 0

        lax.fori_loop(0, (g1 - g0) // 2, pair, 0)

        @pl.when((g1 - g0) % 2 == 1)
        def _():
            step([g1 - 1])

    @pl.when(e == N_EXPERTS - 1)
    def _():
        for j in range(NBUF_Y):
            @pl.when(nused - 1 - j >= 0)
            def _():
                y_copy(0, (nused - 1 - j) % NBUF_Y).wait()
        ybuf[0] = jnp.zeros(ybuf.shape[1:], F32)
        lax.fori_loop(nused, nb, lambda g, _: (y_copy(g, 0).start(), 0)[1], 0)
        lax.fori_loop(nused, nb, lambda g, _: (y_copy(0, 0).wait(), 0)[1], 0)


def _experts(xs, bstart, w_gate, w_up, w_down):
    E, D, F = w_gate.shape
    return pl.pallas_call(
        _experts_kernel,
        out_shape=jax.ShapeDtypeStruct(xs.shape, F32),
        grid_spec=pltpu.PrefetchScalarGridSpec(
            num_scalar_prefetch=1,
            grid=(E,),
            in_specs=[pl.BlockSpec((None, D, F), lambda e, bs: (e, 0, 0)),
                      pl.BlockSpec((None, D, F), lambda e, bs: (e, 0, 0)),
                      pl.BlockSpec((None, F, D), lambda e, bs: (e, 0, 0)),
                      pl.BlockSpec(memory_space=pl.ANY)],
            out_specs=pl.BlockSpec(memory_space=pl.ANY),
            scratch_shapes=[pltpu.VMEM((D, F), BF16), pltpu.VMEM((D, F), BF16), pltpu.VMEM((F, D), BF16),
                            pltpu.VMEM((NBUF_X, BM * RT, LANE), F32), pltpu.VMEM((NBUF_Y, BM * RT, LANE), F32),
                            pltpu.SemaphoreType.DMA((NBUF_X,)), pltpu.SemaphoreType.DMA((NBUF_Y,))],
        ),
        compiler_params=_cparams("arbitrary"),
        name="experts",
    )(bstart, w_gate, w_up, w_down, xs)


def _combine_kernel(dcur_ref, dnxt_ref, w_ref, x1_ref, h_ref, mod_ref, pg_ref, sg_ref, su_ref, sd_ref,
                    ys_ref, o_ref, rows_a, rows_b, sem):
    i = pl.program_id(0)
    n = pl.num_programs(0)
    tm = x1_ref.shape[0]

    def tile(r):
        return pl.ds(pl.multiple_of(r * RT, RT), RT)

    def request(dref, buf, s, t):
        for k in range(TOP_K):
            pltpu.make_async_copy(ys_ref.at[tile(dref[k, t]), :], buf.at[k, tile(t), :],
                                  sem.at[s]).start(priority=k % 2)

    def drain(buf, s):
        for k in range(TOP_K):
            pltpu.make_async_copy(ys_ref.at[pl.ds(0, tm * RT), :], buf.at[k], sem.at[s]).wait()

    @pl.when(i == 0)
    def _():
        lax.fori_loop(0, tm, lambda t, _: (request(dcur_ref, rows_a, 0, t), 0)[1], 0)

    def step(cur, s_cur, nxt, s_nxt):
        drain(cur, s_cur)
        for t in range(tm):
            request(dnxt_ref, nxt, s_nxt, t)
        w = w_ref[...]
        routed = w[:, 0:1] * _rt_load(cur, tm, 0)
        for k in range(1, TOP_K):
            routed = routed + w[:, k:k + 1] * _rt_load(cur, tm, k)
        hb = _rt_load(h_ref, tm).astype(BF16)
        g = _dot(hb, sg_ref[...])
        u = _dot(hb, su_ref[...])
        shared = _dot(((g * _sigmoid(g)) * u).astype(BF16), sd_ref[...])
        y = routed + shared
        g_f = mod_ref[5:6, :]
        o_ref[...] = x1_ref[...] + g_f * (_rms(y, NORM_EPS) * pg_ref[...])

    even = i % 2 == 0

    @pl.when(even)
    def _():
        step(rows_a, 0, rows_b, 1)

    @pl.when(jnp.logical_not(even))
    def _():
        step(rows_b, 1, rows_a, 0)

    @pl.when(i == n - 1)
    def _():
        @pl.when(even)
        def _():
            drain(rows_b, 1)

        @pl.when(jnp.logical_not(even))
        def _():
            drain(rows_a, 0)


def _combine(dest, wts, x1, h2, mod3, ffn_post_g, sw_gate, sw_up, sw_down, ys, S):
    T, D = x1.shape
    tm = min(TM_COMB, S)
    nt = T // tm
    tpb = S // tm
    dest3 = dest.reshape(TOP_K, nt, tm).transpose(1, 0, 2)
    w_tk = wts.T
    full = lambda arr: pl.BlockSpec(arr.shape, lambda i: (0,) * arr.ndim)
    row = lambda w: pl.BlockSpec((tm, w), lambda i: (i, 0))
    sg, su, sd = sw_gate.astype(BF16), sw_up.astype(BF16), sw_down.astype(BF16)
    pg = ffn_post_g.reshape(1, D)
    return pl.pallas_call(
        _combine_kernel,
        out_shape=jax.ShapeDtypeStruct((T, D), F32),
        grid=(nt,),
        in_specs=[pl.BlockSpec((None, TOP_K, tm), lambda i: (i, 0, 0), memory_space=pltpu.SMEM),
                  pl.BlockSpec((None, TOP_K, tm), lambda i: (jnp.minimum(i + 1, nt - 1), 0, 0),
                               memory_space=pltpu.SMEM),
                  row(TOP_K), row(D), pl.BlockSpec((tm * RT, LANE), lambda i: (i, 0)),
                  pl.BlockSpec((None, 6, D), lambda i: (i // tpb, 0, 0)),
                  full(pg), full(sg), full(su), full(sd),
                  pl.BlockSpec(memory_space=pl.ANY)],
        out_specs=row(D),
        scratch_shapes=[pltpu.VMEM((TOP_K, tm * RT, LANE), F32), pltpu.VMEM((TOP_K, tm * RT, LANE), F32),
                        pltpu.SemaphoreType.DMA((2,))],
        compiler_params=_cparams("arbitrary"),
        name="combine",
    )(dest3, dest3, w_tk, x1, h2, mod3, pg, sg, su, sd, ys)


def _moe(h2, x1, idx, wts, rank, counts, mod3, ffn_post_g, exp_w_gate, exp_w_up, exp_w_down,
         sw_gate, sw_up, sw_down, S):
    T, D = x1.shape
    A = T * TOP_K
    P = A + N_EXPERTS * BM
    cnt = counts.reshape(N_EXPERTS)
    blocks = (cnt + BM - 1) // BM
    bstart = jnp.concatenate([jnp.zeros((1,), I32), jnp.cumsum(blocks).astype(I32)])
    pstart = bstart[:-1] * BM

    dest = _dest(idx, rank, pstart)
    xs = _dispatch(h2, dest, bstart, cnt, P)
    ys = _experts(xs, bstart, exp_w_gate, exp_w_up, exp_w_down)
    return _combine(dest, wts, x1, h2, mod3, ffn_post_g, sw_gate, sw_up, sw_down, ys, S)


def _layer(x, c, positions, lambda_init, w_ada, b_ada, attn_pre_g, attn_post_g, w_in, q_norm_g, kv_norm_g,
           w_uq, w_ukv, lam_q1, lam_k1, lam_q2, lam_k2, diff_subln_g, w_o, ffn_pre_g, ffn_post_g,
           router_w, router_b, exp_w_gate, exp_w_up, exp_w_down, sw_gate, sw_up, sw_down):
    B, S, D = x.shape
    T = B * S
    x2 = x.reshape(T, D)
    posf = positions.astype(F32)
    pos_col = posf.reshape(T, 1)
    pos_row = posf.reshape(B, 1, S)

    mod3 = _ada(c, w_ada, b_ada).reshape(B, 6, D)
    q, k, v, dq, dk, dv = _pre_attn(x2, mod3, pos_col, attn_pre_g, w_in, q_norm_g, kv_norm_g, w_uq, w_ukv, S)
    mla = _mla_attn(q, k, v, B, S)
    lam4 = jnp.stack([lam_q1, lam_k1, lam_q2, lam_k2])
    dif = _diff_attn(dq, dk, dv, pos_col, pos_row, lam4, diff_subln_g, lambda_init, B, S)
    x1, h2, idx, wts, rank, counts = _post_attn(x2, mla, dif, mod3, w_o, attn_post_g, ffn_pre_g,
                                                router_w, router_b, S)
    out = _moe(h2, x1, idx, wts, rank, counts, mod3, ffn_post_g, exp_w_gate, exp_w_up, exp_w_down,
               sw_gate, sw_up, sw_down, S)
    return out.reshape(B, S, D)


def kernel(x, c, positions, w_ada, b_ada, attn_pre_g, attn_post_g, w_in, q_norm_g, kv_norm_g, w_uq, w_ukv,
           lam_q1, lam_k1, lam_q2, lam_k2, diff_subln_g, w_o, ffn_pre_g, ffn_post_g, router_w, router_b,
           exp_w_gate, exp_w_up, exp_w_down, shared_w_gate, shared_w_up, shared_w_down):
    depth = w_ada.shape[0]
    for l in range(depth):
        lambda_init = 0.8 - 0.6 * math.exp(-0.3 * l)
        x = _layer(x, c, positions, lambda_init, w_ada[l], b_ada[l], attn_pre_g[l], attn_post_g[l], w_in[l],
                   q_norm_g[l], kv_norm_g[l], w_uq[l], w_ukv[l], lam_q1[l], lam_k1[l], lam_q2[l], lam_k2[l],
                   diff_subln_g[l], w_o[l], ffn_pre_g[l], ffn_post_g[l], router_w[l], router_b[l],
                   exp_w_gate[l], exp_w_up[l], exp_w_down[l], shared_w_gate[l], shared_w_up[l],
                   shared_w_down[l])
    return x
```

```python
import functools
import math

import jax
import jax.numpy as jnp
from jax import lax
from jax.experimental import pallas as pl
from jax.experimental.pallas import tpu as pltpu

F32 = jnp.float32
BF16 = jnp.bfloat16
I32 = jnp.int32

MLA_HEADS = 8
MLA_NOPE = 64
MLA_ROPE = 32
MLA_V = 64
MLA_Q_RANK = 256
MLA_KV_RANK = 128
ROPE_BASE = 10000.0
DIFF_HEADS = 4
DIFF_HD = 64
DIFF_V = 128
N_EXPERTS = 256
TOP_K = 8
N_GROUPS = 8
GROUP_SIZE = N_EXPERTS // N_GROUPS
TOPK_GROUPS = 4
ROUTED_SCALE = 2.5
NORM_EPS = 1e-6
SUBLN_EPS = 1e-5
LOG2E = 1.4426950408889634

LANE = 128
HEAD_PAD = 128

TM_PRE = 512
TQ = 512
TK = 512
TM_POST = 512
TM_DISP = 256
TM_COMB = 128
BM = 128
VMEM_LIMIT = 48 * 1024 * 1024


def _cparams(*sem):
    return pltpu.CompilerParams(dimension_semantics=sem, vmem_limit_bytes=VMEM_LIMIT)


def _rms(x, eps):
    return x * lax.rsqrt(jnp.mean(x * x, axis=-1, keepdims=True) + eps)


def _sigmoid(x):
    return 1.0 / (1.0 + jnp.exp(-x))


def _dot(a, b):
    return jnp.dot(a, b, preferred_element_type=F32)


RT = 4
HI_MASK = -65536


def _pack_rows(x):
    half = x.shape[1] // 2
    lo = lax.bitcast_convert_type(x[:, :half].astype(BF16).astype(F32), I32)
    hi = lax.bitcast_convert_type(x[:, half:].astype(BF16).astype(F32), I32)
    return lax.shift_right_logical(lo, 16) | (hi & HI_MASK)


def _unpack_rows(u, dtype):
    lo = lax.bitcast_convert_type(lax.shift_left(u, 16), F32)
    hi = lax.bitcast_convert_type(u & HI_MASK, F32)
    return jnp.concatenate([lo.astype(dtype), hi.astype(dtype)], axis=1)


def _rt_load(ref, n, *lead):
    return jnp.concatenate([ref[(*lead, pl.ds(j, n, stride=RT), slice(None))] for j in range(RT)], axis=1)


def _rt_store(ref, val, *lead):
    n = val.shape[0]
    for j in range(RT):
        ref[(*lead, pl.ds(j, n, stride=RT), slice(None))] = val[:, j * LANE:(j + 1) * LANE]


def _dot_nt(a, b):
    return lax.dot_general(a, b, (((1,), (1,)), ((), ())), preferred_element_type=F32)


def _ada_kernel(c_ref, w_ref, b_ref, o_ref):
    c = c_ref[...]
    a = c * _sigmoid(c)
    o_ref[...] = jnp.dot(a, w_ref[...], preferred_element_type=F32,
                         precision=lax.Precision.HIGHEST) + b_ref[...]


def _ada(c, w_ada, b_ada):
    B, D = c.shape
    n = w_ada.shape[1]
    return pl.pallas_call(
        _ada_kernel,
        out_shape=jax.ShapeDtypeStruct((B, n), F32),
        grid=(n // D,),
        in_specs=[pl.BlockSpec((B, D), lambda j: (0, 0)),
                  pl.BlockSpec((D, D), lambda j: (0, j)),
                  pl.BlockSpec((1, D), lambda j: (0, j))],
        out_specs=pl.BlockSpec((B, D), lambda j: (0, j)),
        compiler_params=_cparams("arbitrary"),
        name="ada",
    )(c, w_ada, b_ada.reshape(1, n))


_C_CQ = 0
_C_CKV = _C_CQ + MLA_Q_RANK
_C_KRA = _C_CKV + MLA_KV_RANK
_C_KRB = _C_KRA + LANE
_C_DQ = _C_KRB + LANE
_C_DK = _C_DQ + DIFF_HEADS * DIFF_V
_C_DV = _C_DK + DIFF_HEADS * DIFF_V
_C_END = _C_DV + DIFF_HEADS * DIFF_V


def _pre_attn_kernel(x_ref, mod_ref, pos_ref, inv_ref, g_ref, w1_ref, qg_ref, kvg_ref,
                     wqa_ref, wqb_ref, wkn_ref, wv_ref,
                     q_ref, k_ref, v_ref, dq_ref, dk_ref, dv_ref):
    x = x_ref[...]
    sh = mod_ref[0:1, :]
    sc = mod_ref[1:2, :]
    h = _rms(x, NORM_EPS) * g_ref[...]
    h = h * (1.0 + sc) + sh
    p = _dot(h.astype(BF16), w1_ref[...])

    ang = pos_ref[...] * inv_ref[...]
    lane = lax.broadcasted_iota(I32, ang.shape, 1)
    in_rope = (lane >= MLA_NOPE) & (lane < MLA_NOPE + MLA_ROPE)
    cos_r = jnp.where(in_rope, jnp.cos(ang), 0.0)
    sin_r = jnp.where(in_rope, jnp.sin(ang), 0.0)
    cos_q = jnp.where(lane < MLA_NOPE, 1.0, cos_r)

    cqn = (_rms(p[:, _C_CQ:_C_CKV], NORM_EPS) * qg_ref[...]).astype(BF16)
    qa = _dot(cqn, wqa_ref[...])
    qb = _dot(cqn, wqb_ref[...])
    q_scale = LOG2E / math.sqrt(MLA_NOPE + MLA_ROPE)
    cos_t = jnp.concatenate([cos_q] * MLA_HEADS, axis=1)
    sin_t = jnp.concatenate([sin_r] * MLA_HEADS, axis=1)
    q_ref[...] = ((qa * cos_t + qb * sin_t) * q_scale).astype(BF16)

    ckvn = (_rms(p[:, _C_CKV:_C_KRA], NORM_EPS) * kvg_ref[...]).astype(BF16)
    kn = _dot(ckvn, wkn_ref[...])
    kr = p[:, _C_KRA:_C_KRB] * cos_r + p[:, _C_KRB:_C_DQ] * sin_r
    k_ref[...] = (kn + jnp.concatenate([kr] * MLA_HEADS, axis=1)).astype(BF16)
    lane_t = lax.broadcasted_iota(I32, kn.shape, 1)
    ones_col = jnp.where(lane_t % HEAD_PAD == MLA_V, 1.0, 0.0)
    v_ref[...] = (_dot(ckvn, wv_ref[...]) + ones_col).astype(BF16)

    dq_ref[...] = (p[:, _C_DQ:_C_DK] * (LOG2E / math.sqrt(DIFF_HD))).astype(BF16)
    dk_ref[...] = p[:, _C_DK:_C_DV].astype(BF16)
    dv_ref[...] = p[:, _C_DV:_C_END].astype(BF16)


def _pre_attn(x2, mod3, pos_col, attn_pre_g, w_in, q_norm_g, kv_norm_g, w_uq, w_ukv, S):
    T, D = x2.shape
    tm = min(TM_PRE, S)
    tpb = S // tm
    f = lambda a: a.astype(BF16)
    d3 = DIFF_HEADS * DIFF_V
    a = MLA_Q_RANK
    b = a + MLA_KV_RANK
    c = b + MLA_ROPE

    def swap(r):
        hlf = MLA_ROPE // 2
        return jnp.concatenate([-r[..., hlf:], r[..., :hlf]], axis=-1)

    def pad_rope(r):
        return jnp.pad(r, ((0, 0), (MLA_NOPE, LANE - MLA_NOPE - MLA_ROPE)))

    w_kr = w_in[:, b:c]
    w1 = jnp.concatenate([w_in[:, :b], pad_rope(w_kr), pad_rope(swap(w_kr)), w_in[:, c:]], axis=1)
    assert w1.shape[1] == _C_END
    padq = HEAD_PAD - MLA_NOPE - MLA_ROPE
    wqa = jnp.pad(w_uq, ((0, 0), (0, 0), (0, padq))).reshape(MLA_Q_RANK, MLA_HEADS * HEAD_PAD)
    q_rope = w_uq[..., MLA_NOPE:]
    wqb = jnp.pad(swap(q_rope), ((0, 0), (0, 0), (MLA_NOPE, padq))).reshape(MLA_Q_RANK, MLA_HEADS * HEAD_PAD)
    wkn = jnp.pad(w_ukv[..., :MLA_NOPE], ((0, 0), (0, 0), (0, HEAD_PAD - MLA_NOPE))).reshape(
        MLA_KV_RANK, MLA_HEADS * HEAD_PAD)
    wv = jnp.pad(w_ukv[..., MLA_NOPE:], ((0, 0), (0, 0), (0, HEAD_PAD - MLA_V))).reshape(
        MLA_KV_RANK, MLA_HEADS * HEAD_PAD)

    inv = 1.0 / (ROPE_BASE ** (jnp.arange(0, MLA_ROPE, 2, dtype=F32) / MLA_ROPE))
    inv_lane = jnp.pad(jnp.concatenate([inv, inv]), (MLA_NOPE, LANE - MLA_NOPE - MLA_ROPE)).reshape(1, LANE)

    full = lambda arr: pl.BlockSpec(arr.shape, lambda i: (0,) * arr.ndim)
    row = lambda w: pl.BlockSpec((tm, w), lambda i: (i, 0))
    ins = [x2, mod3, pos_col, inv_lane, attn_pre_g.reshape(1, D), f(w1), q_norm_g.reshape(1, -1),
           kv_norm_g.reshape(1, -1), f(wqa), f(wqb), f(wkn), f(wv)]
    in_specs = [row(D), pl.BlockSpec((None, 6, D), lambda i: (i // tpb, 0, 0)), row(1)] + \
               [full(arr) for arr in ins[3:]]
    widths = [MLA_HEADS * HEAD_PAD, MLA_HEADS * HEAD_PAD, MLA_HEADS * HEAD_PAD, d3, d3, d3]
    return pl.pallas_call(
        _pre_attn_kernel,
        out_shape=[jax.ShapeDtypeStruct((T, w), BF16) for w in widths],
        grid=(T // tm,),
        in_specs=in_specs,
        out_specs=[row(w) for w in widths],
        compiler_params=_cparams("parallel"),
        name="pre_attn",
    )(*ins)


def _online_update(s, m, l, acc, v):
    m_new = jnp.maximum(m, jnp.max(s, axis=-1, keepdims=True))
    p = jnp.exp2(s - m_new)
    alpha = jnp.exp2(m - m_new)
    l_new = alpha * l + jnp.sum(p, axis=-1, keepdims=True)
    acc_new = alpha * acc + _dot(p.astype(BF16), v)
    return m_new, l_new, acc_new


MLA_HPS = 4


def _causal_blocks(i, tq, tk):
    nfull = (i * tq) // tk
    diag = [(nfull + j, i * tq - (nfull + j) * tk) for j in range(max(1, tq // tk))]
    return nfull, diag


def _keep(tq, tk, off):
    r = lax.broadcasted_iota(I32, (tq, tk), 0)
    c = lax.broadcasted_iota(I32, (tq, tk), 1)
    return r + off >= c


def _mla_attn_kernel(tk, q_ref, k_ref, v_ref, o_ref):
    i = pl.program_id(2)
    tq = q_ref.shape[0]
    sl = [slice(j * HEAD_PAD, (j + 1) * HEAD_PAD) for j in range(MLA_HPS)]
    qs = [q_ref[:, sl[j]] for j in range(MLA_HPS)]

    def step(kb, carry, off=None):
        r0 = pl.multiple_of(kb * tk, tk)
        out = []
        for j in range(MLA_HPS):
            m, acc = carry[j]
            s = _dot_nt(qs[j], k_ref[pl.ds(r0, tk), sl[j]])
            if off is not None:
                s = jnp.where(_keep(tq, tk, off), s, -jnp.inf)
            m_new = jnp.maximum(m, jnp.max(s, axis=-1, keepdims=True))
            p = jnp.exp2(s - m_new)
            acc = jnp.exp2(m - m_new) * acc + _dot(p.astype(BF16), v_ref[pl.ds(r0, tk), sl[j]])
            out.append((m_new, acc))
        return tuple(out)

    nfull, diag = _causal_blocks(i, tq, tk)
    init = tuple((jnp.full((tq, 1), -jnp.inf, F32), jnp.zeros((tq, HEAD_PAD), F32)) for _ in range(MLA_HPS))
    carry = lax.fori_loop(0, nfull, step, init)
    for kb, off in diag:
        carry = step(kb, carry, off)
    outs = [acc[:, :MLA_V] / acc[:, MLA_V:MLA_V + 1] for _, acc in carry]
    o_ref[...] = jnp.concatenate(outs, axis=1).astype(o_ref.dtype)


def _mla_attn(q, k, v, B, S):
    tq = min(TQ, S)
    tk = min(TK, S)
    q3 = q.reshape(B, S, -1)
    k3 = k.reshape(B, S, -1)
    v3 = v.reshape(B, S, -1)
    w = MLA_HPS * HEAD_PAD
    return pl.pallas_call(
        functools.partial(_mla_attn_kernel, tk),
        out_shape=jax.ShapeDtypeStruct((B, S, MLA_HEADS * MLA_V), BF16),
        grid=(B, MLA_HEADS // MLA_HPS, S // tq),
        in_specs=[pl.BlockSpec((None, tq, w), lambda b, h, i: (b, i, h)),
                  pl.BlockSpec((None, S, w), lambda b, h, i: (b, 0, h)),
                  pl.BlockSpec((None, S, w), lambda b, h, i: (b, 0, h))],
        out_specs=pl.BlockSpec((None, tq, MLA_HPS * MLA_V), lambda b, h, i: (b, i, h)),
        compiler_params=_cparams("parallel", "parallel", "arbitrary"),
        name="mla_attn",
    )(q3, k3, v3)


DIFF_HPS = 2


def _diff_attn_kernel(lambda_init, tk, q_ref, k_ref, v_ref, pc_ref, pr_ref, lam_ref, g_ref, o_ref):
    i = pl.program_id(2)
    tq = q_ref.shape[0]
    sl = [slice(j * DIFF_V, (j + 1) * DIFF_V) for j in range(DIFF_HPS)]
    lane = lax.broadcasted_iota(I32, (tq, DIFF_V), 1)
    qs, nslopes = [], []
    for j in range(DIFF_HPS):
        q = q_ref[:, sl[j]]
        zero = jnp.zeros_like(q)
        qs.append((jnp.where(lane < DIFF_HD, q, zero), jnp.where(lane >= DIFF_HD, q, zero)))
        hv = jnp.full((1, 1), pl.program_id(1) * DIFF_HPS + j, I32).astype(F32)
        nslopes.append(-LOG2E * jnp.exp2(-8.0 * (hv + 1.0) / DIFF_HEADS))
    pq = pc_ref[...]

    def step(kb, carry, off=None):
        r0 = pl.multiple_of(kb * tk, tk)
        dist = jnp.abs(pq - pr_ref[:, pl.ds(r0, tk)])
        keep = None if off is None else _keep(tq, tk, off)
        out = []
        for j in range(DIFF_HPS):
            kblk = k_ref[pl.ds(r0, tk), sl[j]]
            vblk = v_ref[pl.ds(r0, tk), sl[j]]
            bias = nslopes[j] * dist
            for c in range(2):
                s = _dot_nt(qs[j][c], kblk) + bias
                if keep is not None:
                    s = jnp.where(keep, s, -jnp.inf)
                out.append(_online_update(s, *carry[2 * j + c], vblk))
        return tuple(out)

    nfull, diag = _causal_blocks(i, tq, tk)
    init1 = (jnp.full((tq, 1), -jnp.inf, F32), jnp.zeros((tq, 1), F32), jnp.zeros((tq, DIFF_V), F32))
    carry = lax.fori_loop(0, nfull, step, (init1,) * (2 * DIFF_HPS))
    for kb, off in diag:
        carry = step(kb, carry, off)

    lv = lam_ref[...]
    lam = (jnp.exp(jnp.sum(lv[0:1] * lv[1:2], axis=-1, keepdims=True))
           - jnp.exp(jnp.sum(lv[2:3] * lv[3:4], axis=-1, keepdims=True)) + lambda_init)
    outs = []
    for j in range(DIFF_HPS):
        (_, l1, a1), (_, l2, a2) = carry[2 * j], carry[2 * j + 1]
        o = a1 / l1 - lam * (a2 / l2)
        outs.append(_rms(o, SUBLN_EPS) * g_ref[...] * (1.0 - lambda_init))
    o_ref[...] = jnp.concatenate(outs, axis=1).astype(o_ref.dtype)


def _diff_attn(dq, dk, dv, pos_col, pos_row, lam4, subln_g, lambda_init, B, S):
    tq = min(TQ, S)
    tk = min(TK, S)
    nq = S // tq
    q3 = dq.reshape(B, S, -1)
    k3 = dk.reshape(B, S, -1)
    v3 = dv.reshape(B, S, -1)
    w = DIFF_HPS * DIFF_V
    return pl.pallas_call(
        functools.partial(_diff_attn_kernel, lambda_init, tk),
        out_shape=jax.ShapeDtypeStruct((B, S, DIFF_HEADS * DIFF_V), BF16),
        grid=(B, DIFF_HEADS // DIFF_HPS, nq),
        in_specs=[pl.BlockSpec((None, tq, w), lambda b, h, i: (b, i, h)),
                  pl.BlockSpec((None, S, w), lambda b, h, i: (b, 0, h)),
                  pl.BlockSpec((None, S, w), lambda b, h, i: (b, 0, h)),
                  pl.BlockSpec((tq, 1), lambda b, h, i: (b * nq + i, 0)),
                  pl.BlockSpec((None, 1, S), lambda b, h, i: (b, 0, 0)),
                  pl.BlockSpec((4, DIFF_HD), lambda b, h, i: (0, 0)),
                  pl.BlockSpec((1, DIFF_V), lambda b, h, i: (0, 0))],
        out_specs=pl.BlockSpec((None, tq, w), lambda b, h, i: (b, i, h)),
        compiler_params=_cparams("parallel", "parallel", "arbitrary"),
        name="diff_attn",
    )(q3, k3, v3, pos_col, pos_row, lam4, subln_g.reshape(1, DIFF_V))


def _first_argmax(v, io, n, axis):
    m = jnp.max(v, axis=axis, keepdims=True)
    ix = jnp.min(jnp.where(v == m, io, n), axis=axis, keepdims=True)
    return m, ix


def _post_attn_kernel(x_ref, mla_ref, dif_ref, mod_ref, woa_ref, wob_ref, pg_ref, fg_ref,
                      rwt_ref, rb_ref,
                      x1_ref, h2_ref, idx_ref, wts_ref, rank_ref, cnt_ref, run_ref):
    step = pl.program_id(0)
    tm = x_ref.shape[0]
    E, G, GS = N_EXPERTS, N_GROUPS, GROUP_SIZE

    @pl.when(step == 0)
    def _():
        run_ref[...] = jnp.zeros_like(run_ref)

    g_a = mod_ref[2:3, :]
    sh_f = mod_ref[3:4, :]
    sc_f = mod_ref[4:5, :]
    y = _dot(mla_ref[...], woa_ref[...]) + _dot(dif_ref[...], wob_ref[...])
    x1 = x_ref[...] + g_a * (_rms(y, NORM_EPS) * pg_ref[...])
    x1_ref[...] = x1
    h2 = (_rms(x1, NORM_EPS) * fg_ref[...]) * (1.0 + sc_f) + sh_f
    _rt_store(h2_ref, _pack_rows(h2))

    logits = lax.dot_general(rwt_ref[...], h2, (((1,), (1,)), ((), ())),
                             preferred_element_type=F32, precision=lax.Precision.HIGHEST)
    scores = _sigmoid(logits)
    sel = scores + rb_ref[...]

    sio = lax.broadcasted_iota(I32, (GS, tm), 0)
    gs_rows = []
    for g in range(G):
        blk = sel[g * GS:(g + 1) * GS, :]
        m1, i1 = _first_argmax(blk, sio, GS, 0)
        m2 = jnp.max(jnp.where(sio == i1, -jnp.inf, blk), axis=0, keepdims=True)
        gs_rows.append(m1 + m2)
    gs = jnp.concatenate(gs_rows, axis=0)

    gio = lax.broadcasted_iota(I32, (G, tm), 0)
    gkeep = jnp.zeros((G, tm), F32)
    for _ in range(TOPK_GROUPS):
        _, ix = _first_argmax(gs, gio, G, 0)
        pick = gio == ix
        gkeep = jnp.where(pick, 1.0, gkeep)
        gs = jnp.where(pick, -jnp.inf, gs)
    ekeep = jnp.concatenate([jnp.broadcast_to(gkeep[g:g + 1, :], (GS, tm)) for g in range(G)], axis=0)
    cand = jnp.where(ekeep > 0.0, sel, -jnp.inf)

    eio = lax.broadcasted_iota(I32, (E, tm), 0)
    idx_rows, w_rows = [], []
    for _ in range(TOP_K):
        _, ix = _first_argmax(cand, eio, E, 0)
        pick = eio == ix
        w_rows.append(jnp.sum(jnp.where(pick, scores, 0.0), axis=0, keepdims=True))
        cand = jnp.where(pick, -jnp.inf, cand)
        idx_rows.append(ix)
    idx = jnp.concatenate(idx_rows, axis=0)
    w = jnp.concatenate(w_rows, axis=0)
    wts_ref[...] = w / jnp.sum(w, axis=0, keepdims=True) * ROUTED_SCALE
    idx_ref[...] = idx

    onehot = jnp.zeros((E, tm), F32)
    for k in range(TOP_K):
        onehot = onehot + jnp.where(eio == idx_rows[k], 1.0, 0.0)
    tr = lax.broadcasted_iota(I32, (tm, tm), 0)
    tc = lax.broadcasted_iota(I32, (tm, tm), 1)
    before = jnp.where(tr < tc, 1.0, 0.0).astype(BF16)
    prior = _dot(onehot.astype(BF16), before) + run_ref[...]
    rank_rows = [jnp.sum(jnp.where(eio == idx_rows[k], prior, 0.0), axis=0, keepdims=True)
                 for k in range(TOP_K)]
    rank_ref[...] = jnp.concatenate(rank_rows, axis=0).astype(I32)
    run_ref[...] += jnp.sum(onehot, axis=1, keepdims=True)
    cnt_ref[...] = run_ref[...].astype(I32)


def _post_attn(x2, mla, dif, mod3, w_o, attn_post_g, ffn_pre_g, router_w, router_b, S):
    T, D = x2.shape
    tm = min(TM_POST, S)
    tpb = S // tm
    half = MLA_HEADS * MLA_V
    woa = w_o[:half].astype(BF16)
    wob = w_o[half:].astype(BF16)
    rwt = router_w.T
    full = lambda arr: pl.BlockSpec(arr.shape, lambda i: (0,) * arr.ndim)
    row = lambda w: pl.BlockSpec((tm, w), lambda i: (i, 0))
    col = lambda r: pl.BlockSpec((r, tm), lambda i: (0, i))
    ins = [x2, mla.reshape(T, -1), dif.reshape(T, -1), mod3, woa, wob, attn_post_g.reshape(1, D),
           ffn_pre_g.reshape(1, D), rwt, router_b.reshape(N_EXPERTS, 1)]
    in_specs = [row(D), row(half), row(D - half), pl.BlockSpec((None, 6, D), lambda i: (i // tpb, 0, 0))] + \
               [full(arr) for arr in ins[4:]]
    return pl.pallas_call(
        _post_attn_kernel,
        out_shape=[jax.ShapeDtypeStruct((T, D), F32), jax.ShapeDtypeStruct((T * RT, LANE), I32),
                   jax.ShapeDtypeStruct((TOP_K, T), I32), jax.ShapeDtypeStruct((TOP_K, T), F32),
                   jax.ShapeDtypeStruct((TOP_K, T), I32), jax.ShapeDtypeStruct((N_EXPERTS, 1), I32)],
        grid=(T // tm,),
        in_specs=in_specs,
        out_specs=[row(D), pl.BlockSpec((tm * RT, LANE), lambda i: (i, 0)), col(TOP_K), col(TOP_K), col(TOP_K),
                   pl.BlockSpec((N_EXPERTS, 1), lambda i: (0, 0))],
        scratch_shapes=[pltpu.VMEM((N_EXPERTS, 1), F32)],
        compiler_params=_cparams("arbitrary"),
        name="post_attn",
    )(*ins)


def _dest_kernel(idx_ref, rank_ref, ps_ref, o_ref):
    idx = idx_ref[...]
    tm = idx.shape[1]
    eio = lax.broadcasted_iota(I32, (N_EXPERTS, tm), 0)
    ps = ps_ref[...]
    rows = [jnp.sum(jnp.where(eio == idx[k:k + 1, :], ps, 0.0), axis=0, keepdims=True)
            for k in range(TOP_K)]
    o_ref[...] = jnp.concatenate(rows, axis=0).astype(I32) + rank_ref[...]


def _dest(idx, rank, pstart):
    K, T = idx.shape
    tm = min(2048, T)
    col = pl.BlockSpec((K, tm), lambda i: (0, i))
    return pl.pallas_call(
        _dest_kernel,
        out_shape=jax.ShapeDtypeStruct((K, T), I32),
        grid=(T // tm,),
        in_specs=[col, col, pl.BlockSpec((N_EXPERTS, 1), lambda i: (0, 0))],
        out_specs=col,
        compiler_params=_cparams("parallel"),
        name="dest",
    )(idx, rank, pstart.astype(F32).reshape(N_EXPERTS, 1))


def _dispatch_kernel(bs_ref, cnt_ref, dest_ref, h_ref, xs_ref, zeros, sem, zsem):
    tm = h_ref.shape[0] // RT
    blk = BM * RT
    nb = xs_ref.shape[0] // blk

    @pl.when(pl.program_id(0) == 0)
    def _():
        zeros[...] = jnp.zeros_like(zeros)

        def zcopy(g):
            return pltpu.make_async_copy(zeros, xs_ref.at[pl.ds(pl.multiple_of(g * blk, blk), blk), :], zsem)

        def has_pad(e):
            return cnt_ref[e] % BM != 0

        def fill(e, _):
            @pl.when(has_pad(e))
            def _():
                zcopy(bs_ref[e + 1] - 1).start()
            return 0

        def drain(e, _):
            @pl.when(has_pad(e))
            def _():
                zcopy(0).wait()
            return 0

        lax.fori_loop(0, N_EXPERTS, fill, 0)
        lax.fori_loop(bs_ref[N_EXPERTS], nb, lambda g, _: (zcopy(g).start(), 0)[1], 0)
        lax.fori_loop(0, N_EXPERTS, drain, 0)
        lax.fori_loop(bs_ref[N_EXPERTS], nb, lambda g, _: (zcopy(0).wait(), 0)[1], 0)

    def tile(ref, r):
        return ref.at[pl.ds(pl.multiple_of(r * RT, RT), RT), :]

    def issue(t, _):
        for k in range(TOP_K):
            pltpu.make_async_copy(tile(h_ref, t), tile(xs_ref, dest_ref[k, t]), sem).start(priority=k % 2)
        return 0

    lax.fori_loop(0, tm, issue, 0)
    for _ in range(TOP_K):
        pltpu.make_async_copy(h_ref, xs_ref.at[pl.ds(0, tm * RT), :], sem).wait()


def _dispatch(h2rt, dest, bstart, cnt, P):
    T = h2rt.shape[0] // RT
    tm = min(TM_DISP, T)
    nt = T // tm
    dest3 = dest.reshape(TOP_K, nt, tm).transpose(1, 0, 2)
    return pl.pallas_call(
        _dispatch_kernel,
        out_shape=jax.ShapeDtypeStruct((P * RT, LANE), I32),
        grid_spec=pltpu.PrefetchScalarGridSpec(
            num_scalar_prefetch=2,
            grid=(nt,),
            in_specs=[pl.BlockSpec((None, TOP_K, tm), lambda i, bs, cn: (i, 0, 0), memory_space=pltpu.SMEM),
                      pl.BlockSpec((tm * RT, LANE), lambda i, bs, cn: (i, 0))],
            out_specs=pl.BlockSpec(memory_space=pl.ANY),
            scratch_shapes=[pltpu.VMEM((BM * RT, LANE), I32), pltpu.SemaphoreType.DMA(()),
                            pltpu.SemaphoreType.DMA(())],
        ),
        compiler_params=_cparams("arbitrary"),
        name="dispatch",
    )(bstart, cnt, dest3, h2rt)


NBUF_X = 6
X_AHEAD = NBUF_X - 2
NBUF_Y = 4


def _experts_kernel(bs_ref, wg_ref, wu_ref, wd_ref, xs_ref, ys_ref, wgb, wub, wdb, xbuf, ybuf, xsem, ysem):
    e = pl.program_id(0)
    blk = BM * RT
    nb = xs_ref.shape[0] // blk
    nused = bs_ref[N_EXPERTS]
    g0 = bs_ref[e]
    g1 = bs_ref[e + 1]

    def rows(g):
        return pl.ds(pl.multiple_of(g * blk, blk), blk)

    def x_copy(g, slot):
        return pltpu.make_async_copy(xs_ref.at[rows(g), :], xbuf.at[slot], xsem.at[slot])

    def y_copy(g, slot):
        return pltpu.make_async_copy(ybuf.at[slot], ys_ref.at[rows(g), :], ysem.at[slot])

    @pl.when(e == 0)
    def _():
        for j in range(X_AHEAD):
            @pl.when(j < nused)
            def _():
                x_copy(j, j).start()

    def step(gs):
        for g in gs:
            x_copy(g, g % NBUF_X).wait()
            nxt = g + X_AHEAD

            @pl.when(nxt < nused)
            def _():
                x_copy(nxt, nxt % NBUF_X).start()

        x = _unpack_rows(jnp.concatenate([_rt_load(xbuf, BM, g % NBUF_X) for g in gs], axis=0), BF16)
        gate = _dot(x, wgb[...])
        up = _dot(x, wub[...])
        a = (gate * _sigmoid(gate)) * up
        y = _dot(a.astype(BF16), wdb[...])
        for j, g in enumerate(gs):
            ys = g % NBUF_Y

            @pl.when(g >= NBUF_Y)
            def _():
                y_copy(g - NBUF_Y, ys).wait()

            _rt_store(ybuf, _pack_rows(y[j * BM:(j + 1) * BM]), ys)
            y_copy(g, ys).start(priority=1)

    @pl.when(g1 > g0)
    def _():
        wgb[...] = wg_ref[...].astype(BF16)
        wub[...] = wu_ref[...].astype(BF16)
        wdb[...] = wd_ref[...].astype(BF16)

        def pair(i, _):
            g = g0 + 2 * i
            step([g, g + 1])
            return 0

        lax.fori_loop(0, (g1 - g0) // 2, pair, 0)

        @pl.when((g1 - g0) % 2 == 1)
        def _():
            step([g1 - 1])

    @pl.when(e == N_EXPERTS - 1)
    def _():
        for j in range(NBUF_Y):
            @pl.when(nused - 1 - j >= 0)
            def _():
                y_copy(0, (nused - 1 - j) % NBUF_Y).wait()
        ybuf[0] = jnp.zeros(ybuf.shape[1:], I32)
        lax.fori_loop(nused, nb, lambda g, _: (y_copy(g, 0).start(), 0)[1], 0)
        lax.fori_loop(nused, nb, lambda g, _: (y_copy(0, 0).wait(), 0)[1], 0)


def _experts(xs, bstart, w_gate, w_up, w_down):
    E, D, F = w_gate.shape
    return pl.pallas_call(
        _experts_kernel,
        out_shape=jax.ShapeDtypeStruct(xs.shape, I32),
        grid_spec=pltpu.PrefetchScalarGridSpec(
            num_scalar_prefetch=1,
            grid=(E,),
            in_specs=[pl.BlockSpec((None, D, F), lambda e, bs: (e, 0, 0)),
                      pl.BlockSpec((None, D, F), lambda e, bs: (e, 0, 0)),
                      pl.BlockSpec((None, F, D), lambda e, bs: (e, 0, 0)),
                      pl.BlockSpec(memory_space=pl.ANY)],
            out_specs=pl.BlockSpec(memory_space=pl.ANY),
            scratch_shapes=[pltpu.VMEM((D, F), BF16), pltpu.VMEM((D, F), BF16), pltpu.VMEM((F, D), BF16),
                            pltpu.VMEM((NBUF_X, BM * RT, LANE), I32), pltpu.VMEM((NBUF_Y, BM * RT, LANE), I32),
                            pltpu.SemaphoreType.DMA((NBUF_X,)), pltpu.SemaphoreType.DMA((NBUF_Y,))],
        ),
        compiler_params=_cparams("arbitrary"),
        name="experts",
    )(bstart, w_gate, w_up, w_down, xs)


def _combine_kernel(dcur_ref, dnxt_ref, w_ref, x1_ref, h_ref, mod_ref, pg_ref, sg_ref, su_ref, sd_ref,
                    ys_ref, o_ref, rows_a, rows_b, sem):
    i = pl.program_id(0)
    n = pl.num_programs(0)
    tm = x1_ref.shape[0]

    def tile(r):
        return pl.ds(pl.multiple_of(r * RT, RT), RT)

    def request(dref, buf, s, t):
        for k in range(TOP_K):
            pltpu.make_async_copy(ys_ref.at[tile(dref[k, t]), :], buf.at[k, tile(t), :],
                                  sem.at[s]).start(priority=k % 2)

    def drain(buf, s):
        for k in range(TOP_K):
            pltpu.make_async_copy(ys_ref.at[pl.ds(0, tm * RT), :], buf.at[k], sem.at[s]).wait()

    @pl.when(i == 0)
    def _():
        lax.fori_loop(0, tm, lambda t, _: (request(dcur_ref, rows_a, 0, t), 0)[1], 0)

    def step(cur, s_cur, nxt, s_nxt):
        drain(cur, s_cur)
        for t in range(tm):
            request(dnxt_ref, nxt, s_nxt, t)
        w = w_ref[...]
        routed = w[:, 0:1] * _unpack_rows(_rt_load(cur, tm, 0), F32)
        for k in range(1, TOP_K):
            routed = routed + w[:, k:k + 1] * _unpack_rows(_rt_load(cur, tm, k), F32)
        hb = _unpack_rows(_rt_load(h_ref, tm), BF16)
        g = _dot(hb, sg_ref[...])
        u = _dot(hb, su_ref[...])
        shared = _dot(((g * _sigmoid(g)) * u).astype(BF16), sd_ref[...])
        y = routed + shared
        g_f = mod_ref[5:6, :]
        o_ref[...] = x1_ref[...] + g_f * (_rms(y, NORM_EPS) * pg_ref[...])

    even = i % 2 == 0

    @pl.when(even)
    def _():
        step(rows_a, 0, rows_b, 1)

    @pl.when(jnp.logical_not(even))
    def _():
        step(rows_b, 1, rows_a, 0)

    @pl.when(i == n - 1)
    def _():
        @pl.when(even)
        def _():
            drain(rows_b, 1)

        @pl.when(jnp.logical_not(even))
        def _():
            drain(rows_a, 0)


def _combine(dest, wts, x1, h2, mod3, ffn_post_g, sw_gate, sw_up, sw_down, ys, S):
    T, D = x1.shape
    tm = min(TM_COMB, S)
    nt = T // tm
    tpb = S // tm
    dest3 = dest.reshape(TOP_K, nt, tm).transpose(1, 0, 2)
    w_tk = wts.T
    full = lambda arr: pl.BlockSpec(arr.shape, lambda i: (0,) * arr.ndim)
    row = lambda w: pl.BlockSpec((tm, w), lambda i: (i, 0))
    sg, su, sd = sw_gate.astype(BF16), sw_up.astype(BF16), sw_down.astype(BF16)
    pg = ffn_post_g.reshape(1, D)
    return pl.pallas_call(
        _combine_kernel,
        out_shape=jax.ShapeDtypeStruct((T, D), F32),
        grid=(nt,),
        in_specs=[pl.BlockSpec((None, TOP_K, tm), lambda i: (i, 0, 0), memory_space=pltpu.SMEM),
                  pl.BlockSpec((None, TOP_K, tm), lambda i: (jnp.minimum(i + 1, nt - 1), 0, 0),
                               memory_space=pltpu.SMEM),
                  row(TOP_K), row(D), pl.BlockSpec((tm * RT, LANE), lambda i: (i, 0)),
                  pl.BlockSpec((None, 6, D), lambda i: (i // tpb, 0, 0)),
                  full(pg), full(sg), full(su), full(sd),
                  pl.BlockSpec(memory_space=pl.ANY)],
        out_specs=row(D),
        scratch_shapes=[pltpu.VMEM((TOP_K, tm * RT, LANE), I32), pltpu.VMEM((TOP_K, tm * RT, LANE), I32),
                        pltpu.SemaphoreType.DMA((2,))],
        compiler_params=_cparams("arbitrary"),
        name="combine",
    )(dest3, dest3, w_tk, x1, h2, mod3, pg, sg, su, sd, ys)


def _moe(h2, x1, idx, wts, rank, counts, mod3, ffn_post_g, exp_w_gate, exp_w_up, exp_w_down,
         sw_gate, sw_up, sw_down, S):
    T, D = x1.shape
    A = T * TOP_K
    P = A + N_EXPERTS * BM
    cnt = counts.reshape(N_EXPERTS)
    blocks = (cnt + BM - 1) // BM
    bstart = jnp.concatenate([jnp.zeros((1,), I32), jnp.cumsum(blocks).astype(I32)])
    pstart = bstart[:-1] * BM

    dest = _dest(idx, rank, pstart)
    xs = _dispatch(h2, dest, bstart, cnt, P)
    ys = _experts(xs, bstart, exp_w_gate, exp_w_up, exp_w_down)
    return _combine(dest, wts, x1, h2, mod3, ffn_post_g, sw_gate, sw_up, sw_down, ys, S)


def _layer(x, c, positions, lambda_init, w_ada, b_ada, attn_pre_g, attn_post_g, w_in, q_norm_g, kv_norm_g,
           w_uq, w_ukv, lam_q1, lam_k1, lam_q2, lam_k2, diff_subln_g, w_o, ffn_pre_g, ffn_post_g,
           router_w, router_b, exp_w_gate, exp_w_up, exp_w_down, sw_gate, sw_up, sw_down):
    B, S, D = x.shape
    T = B * S
    x2 = x.reshape(T, D)
    posf = positions.astype(F32)
    pos_col = posf.reshape(T, 1)
    pos_row = posf.reshape(B, 1, S)

    mod3 = _ada(c, w_ada, b_ada).reshape(B, 6, D)
    q, k, v, dq, dk, dv = _pre_attn(x2, mod3, pos_col, attn_pre_g, w_in, q_norm_g, kv_norm_g, w_uq, w_ukv, S)
    mla = _mla_attn(q, k, v, B, S)
    lam4 = jnp.stack([lam_q1, lam_k1, lam_q2, lam_k2])
    dif = _diff_attn(dq, dk, dv, pos_col, pos_row, lam4, diff_subln_g, lambda_init, B, S)
    x1, h2, idx, wts, rank, counts = _post_attn(x2, mla, dif, mod3, w_o, attn_post_g, ffn_pre_g,
                                                router_w, router_b, S)
    out = _moe(h2, x1, idx, wts, rank, counts, mod3, ffn_post_g, exp_w_gate, exp_w_up, exp_w_down,
               sw_gate, sw_up, sw_down, S)
    return out.reshape(B, S, D)


def kernel(x, c, positions, w_ada, b_ada, attn_pre_g, attn_post_g, w_in, q_norm_g, kv_norm_g, w_uq, w_ukv,
           lam_q1, lam_k1, lam_q2, lam_k2, diff_subln_g, w_o, ffn_pre_g, ffn_post_g, router_w, router_b,
           exp_w_gate, exp_w_up, exp_w_down, shared_w_gate, shared_w_up, shared_w_down):
    depth = w_ada.shape[0]
    for l in range(depth):
        lambda_init = 0.8 - 0.6 * math.exp(-0.3 * l)
        x = _layer(x, c, positions, lambda_init, w_ada[l], b_ada[l], attn_pre_g[l], attn_post_g[l], w_in[l],
                   q_norm_g[l], kv_norm_g[l], w_uq[l], w_ukv[l], lam_q1[l], lam_k1[l], lam_q2[l], lam_k2[l],
                   diff_subln_g[l], w_o[l], ffn_pre_g[l], ffn_post_g[l], router_w[l], router_b[l],
                   exp_w_gate[l], exp_w_up[l], exp_w_down[l], shared_w_gate[l], shared_w_up[l],
                   shared_w_down[l])
    return x
```

```python
import functools
import math

import jax
import jax.numpy as jnp
from jax import lax
from jax.experimental import pallas as pl
from jax.experimental.pallas import tpu as pltpu

F32 = jnp.float32
BF16 = jnp.bfloat16
I32 = jnp.int32

MLA_HEADS = 8
MLA_NOPE = 64
MLA_ROPE = 32
MLA_V = 64
MLA_Q_RANK = 256
MLA_KV_RANK = 128
ROPE_BASE = 10000.0
DIFF_HEADS = 4
DIFF_HD = 64
DIFF_V = 128
N_EXPERTS = 256
TOP_K = 8
N_GROUPS = 8
GROUP_SIZE = N_EXPERTS // N_GROUPS
TOPK_GROUPS = 4
ROUTED_SCALE = 2.5
NORM_EPS = 1e-6
SUBLN_EPS = 1e-5
LOG2E = 1.4426950408889634

LANE = 128
HEAD_PAD = 128

TM_PRE = 512
TQ = 512
TK = 512
TM_POST = 512
TM_DISP = 1024
TM_COMB = 128
BM = 128
VMEM_LIMIT = 48 * 1024 * 1024


def _cparams(*sem):
    return pltpu.CompilerParams(dimension_semantics=sem, vmem_limit_bytes=VMEM_LIMIT)


def _rms(x, eps):
    return x * lax.rsqrt(jnp.mean(x * x, axis=-1, keepdims=True) + eps)


def _sigmoid(x):
    return 1.0 / (1.0 + jnp.exp(-x))


def _dot(a, b):
    return jnp.dot(a, b, preferred_element_type=F32)


RT = 4
HI_MASK = -65536


def _pack_rows(x):
    half = x.shape[1] // 2
    lo = lax.bitcast_convert_type(x[:, :half].astype(BF16).astype(F32), I32)
    hi = lax.bitcast_convert_type(x[:, half:].astype(BF16).astype(F32), I32)
    return lax.shift_right_logical(lo, 16) | (hi & HI_MASK)


def _unpack_rows(u, dtype):
    lo = lax.bitcast_convert_type(lax.shift_left(u, 16), F32)
    hi = lax.bitcast_convert_type(u & HI_MASK, F32)
    return jnp.concatenate([lo.astype(dtype), hi.astype(dtype)], axis=1)


def _rt_load(ref, n, *lead):
    return jnp.concatenate([ref[(*lead, pl.ds(j, n, stride=RT), slice(None))] for j in range(RT)], axis=1)


def _rt_store(ref, val, *lead):
    n = val.shape[0]
    for j in range(RT):
        ref[(*lead, pl.ds(j, n, stride=RT), slice(None))] = val[:, j * LANE:(j + 1) * LANE]


def _dot_nt(a, b):
    return lax.dot_general(a, b, (((1,), (1,)), ((), ())), preferred_element_type=F32)


def _ada_kernel(c_ref, w_ref, b_ref, o_ref):
    c = c_ref[...]
    a = c * _sigmoid(c)
    o_ref[...] = jnp.dot(a, w_ref[...], preferred_element_type=F32,
                         precision=lax.Precision.HIGHEST) + b_ref[...]


def _ada(c, w_ada, b_ada):
    B, D = c.shape
    n = w_ada.shape[1]
    return pl.pallas_call(
        _ada_kernel,
        out_shape=jax.ShapeDtypeStruct((B, n), F32),
        grid=(n // D,),
        in_specs=[pl.BlockSpec((B, D), lambda j: (0, 0)),
                  pl.BlockSpec((D, D), lambda j: (0, j)),
                  pl.BlockSpec((1, D), lambda j: (0, j))],
        out_specs=pl.BlockSpec((B, D), lambda j: (0, j)),
        compiler_params=_cparams("arbitrary"),
        name="ada",
    )(c, w_ada, b_ada.reshape(1, n))


_C_CQ = 0
_C_CKV = _C_CQ + MLA_Q_RANK
_C_KRA = _C_CKV + MLA_KV_RANK
_C_KRB = _C_KRA + LANE
_C_DQ = _C_KRB + LANE
_C_DK = _C_DQ + DIFF_HEADS * DIFF_V
_C_DV = _C_DK + DIFF_HEADS * DIFF_V
_C_END = _C_DV + DIFF_HEADS * DIFF_V


def _pre_attn_kernel(x_ref, mod_ref, pos_ref, inv_ref, g_ref, w1_ref, qg_ref, kvg_ref,
                     wqa_ref, wqb_ref, wkn_ref, wv_ref,
                     q_ref, k_ref, v_ref, dq_ref, dk_ref, dv_ref):
    x = x_ref[...]
    sh = mod_ref[0:1, :]
    sc = mod_ref[1:2, :]
    h = _rms(x, NORM_EPS) * g_ref[...]
    h = h * (1.0 + sc) + sh
    p = _dot(h.astype(BF16), w1_ref[...])

    ang = pos_ref[...] * inv_ref[...]
    lane = lax.broadcasted_iota(I32, ang.shape, 1)
    in_rope = (lane >= MLA_NOPE) & (lane < MLA_NOPE + MLA_ROPE)
    cos_r = jnp.where(in_rope, jnp.cos(ang), 0.0)
    sin_r = jnp.where(in_rope, jnp.sin(ang), 0.0)
    cos_q = jnp.where(lane < MLA_NOPE, 1.0, cos_r)

    cqn = (_rms(p[:, _C_CQ:_C_CKV], NORM_EPS) * qg_ref[...]).astype(BF16)
    qa = _dot(cqn, wqa_ref[...])
    qb = _dot(cqn, wqb_ref[...])
    q_scale = LOG2E / math.sqrt(MLA_NOPE + MLA_ROPE)
    cos_t = jnp.concatenate([cos_q] * MLA_HEADS, axis=1)
    sin_t = jnp.concatenate([sin_r] * MLA_HEADS, axis=1)
    q_ref[...] = ((qa * cos_t + qb * sin_t) * q_scale).astype(BF16)

    ckvn = (_rms(p[:, _C_CKV:_C_KRA], NORM_EPS) * kvg_ref[...]).astype(BF16)
    kn = _dot(ckvn, wkn_ref[...])
    kr = p[:, _C_KRA:_C_KRB] * cos_r + p[:, _C_KRB:_C_DQ] * sin_r
    k_ref[...] = (kn + jnp.concatenate([kr] * MLA_HEADS, axis=1)).astype(BF16)
    lane_t = lax.broadcasted_iota(I32, kn.shape, 1)
    ones_col = jnp.where(lane_t % HEAD_PAD == MLA_V, 1.0, 0.0)
    v_ref[...] = (_dot(ckvn, wv_ref[...]) + ones_col).astype(BF16)

    dq_ref[...] = (p[:, _C_DQ:_C_DK] * (LOG2E / math.sqrt(DIFF_HD))).astype(BF16)
    dk_ref[...] = p[:, _C_DK:_C_DV].astype(BF16)
    dv_ref[...] = p[:, _C_DV:_C_END].astype(BF16)


def _pre_attn(x2, mod3, pos_col, attn_pre_g, w_in, q_norm_g, kv_norm_g, w_uq, w_ukv, S):
    T, D = x2.shape
    tm = min(TM_PRE, S)
    tpb = S // tm
    f = lambda a: a.astype(BF16)
    d3 = DIFF_HEADS * DIFF_V
    a = MLA_Q_RANK
    b = a + MLA_KV_RANK
    c = b + MLA_ROPE

    def swap(r):
        hlf = MLA_ROPE // 2
        return jnp.concatenate([-r[..., hlf:], r[..., :hlf]], axis=-1)

    def pad_rope(r):
        return jnp.pad(r, ((0, 0), (MLA_NOPE, LANE - MLA_NOPE - MLA_ROPE)))

    w_kr = w_in[:, b:c]
    w1 = jnp.concatenate([w_in[:, :b], pad_rope(w_kr), pad_rope(swap(w_kr)), w_in[:, c:]], axis=1)
    assert w1.shape[1] == _C_END
    padq = HEAD_PAD - MLA_NOPE - MLA_ROPE
    wqa = jnp.pad(w_uq, ((0, 0), (0, 0), (0, padq))).reshape(MLA_Q_RANK, MLA_HEADS * HEAD_PAD)
    q_rope = w_uq[..., MLA_NOPE:]
    wqb = jnp.pad(swap(q_rope), ((0, 0), (0, 0), (MLA_NOPE, padq))).reshape(MLA_Q_RANK, MLA_HEADS * HEAD_PAD)
    wkn = jnp.pad(w_ukv[..., :MLA_NOPE], ((0, 0), (0, 0), (0, HEAD_PAD - MLA_NOPE))).reshape(
        MLA_KV_RANK, MLA_HEADS * HEAD_PAD)
    wv = jnp.pad(w_ukv[..., MLA_NOPE:], ((0, 0), (0, 0), (0, HEAD_PAD - MLA_V))).reshape(
        MLA_KV_RANK, MLA_HEADS * HEAD_PAD)

    inv = 1.0 / (ROPE_BASE ** (jnp.arange(0, MLA_ROPE, 2, dtype=F32) / MLA_ROPE))
    inv_lane = jnp.pad(jnp.concatenate([inv, inv]), (MLA_NOPE, LANE - MLA_NOPE - MLA_ROPE)).reshape(1, LANE)

    full = lambda arr: pl.BlockSpec(arr.shape, lambda i: (0,) * arr.ndim)
    row = lambda w: pl.BlockSpec((tm, w), lambda i: (i, 0))
    ins = [x2, mod3, pos_col, inv_lane, attn_pre_g.reshape(1, D), f(w1), q_norm_g.reshape(1, -1),
           kv_norm_g.reshape(1, -1), f(wqa), f(wqb), f(wkn), f(wv)]
    in_specs = [row(D), pl.BlockSpec((None, 6, D), lambda i: (i // tpb, 0, 0)), row(1)] + \
               [full(arr) for arr in ins[3:]]
    widths = [MLA_HEADS * HEAD_PAD, MLA_HEADS * HEAD_PAD, MLA_HEADS * HEAD_PAD, d3, d3, d3]
    return pl.pallas_call(
        _pre_attn_kernel,
        out_shape=[jax.ShapeDtypeStruct((T, w), BF16) for w in widths],
        grid=(T // tm,),
        in_specs=in_specs,
        out_specs=[row(w) for w in widths],
        compiler_params=_cparams("parallel"),
        name="pre_attn",
    )(*ins)


def _online_update(s, m, l, acc, v):
    m_new = jnp.maximum(m, jnp.max(s, axis=-1, keepdims=True))
    p = jnp.exp2(s - m_new)
    alpha = jnp.exp2(m - m_new)
    l_new = alpha * l + jnp.sum(p, axis=-1, keepdims=True)
    acc_new = alpha * acc + _dot(p.astype(BF16), v)
    return m_new, l_new, acc_new


MLA_HPS = 4


def _causal_blocks(i, tq, tk):
    nfull = (i * tq) // tk
    diag = [(nfull + j, i * tq - (nfull + j) * tk) for j in range(max(1, tq // tk))]
    return nfull, diag


def _keep(tq, tk, off):
    r = lax.broadcasted_iota(I32, (tq, tk), 0)
    c = lax.broadcasted_iota(I32, (tq, tk), 1)
    return r + off >= c


def _mla_attn_kernel(tk, q_ref, k_ref, v_ref, o_ref):
    i = pl.program_id(2)
    tq = q_ref.shape[0]
    sl = [slice(j * HEAD_PAD, (j + 1) * HEAD_PAD) for j in range(MLA_HPS)]
    qs = [q_ref[:, sl[j]] for j in range(MLA_HPS)]

    def step(kb, carry, off=None):
        r0 = pl.multiple_of(kb * tk, tk)
        out = []
        for j in range(MLA_HPS):
            m, acc = carry[j]
            s = _dot_nt(qs[j], k_ref[pl.ds(r0, tk), sl[j]])
            if off is not None:
                s = jnp.where(_keep(tq, tk, off), s, -jnp.inf)
            m_new = jnp.maximum(m, jnp.max(s, axis=-1, keepdims=True))
            p = jnp.exp2(s - m_new)
            acc = jnp.exp2(m - m_new) * acc + _dot(p.astype(BF16), v_ref[pl.ds(r0, tk), sl[j]])
            out.append((m_new, acc))
        return tuple(out)

    nfull, diag = _causal_blocks(i, tq, tk)
    init = tuple((jnp.full((tq, 1), -jnp.inf, F32), jnp.zeros((tq, HEAD_PAD), F32)) for _ in range(MLA_HPS))
    carry = lax.fori_loop(0, nfull, step, init)
    for kb, off in diag:
        carry = step(kb, carry, off)
    outs = [acc[:, :MLA_V] / acc[:, MLA_V:MLA_V + 1] for _, acc in carry]
    o_ref[...] = jnp.concatenate(outs, axis=1).astype(o_ref.dtype)


def _mla_attn(q, k, v, B, S):
    tq = min(TQ, S)
    tk = min(TK, S)
    q3 = q.reshape(B, S, -1)
    k3 = k.reshape(B, S, -1)
    v3 = v.reshape(B, S, -1)
    w = MLA_HPS * HEAD_PAD
    return pl.pallas_call(
        functools.partial(_mla_attn_kernel, tk),
        out_shape=jax.ShapeDtypeStruct((B, S, MLA_HEADS * MLA_V), BF16),
        grid=(B, MLA_HEADS // MLA_HPS, S // tq),
        in_specs=[pl.BlockSpec((None, tq, w), lambda b, h, i: (b, i, h)),
                  pl.BlockSpec((None, S, w), lambda b, h, i: (b, 0, h)),
                  pl.BlockSpec((None, S, w), lambda b, h, i: (b, 0, h))],
        out_specs=pl.BlockSpec((None, tq, MLA_HPS * MLA_V), lambda b, h, i: (b, i, h)),
        compiler_params=_cparams("parallel", "parallel", "arbitrary"),
        name="mla_attn",
    )(q3, k3, v3)


DIFF_HPS = 2


def _diff_attn_kernel(lambda_init, tk, q_ref, k_ref, v_ref, pc_ref, pr_ref, lam_ref, g_ref, o_ref):
    i = pl.program_id(2)
    tq = q_ref.shape[0]
    sl = [slice(j * DIFF_V, (j + 1) * DIFF_V) for j in range(DIFF_HPS)]
    lane = lax.broadcasted_iota(I32, (tq, DIFF_V), 1)
    qs, nslopes = [], []
    for j in range(DIFF_HPS):
        q = q_ref[:, sl[j]]
        zero = jnp.zeros_like(q)
        qs.append((jnp.where(lane < DIFF_HD, q, zero), jnp.where(lane >= DIFF_HD, q, zero)))
        hv = jnp.full((1, 1), pl.program_id(1) * DIFF_HPS + j, I32).astype(F32)
        nslopes.append(-LOG2E * jnp.exp2(-8.0 * (hv + 1.0) / DIFF_HEADS))
    pq = pc_ref[...]

    def step(kb, carry, off=None):
        r0 = pl.multiple_of(kb * tk, tk)
        dist = jnp.abs(pq - pr_ref[:, pl.ds(r0, tk)])
        keep = None if off is None else _keep(tq, tk, off)
        out = []
        for j in range(DIFF_HPS):
            kblk = k_ref[pl.ds(r0, tk), sl[j]]
            vblk = v_ref[pl.ds(r0, tk), sl[j]]
            bias = nslopes[j] * dist
            for c in range(2):
                s = _dot_nt(qs[j][c], kblk) + bias
                if keep is not None:
                    s = jnp.where(keep, s, -jnp.inf)
                out.append(_online_update(s, *carry[2 * j + c], vblk))
        return tuple(out)

    nfull, diag = _causal_blocks(i, tq, tk)
    init1 = (jnp.full((tq, 1), -jnp.inf, F32), jnp.zeros((tq, 1), F32), jnp.zeros((tq, DIFF_V), F32))
    carry = lax.fori_loop(0, nfull, step, (init1,) * (2 * DIFF_HPS))
    for kb, off in diag:
        carry = step(kb, carry, off)

    lv = lam_ref[...]
    lam = (jnp.exp(jnp.sum(lv[0:1] * lv[1:2], axis=-1, keepdims=True))
           - jnp.exp(jnp.sum(lv[2:3] * lv[3:4], axis=-1, keepdims=True)) + lambda_init)
    outs = []
    for j in range(DIFF_HPS):
        (_, l1, a1), (_, l2, a2) = carry[2 * j], carry[2 * j + 1]
        o = a1 / l1 - lam * (a2 / l2)
        outs.append(_rms(o, SUBLN_EPS) * g_ref[...] * (1.0 - lambda_init))
    o_ref[...] = jnp.concatenate(outs, axis=1).astype(o_ref.dtype)


def _diff_attn(dq, dk, dv, pos_col, pos_row, lam4, subln_g, lambda_init, B, S):
    tq = min(TQ, S)
    tk = min(TK, S)
    nq = S // tq
    q3 = dq.reshape(B, S, -1)
    k3 = dk.reshape(B, S, -1)
    v3 = dv.reshape(B, S, -1)
    w = DIFF_HPS * DIFF_V
    return pl.pallas_call(
        functools.partial(_diff_attn_kernel, lambda_init, tk),
        out_shape=jax.ShapeDtypeStruct((B, S, DIFF_HEADS * DIFF_V), BF16),
        grid=(B, DIFF_HEADS // DIFF_HPS, nq),
        in_specs=[pl.BlockSpec((None, tq, w), lambda b, h, i: (b, i, h)),
                  pl.BlockSpec((None, S, w), lambda b, h, i: (b, 0, h)),
                  pl.BlockSpec((None, S, w), lambda b, h, i: (b, 0, h)),
                  pl.BlockSpec((tq, 1), lambda b, h, i: (b * nq + i, 0)),
                  pl.BlockSpec((None, 1, S), lambda b, h, i: (b, 0, 0)),
                  pl.BlockSpec((4, DIFF_HD), lambda b, h, i: (0, 0)),
                  pl.BlockSpec((1, DIFF_V), lambda b, h, i: (0, 0))],
        out_specs=pl.BlockSpec((None, tq, w), lambda b, h, i: (b, i, h)),
        compiler_params=_cparams("parallel", "parallel", "arbitrary"),
        name="diff_attn",
    )(q3, k3, v3, pos_col, pos_row, lam4, subln_g.reshape(1, DIFF_V))


def _first_argmax(v, io, n, axis):
    m = jnp.max(v, axis=axis, keepdims=True)
    ix = jnp.min(jnp.where(v == m, io, n), axis=axis, keepdims=True)
    return m, ix


def _post_attn_kernel(x_ref, mla_ref, dif_ref, mod_ref, woa_ref, wob_ref, pg_ref, fg_ref,
                      rwt_ref, rb_ref,
                      x1_ref, h2_ref, idx_ref, wts_ref, rank_ref, cnt_ref, run_ref):
    step = pl.program_id(0)
    tm = x_ref.shape[0]
    E, G, GS = N_EXPERTS, N_GROUPS, GROUP_SIZE

    @pl.when(step == 0)
    def _():
        run_ref[...] = jnp.zeros_like(run_ref)

    g_a = mod_ref[2:3, :]
    sh_f = mod_ref[3:4, :]
    sc_f = mod_ref[4:5, :]
    y = _dot(mla_ref[...], woa_ref[...]) + _dot(dif_ref[...], wob_ref[...])
    x1 = x_ref[...] + g_a * (_rms(y, NORM_EPS) * pg_ref[...])
    x1_ref[...] = x1
    h2 = (_rms(x1, NORM_EPS) * fg_ref[...]) * (1.0 + sc_f) + sh_f
    _rt_store(h2_ref, _pack_rows(h2))

    logits = lax.dot_general(rwt_ref[...], h2, (((1,), (1,)), ((), ())),
                             preferred_element_type=F32, precision=lax.Precision.HIGHEST)
    scores = _sigmoid(logits)
    sel = scores + rb_ref[...]

    sio = lax.broadcasted_iota(I32, (GS, tm), 0)
    gs_rows = []
    for g in range(G):
        blk = sel[g * GS:(g + 1) * GS, :]
        m1, i1 = _first_argmax(blk, sio, GS, 0)
        m2 = jnp.max(jnp.where(sio == i1, -jnp.inf, blk), axis=0, keepdims=True)
        gs_rows.append(m1 + m2)
    gs = jnp.concatenate(gs_rows, axis=0)

    gio = lax.broadcasted_iota(I32, (G, tm), 0)
    gkeep = jnp.zeros((G, tm), F32)
    for _ in range(TOPK_GROUPS):
        _, ix = _first_argmax(gs, gio, G, 0)
        pick = gio == ix
        gkeep = jnp.where(pick, 1.0, gkeep)
        gs = jnp.where(pick, -jnp.inf, gs)
    ekeep = jnp.concatenate([jnp.broadcast_to(gkeep[g:g + 1, :], (GS, tm)) for g in range(G)], axis=0)
    cand = jnp.where(ekeep > 0.0, sel, -jnp.inf)

    eio = lax.broadcasted_iota(I32, (E, tm), 0)
    idx_rows, w_rows = [], []
    for _ in range(TOP_K):
        _, ix = _first_argmax(cand, eio, E, 0)
        pick = eio == ix
        w_rows.append(jnp.sum(jnp.where(pick, scores, 0.0), axis=0, keepdims=True))
        cand = jnp.where(pick, -jnp.inf, cand)
        idx_rows.append(ix)
    idx = jnp.concatenate(idx_rows, axis=0)
    w = jnp.concatenate(w_rows, axis=0)
    wts_ref[...] = w / jnp.sum(w, axis=0, keepdims=True) * ROUTED_SCALE
    idx_ref[...] = idx

    onehot = jnp.zeros((E, tm), F32)
    for k in range(TOP_K):
        onehot = onehot + jnp.where(eio == idx_rows[k], 1.0, 0.0)
    tr = lax.broadcasted_iota(I32, (tm, tm), 0)
    tc = lax.broadcasted_iota(I32, (tm, tm), 1)
    before = jnp.where(tr < tc, 1.0, 0.0).astype(BF16)
    prior = _dot(onehot.astype(BF16), before) + run_ref[...]
    rank_rows = [jnp.sum(jnp.where(eio == idx_rows[k], prior, 0.0), axis=0, keepdims=True)
                 for k in range(TOP_K)]
    rank_ref[...] = jnp.concatenate(rank_rows, axis=0).astype(I32)
    run_ref[...] += jnp.sum(onehot, axis=1, keepdims=True)
    cnt_ref[...] = run_ref[...].astype(I32)


def _post_attn(x2, mla, dif, mod3, w_o, attn_post_g, ffn_pre_g, router_w, router_b, S):
    T, D = x2.shape
    tm = min(TM_POST, S)
    tpb = S // tm
    half = MLA_HEADS * MLA_V
    woa = w_o[:half].astype(BF16)
    wob = w_o[half:].astype(BF16)
    rwt = router_w.T
    full = lambda arr: pl.BlockSpec(arr.shape, lambda i: (0,) * arr.ndim)
    row = lambda w: pl.BlockSpec((tm, w), lambda i: (i, 0))
    col = lambda r: pl.BlockSpec((r, tm), lambda i: (0, i))
    ins = [x2, mla.reshape(T, -1), dif.reshape(T, -1), mod3, woa, wob, attn_post_g.reshape(1, D),
           ffn_pre_g.reshape(1, D), rwt, router_b.reshape(N_EXPERTS, 1)]
    in_specs = [row(D), row(half), row(D - half), pl.BlockSpec((None, 6, D), lambda i: (i // tpb, 0, 0))] + \
               [full(arr) for arr in ins[4:]]
    return pl.pallas_call(
        _post_attn_kernel,
        out_shape=[jax.ShapeDtypeStruct((T, D), F32), jax.ShapeDtypeStruct((T * RT, LANE), I32),
                   jax.ShapeDtypeStruct((TOP_K, T), I32), jax.ShapeDtypeStruct((TOP_K, T), F32),
                   jax.ShapeDtypeStruct((TOP_K, T), I32), jax.ShapeDtypeStruct((N_EXPERTS, 1), I32)],
        grid=(T // tm,),
        in_specs=in_specs,
        out_specs=[row(D), pl.BlockSpec((tm * RT, LANE), lambda i: (i, 0)), col(TOP_K), col(TOP_K), col(TOP_K),
                   pl.BlockSpec((N_EXPERTS, 1), lambda i: (0, 0))],
        scratch_shapes=[pltpu.VMEM((N_EXPERTS, 1), F32)],
        compiler_params=_cparams("arbitrary"),
        name="post_attn",
    )(*ins)


def _dest_kernel(idx_ref, rank_ref, ps_ref, o_ref):
    idx = idx_ref[...]
    tm = idx.shape[1]
    eio = lax.broadcasted_iota(I32, (N_EXPERTS, tm), 0)
    ps = ps_ref[...]
    rows = [jnp.sum(jnp.where(eio == idx[k:k + 1, :], ps, 0.0), axis=0, keepdims=True)
            for k in range(TOP_K)]
    o_ref[...] = jnp.concatenate(rows, axis=0).astype(I32) + rank_ref[...]


def _dest(idx, rank, pstart):
    K, T = idx.shape
    tm = min(2048, T)
    col = pl.BlockSpec((K, tm), lambda i: (0, i))
    return pl.pallas_call(
        _dest_kernel,
        out_shape=jax.ShapeDtypeStruct((K, T), I32),
        grid=(T // tm,),
        in_specs=[col, col, pl.BlockSpec((N_EXPERTS, 1), lambda i: (0, 0))],
        out_specs=col,
        compiler_params=_cparams("parallel"),
        name="dest",
    )(idx, rank, pstart.astype(F32).reshape(N_EXPERTS, 1))


def _dispatch_kernel(bs_ref, cnt_ref, dest_ref, h_ref, xs_ref, zeros, sem, zsem):
    tm = h_ref.shape[0] // RT
    blk = BM * RT
    nb = xs_ref.shape[0] // blk

    @pl.when(pl.program_id(0) == 0)
    def _():
        zeros[...] = jnp.zeros_like(zeros)

        def zcopy(g):
            return pltpu.make_async_copy(zeros, xs_ref.at[pl.ds(pl.multiple_of(g * blk, blk), blk), :], zsem)

        def has_pad(e):
            return cnt_ref[e] % BM != 0

        def fill(e, _):
            @pl.when(has_pad(e))
            def _():
                zcopy(bs_ref[e + 1] - 1).start()
            return 0

        def drain(e, _):
            @pl.when(has_pad(e))
            def _():
                zcopy(0).wait()
            return 0

        lax.fori_loop(0, N_EXPERTS, fill, 0)
        lax.fori_loop(bs_ref[N_EXPERTS], nb, lambda g, _: (zcopy(g).start(), 0)[1], 0)
        lax.fori_loop(0, N_EXPERTS, drain, 0)
        lax.fori_loop(bs_ref[N_EXPERTS], nb, lambda g, _: (zcopy(0).wait(), 0)[1], 0)

    def tile(ref, r):
        return ref.at[pl.ds(pl.multiple_of(r * RT, RT), RT), :]

    def issue(t, _):
        for k in range(TOP_K):
            pltpu.make_async_copy(tile(h_ref, t), tile(xs_ref, dest_ref[k, t]), sem).start(priority=k % 2)
        return 0

    lax.fori_loop(0, tm, issue, 0)
    for _ in range(TOP_K):
        pltpu.make_async_copy(h_ref, xs_ref.at[pl.ds(0, tm * RT), :], sem).wait()


def _dispatch(h2rt, dest, bstart, cnt, P):
    T = h2rt.shape[0] // RT
    tm = min(TM_DISP, T)
    nt = T // tm
    dest3 = dest.reshape(TOP_K, nt, tm).transpose(1, 0, 2)
    return pl.pallas_call(
        _dispatch_kernel,
        out_shape=jax.ShapeDtypeStruct((P * RT, LANE), I32),
        grid_spec=pltpu.PrefetchScalarGridSpec(
            num_scalar_prefetch=2,
            grid=(nt,),
            in_specs=[pl.BlockSpec((None, TOP_K, tm), lambda i, bs, cn: (i, 0, 0), memory_space=pltpu.SMEM),
                      pl.BlockSpec((tm * RT, LANE), lambda i, bs, cn: (i, 0))],
            out_specs=pl.BlockSpec(memory_space=pl.ANY),
            scratch_shapes=[pltpu.VMEM((BM * RT, LANE), I32), pltpu.SemaphoreType.DMA(()),
                            pltpu.SemaphoreType.DMA(())],
        ),
        compiler_params=_cparams("arbitrary"),
        name="dispatch",
    )(bstart, cnt, dest3, h2rt)


NBUF_X = 10
X_AHEAD = NBUF_X - 2
NBUF_Y = 6


def _experts_kernel(bs_ref, wg_ref, wu_ref, wd_ref, xs_ref, ys_ref, wgb, wub, wdb, xbuf, ybuf, xsem, ysem):
    e = pl.program_id(0)
    blk = BM * RT
    nb = xs_ref.shape[0] // blk
    nused = bs_ref[N_EXPERTS]
    g0 = bs_ref[e]
    g1 = bs_ref[e + 1]

    def rows(g):
        return pl.ds(pl.multiple_of(g * blk, blk), blk)

    def x_copy(g, slot):
        return pltpu.make_async_copy(xs_ref.at[rows(g), :], xbuf.at[slot], xsem.at[slot])

    def y_copy(g, slot):
        return pltpu.make_async_copy(ybuf.at[slot], ys_ref.at[rows(g), :], ysem.at[slot])

    @pl.when(e == 0)
    def _():
        for j in range(X_AHEAD):
            @pl.when(j < nused)
            def _():
                x_copy(j, j).start()

    def step(gs):
        for g in gs:
            x_copy(g, g % NBUF_X).wait()
            nxt = g + X_AHEAD

            @pl.when(nxt < nused)
            def _():
                x_copy(nxt, nxt % NBUF_X).start()

        x = _unpack_rows(jnp.concatenate([_rt_load(xbuf, BM, g % NBUF_X) for g in gs], axis=0), BF16)
        gate = _dot(x, wgb[...])
        up = _dot(x, wub[...])
        a = (gate * _sigmoid(gate)) * up
        y = _dot(a.astype(BF16), wdb[...])
        for j, g in enumerate(gs):
            ys = g % NBUF_Y

            @pl.when(g >= NBUF_Y)
            def _():
                y_copy(g - NBUF_Y, ys).wait()

            _rt_store(ybuf, _pack_rows(y[j * BM:(j + 1) * BM]), ys)
            y_copy(g, ys).start(priority=1)

    @pl.when(g1 > g0)
    def _():
        wgb[...] = wg_ref[...].astype(BF16)
        wub[...] = wu_ref[...].astype(BF16)
        wdb[...] = wd_ref[...].astype(BF16)

        def pair(i, _):
            g = g0 + 2 * i
            step([g, g + 1])
            return 0

        lax.fori_loop(0, (g1 - g0) // 2, pair, 0)

        @pl.when((g1 - g0) % 2 == 1)
        def _():
            step([g1 - 1])

    @pl.when(e == N_EXPERTS - 1)
    def _():
        for j in range(NBUF_Y):
            @pl.when(nused - 1 - j >= 0)
            def _():
                y_copy(0, (nused - 1 - j) % NBUF_Y).wait()
        ybuf[0] = jnp.zeros(ybuf.shape[1:], I32)
        lax.fori_loop(nused, nb, lambda g, _: (y_copy(g, 0).start(), 0)[1], 0)
        lax.fori_loop(nused, nb, lambda g, _: (y_copy(0, 0).wait(), 0)[1], 0)


def _experts(xs, bstart, w_gate, w_up, w_down):
    E, D, F = w_gate.shape
    return pl.pallas_call(
        _experts_kernel,
        out_shape=jax.ShapeDtypeStruct(xs.shape, I32),
        grid_spec=pltpu.PrefetchScalarGridSpec(
            num_scalar_prefetch=1,
            grid=(E,),
            in_specs=[pl.BlockSpec((None, D, F), lambda e, bs: (e, 0, 0)),
                      pl.BlockSpec((None, D, F), lambda e, bs: (e, 0, 0)),
                      pl.BlockSpec((None, F, D), lambda e, bs: (e, 0, 0)),
                      pl.BlockSpec(memory_space=pl.ANY)],
            out_specs=pl.BlockSpec(memory_space=pl.ANY),
            scratch_shapes=[pltpu.VMEM((D, F), BF16), pltpu.VMEM((D, F), BF16), pltpu.VMEM((F, D), BF16),
                            pltpu.VMEM((NBUF_X, BM * RT, LANE), I32), pltpu.VMEM((NBUF_Y, BM * RT, LANE), I32),
                            pltpu.SemaphoreType.DMA((NBUF_X,)), pltpu.SemaphoreType.DMA((NBUF_Y,))],
        ),
        compiler_params=_cparams("arbitrary"),
        name="experts",
    )(bstart, w_gate, w_up, w_down, xs)


def _combine_kernel(dcur_ref, dnxt_ref, w_ref, x1_ref, h_ref, mod_ref, pg_ref, sg_ref, su_ref, sd_ref,
                    ys_ref, o_ref, rows_a, rows_b, sem):
    i = pl.program_id(0)
    n = pl.num_programs(0)
    tm = x1_ref.shape[0]

    def tile(r):
        return pl.ds(pl.multiple_of(r * RT, RT), RT)

    def request(dref, buf, s, t):
        for k in range(TOP_K):
            pltpu.make_async_copy(ys_ref.at[tile(dref[k, t]), :], buf.at[k, tile(t), :],
                                  sem.at[s]).start(priority=k % 2)

    def drain(buf, s):
        for k in range(TOP_K):
            pltpu.make_async_copy(ys_ref.at[pl.ds(0, tm * RT), :], buf.at[k], sem.at[s]).wait()

    @pl.when(i == 0)
    def _():
        lax.fori_loop(0, tm, lambda t, _: (request(dcur_ref, rows_a, 0, t), 0)[1], 0)

    def step(cur, s_cur, nxt, s_nxt):
        drain(cur, s_cur)
        for t in range(tm):
            request(dnxt_ref, nxt, s_nxt, t)
        w = w_ref[...]
        routed = w[:, 0:1] * _unpack_rows(_rt_load(cur, tm, 0), F32)
        for k in range(1, TOP_K):
            routed = routed + w[:, k:k + 1] * _unpack_rows(_rt_load(cur, tm, k), F32)
        hb = _unpack_rows(_rt_load(h_ref, tm), BF16)
        g = _dot(hb, sg_ref[...])
        u = _dot(hb, su_ref[...])
        shared = _dot(((g * _sigmoid(g)) * u).astype(BF16), sd_ref[...])
        y = routed + shared
        g_f = mod_ref[5:6, :]
        o_ref[...] = x1_ref[...] + g_f * (_rms(y, NORM_EPS) * pg_ref[...])

    even = i % 2 == 0

    @pl.when(even)
    def _():
        step(rows_a, 0, rows_b, 1)

    @pl.when(jnp.logical_not(even))
    def _():
        step(rows_b, 1, rows_a, 0)

    @pl.when(i == n - 1)
    def _():
        @pl.when(even)
        def _():
            drain(rows_b, 1)

        @pl.when(jnp.logical_not(even))
        def _():
            drain(rows_a, 0)


def _combine(dest, wts, x1, h2, mod3, ffn_post_g, sw_gate, sw_up, sw_down, ys, S):
    T, D = x1.shape
    tm = min(TM_COMB, S)
    nt = T // tm
    tpb = S // tm
    dest3 = dest.reshape(TOP_K, nt, tm).transpose(1, 0, 2)
    w_tk = wts.T
    full = lambda arr: pl.BlockSpec(arr.shape, lambda i: (0,) * arr.ndim)
    row = lambda w: pl.BlockSpec((tm, w), lambda i: (i, 0))
    sg, su, sd = sw_gate.astype(BF16), sw_up.astype(BF16), sw_down.astype(BF16)
    pg = ffn_post_g.reshape(1, D)
    return pl.pallas_call(
        _combine_kernel,
        out_shape=jax.ShapeDtypeStruct((T, D), F32),
        grid=(nt,),
        in_specs=[pl.BlockSpec((None, TOP_K, tm), lambda i: (i, 0, 0), memory_space=pltpu.SMEM),
                  pl.BlockSpec((None, TOP_K, tm), lambda i: (jnp.minimum(i + 1, nt - 1), 0, 0),
                               memory_space=pltpu.SMEM),
                  row(TOP_K), row(D), pl.BlockSpec((tm * RT, LANE), lambda i: (i, 0)),
                  pl.BlockSpec((None, 6, D), lambda i: (i // tpb, 0, 0)),
                  full(pg), full(sg), full(su), full(sd),
                  pl.BlockSpec(memory_space=pl.ANY)],
        out_specs=row(D),
        scratch_shapes=[pltpu.VMEM((TOP_K, tm * RT, LANE), I32), pltpu.VMEM((TOP_K, tm * RT, LANE), I32),
                        pltpu.SemaphoreType.DMA((2,))],
        compiler_params=_cparams("arbitrary"),
        name="combine",
    )(dest3, dest3, w_tk, x1, h2, mod3, pg, sg, su, sd, ys)


def _moe(h2, x1, idx, wts, rank, counts, mod3, ffn_post_g, exp_w_gate, exp_w_up, exp_w_down,
         sw_gate, sw_up, sw_down, S):
    T, D = x1.shape
    A = T * TOP_K
    P = A + N_EXPERTS * BM
    cnt = counts.reshape(N_EXPERTS)
    blocks = (cnt + BM - 1) // BM
    bstart = jnp.concatenate([jnp.zeros((1,), I32), jnp.cumsum(blocks).astype(I32)])
    pstart = bstart[:-1] * BM

    dest = _dest(idx, rank, pstart)
    xs = _dispatch(h2, dest, bstart, cnt, P)
    ys = _experts(xs, bstart, exp_w_gate, exp_w_up, exp_w_down)
    return _combine(dest, wts, x1, h2, mod3, ffn_post_g, sw_gate, sw_up, sw_down, ys, S)


def _layer(x, c, positions, lambda_init, w_ada, b_ada, attn_pre_g, attn_post_g, w_in, q_norm_g, kv_norm_g,
           w_uq, w_ukv, lam_q1, lam_k1, lam_q2, lam_k2, diff_subln_g, w_o, ffn_pre_g, ffn_post_g,
           router_w, router_b, exp_w_gate, exp_w_up, exp_w_down, sw_gate, sw_up, sw_down):
    B, S, D = x.shape
    T = B * S
    x2 = x.reshape(T, D)
    posf = positions.astype(F32)
    pos_col = posf.reshape(T, 1)
    pos_row = posf.reshape(B, 1, S)

    mod3 = _ada(c, w_ada, b_ada).reshape(B, 6, D)
    q, k, v, dq, dk, dv = _pre_attn(x2, mod3, pos_col, attn_pre_g, w_in, q_norm_g, kv_norm_g, w_uq, w_ukv, S)
    mla = _mla_attn(q, k, v, B, S)
    lam4 = jnp.stack([lam_q1, lam_k1, lam_q2, lam_k2])
    dif = _diff_attn(dq, dk, dv, pos_col, pos_row, lam4, diff_subln_g, lambda_init, B, S)
    x1, h2, idx, wts, rank, counts = _post_attn(x2, mla, dif, mod3, w_o, attn_post_g, ffn_pre_g,
                                                router_w, router_b, S)
    out = _moe(h2, x1, idx, wts, rank, counts, mod3, ffn_post_g, exp_w_gate, exp_w_up, exp_w_down,
               sw_gate, sw_up, sw_down, S)
    return out.reshape(B, S, D)


def kernel(x, c, positions, w_ada, b_ada, attn_pre_g, attn_post_g, w_in, q_norm_g, kv_norm_g, w_uq, w_ukv,
           lam_q1, lam_k1, lam_q2, lam_k2, diff_subln_g, w_o, ffn_pre_g, ffn_post_g, router_w, router_b,
           exp_w_gate, exp_w_up, exp_w_down, shared_w_gate, shared_w_up, shared_w_down):
    depth = w_ada.shape[0]
    for l in range(depth):
        lambda_init = 0.8 - 0.6 * math.exp(-0.3 * l)
        x = _layer(x, c, positions, lambda_init, w_ada[l], b_ada[l], attn_pre_g[l], attn_post_g[l], w_in[l],
                   q_norm_g[l], kv_norm_g[l], w_uq[l], w_ukv[l], lam_q1[l], lam_k1[l], lam_q2[l], lam_k2[l],
                   diff_subln_g[l], w_o[l], ffn_pre_g[l], ffn_post_g[l], router_w[l], router_b[l],
                   exp_w_gate[l], exp_w_up[l], exp_w_down[l], shared_w_gate[l], shared_w_up[l],
                   shared_w_down[l])
    return x
```

```python
import functools
import math

import jax
import jax.numpy as jnp
from jax import lax
from jax.experimental import pallas as pl
from jax.experimental.pallas import tpu as pltpu

F32 = jnp.float32
BF16 = jnp.bfloat16
I32 = jnp.int32

MLA_HEADS = 8
MLA_NOPE = 64
MLA_ROPE = 32
MLA_V = 64
MLA_Q_RANK = 256
MLA_KV_RANK = 128
ROPE_BASE = 10000.0
DIFF_HEADS = 4
DIFF_HD = 64
DIFF_V = 128
N_EXPERTS = 256
TOP_K = 8
N_GROUPS = 8
GROUP_SIZE = N_EXPERTS // N_GROUPS
TOPK_GROUPS = 4
ROUTED_SCALE = 2.5
NORM_EPS = 1e-6
SUBLN_EPS = 1e-5
LOG2E = 1.4426950408889634

LANE = 128
HEAD_PAD = 128

TM_PRE = 512
TQ = 512
TK = 512
TM_POST = 512
TM_DISP = 1024
TM_COMB = 128
BM = 128
VMEM_LIMIT = 48 * 1024 * 1024


def _cparams(*sem):
    return pltpu.CompilerParams(dimension_semantics=sem, vmem_limit_bytes=VMEM_LIMIT)


def _rms(x, eps):
    return x * lax.rsqrt(jnp.mean(x * x, axis=-1, keepdims=True) + eps)


def _sigmoid(x):
    return 1.0 / (1.0 + jnp.exp(-x))


def _dot(a, b):
    return jnp.dot(a, b, preferred_element_type=F32)


RT = 4
HI_MASK = -65536


def _pack_rows(x):
    half = x.shape[1] // 2
    lo = lax.bitcast_convert_type(x[:, :half].astype(BF16).astype(F32), I32)
    hi = lax.bitcast_convert_type(x[:, half:].astype(BF16).astype(F32), I32)
    return lax.shift_right_logical(lo, 16) | (hi & HI_MASK)


def _unpack_rows(u, dtype):
    lo = lax.bitcast_convert_type(lax.shift_left(u, 16), F32)
    hi = lax.bitcast_convert_type(u & HI_MASK, F32)
    return jnp.concatenate([lo.astype(dtype), hi.astype(dtype)], axis=1)


def _rt_load(ref, n, *lead):
    return jnp.concatenate([ref[(*lead, pl.ds(j, n, stride=RT), slice(None))] for j in range(RT)], axis=1)


def _rt_store(ref, val, *lead):
    n = val.shape[0]
    for j in range(RT):
        ref[(*lead, pl.ds(j, n, stride=RT), slice(None))] = val[:, j * LANE:(j + 1) * LANE]


def _dot_nt(a, b):
    return lax.dot_general(a, b, (((1,), (1,)), ((), ())), preferred_element_type=F32)


def _ada_kernel(c_ref, w_ref, b_ref, o_ref):
    c = c_ref[...]
    a = c * _sigmoid(c)
    o_ref[...] = jnp.dot(a, w_ref[...], preferred_element_type=F32,
                         precision=lax.Precision.HIGHEST) + b_ref[...]


def _ada(c, w_ada, b_ada):
    B, D = c.shape
    n = w_ada.shape[1]
    return pl.pallas_call(
        _ada_kernel,
        out_shape=jax.ShapeDtypeStruct((B, n), F32),
        grid=(n // D,),
        in_specs=[pl.BlockSpec((B, D), lambda j: (0, 0)),
                  pl.BlockSpec((D, D), lambda j: (0, j)),
                  pl.BlockSpec((1, D), lambda j: (0, j))],
        out_specs=pl.BlockSpec((B, D), lambda j: (0, j)),
        compiler_params=_cparams("arbitrary"),
        name="ada",
    )(c, w_ada, b_ada.reshape(1, n))


_C_CQ = 0
_C_CKV = _C_CQ + MLA_Q_RANK
_C_KRA = _C_CKV + MLA_KV_RANK
_C_KRB = _C_KRA + LANE
_C_DQ = _C_KRB + LANE
_C_DK = _C_DQ + DIFF_HEADS * DIFF_V
_C_DV = _C_DK + DIFF_HEADS * DIFF_V
_C_END = _C_DV + DIFF_HEADS * DIFF_V


def _pre_attn_kernel(x_ref, mod_ref, pos_ref, inv_ref, g_ref, w1_ref, qg_ref, kvg_ref,
                     wqa_ref, wqb_ref, wkn_ref, wv_ref,
                     q_ref, k_ref, v_ref, dq_ref, dk_ref, dv_ref):
    x = x_ref[...]
    sh = mod_ref[0:1, :]
    sc = mod_ref[1:2, :]
    h = _rms(x, NORM_EPS) * g_ref[...]
    h = h * (1.0 + sc) + sh
    p = _dot(h.astype(BF16), w1_ref[...])

    ang = pos_ref[...] * inv_ref[...]
    lane = lax.broadcasted_iota(I32, ang.shape, 1)
    in_rope = (lane >= MLA_NOPE) & (lane < MLA_NOPE + MLA_ROPE)
    cos_r = jnp.where(in_rope, jnp.cos(ang), 0.0)
    sin_r = jnp.where(in_rope, jnp.sin(ang), 0.0)
    cos_q = jnp.where(lane < MLA_NOPE, 1.0, cos_r)

    cqn = (_rms(p[:, _C_CQ:_C_CKV], NORM_EPS) * qg_ref[...]).astype(BF16)
    qa = _dot(cqn, wqa_ref[...])
    qb = _dot(cqn, wqb_ref[...])
    q_scale = LOG2E / math.sqrt(MLA_NOPE + MLA_ROPE)
    cos_t = jnp.concatenate([cos_q] * MLA_HEADS, axis=1)
    sin_t = jnp.concatenate([sin_r] * MLA_HEADS, axis=1)
    q_ref[...] = ((qa * cos_t + qb * sin_t) * q_scale).astype(BF16)

    ckvn = (_rms(p[:, _C_CKV:_C_KRA], NORM_EPS) * kvg_ref[...]).astype(BF16)
    kn = _dot(ckvn, wkn_ref[...])
    kr = p[:, _C_KRA:_C_KRB] * cos_r + p[:, _C_KRB:_C_DQ] * sin_r
    k_ref[...] = (kn + jnp.concatenate([kr] * MLA_HEADS, axis=1)).astype(BF16)
    lane_t = lax.broadcasted_iota(I32, kn.shape, 1)
    ones_col = jnp.where(lane_t % HEAD_PAD == MLA_V, 1.0, 0.0)
    v_ref[...] = (_dot(ckvn, wv_ref[...]) + ones_col).astype(BF16)

    dq_ref[...] = (p[:, _C_DQ:_C_DK] * (LOG2E / math.sqrt(DIFF_HD))).astype(BF16)
    dk_ref[...] = p[:, _C_DK:_C_DV].astype(BF16)
    dv_ref[...] = p[:, _C_DV:_C_END].astype(BF16)


def _pre_attn(x2, mod3, pos_col, attn_pre_g, w_in, q_norm_g, kv_norm_g, w_uq, w_ukv, S):
    T, D = x2.shape
    tm = min(TM_PRE, S)
    tpb = S // tm
    f = lambda a: a.astype(BF16)
    d3 = DIFF_HEADS * DIFF_V
    a = MLA_Q_RANK
    b = a + MLA_KV_RANK
    c = b + MLA_ROPE

    def swap(r):
        hlf = MLA_ROPE // 2
        return jnp.concatenate([-r[..., hlf:], r[..., :hlf]], axis=-1)

    def pad_rope(r):
        return jnp.pad(r, ((0, 0), (MLA_NOPE, LANE - MLA_NOPE - MLA_ROPE)))

    w_kr = w_in[:, b:c]
    w1 = jnp.concatenate([w_in[:, :b], pad_rope(w_kr), pad_rope(swap(w_kr)), w_in[:, c:]], axis=1)
    assert w1.shape[1] == _C_END
    padq = HEAD_PAD - MLA_NOPE - MLA_ROPE
    wqa = jnp.pad(w_uq, ((0, 0), (0, 0), (0, padq))).reshape(MLA_Q_RANK, MLA_HEADS * HEAD_PAD)
    q_rope = w_uq[..., MLA_NOPE:]
    wqb = jnp.pad(swap(q_rope), ((0, 0), (0, 0), (MLA_NOPE, padq))).reshape(MLA_Q_RANK, MLA_HEADS * HEAD_PAD)
    wkn = jnp.pad(w_ukv[..., :MLA_NOPE], ((0, 0), (0, 0), (0, HEAD_PAD - MLA_NOPE))).reshape(
        MLA_KV_RANK, MLA_HEADS * HEAD_PAD)
    wv = jnp.pad(w_ukv[..., MLA_NOPE:], ((0, 0), (0, 0), (0, HEAD_PAD - MLA_V))).reshape(
        MLA_KV_RANK, MLA_HEADS * HEAD_PAD)

    inv = 1.0 / (ROPE_BASE ** (jnp.arange(0, MLA_ROPE, 2, dtype=F32) / MLA_ROPE))
    inv_lane = jnp.pad(jnp.concatenate([inv, inv]), (MLA_NOPE, LANE - MLA_NOPE - MLA_ROPE)).reshape(1, LANE)

    full = lambda arr: pl.BlockSpec(arr.shape, lambda i: (0,) * arr.ndim)
    row = lambda w: pl.BlockSpec((tm, w), lambda i: (i, 0))
    ins = [x2, mod3, pos_col, inv_lane, attn_pre_g.reshape(1, D), f(w1), q_norm_g.reshape(1, -1),
           kv_norm_g.reshape(1, -1), f(wqa), f(wqb), f(wkn), f(wv)]
    in_specs = [row(D), pl.BlockSpec((None, 6, D), lambda i: (i // tpb, 0, 0)), row(1)] + \
               [full(arr) for arr in ins[3:]]
    widths = [MLA_HEADS * HEAD_PAD, MLA_HEADS * HEAD_PAD, MLA_HEADS * HEAD_PAD, d3, d3, d3]
    return pl.pallas_call(
        _pre_attn_kernel,
        out_shape=[jax.ShapeDtypeStruct((T, w), BF16) for w in widths],
        grid=(T // tm,),
        in_specs=in_specs,
        out_specs=[row(w) for w in widths],
        compiler_params=_cparams("parallel"),
        name="pre_attn",
    )(*ins)


def _online_update(s, m, l, acc, v):
    m_new = jnp.maximum(m, jnp.max(s, axis=-1, keepdims=True))
    p = jnp.exp2(s - m_new)
    alpha = jnp.exp2(m - m_new)
    l_new = alpha * l + jnp.sum(p, axis=-1, keepdims=True)
    acc_new = alpha * acc + _dot(p.astype(BF16), v)
    return m_new, l_new, acc_new


MLA_HPS = 4


def _causal_blocks(i, tq, tk):
    nfull = (i * tq) // tk
    diag = [(nfull + j, i * tq - (nfull + j) * tk) for j in range(max(1, tq // tk))]
    return nfull, diag


def _keep(tq, tk, off):
    r = lax.broadcasted_iota(I32, (tq, tk), 0)
    c = lax.broadcasted_iota(I32, (tq, tk), 1)
    return r + off >= c


def _mla_attn_kernel(tk, q_ref, k_ref, v_ref, o_ref):
    i = pl.program_id(2)
    tq = q_ref.shape[0]
    sl = [slice(j * HEAD_PAD, (j + 1) * HEAD_PAD) for j in range(MLA_HPS)]
    qs = [q_ref[:, sl[j]] for j in range(MLA_HPS)]

    def step(kb, carry, off=None):
        r0 = pl.multiple_of(kb * tk, tk)
        out = []
        for j in range(MLA_HPS):
            m, acc = carry[j]
            s = _dot_nt(qs[j], k_ref[pl.ds(r0, tk), sl[j]])
            if off is not None:
                s = jnp.where(_keep(tq, tk, off), s, -jnp.inf)
            m_new = jnp.maximum(m, jnp.max(s, axis=-1, keepdims=True))
            p = jnp.exp2(s - m_new)
            acc = jnp.exp2(m - m_new) * acc + _dot(p.astype(BF16), v_ref[pl.ds(r0, tk), sl[j]])
            out.append((m_new, acc))
        return tuple(out)

    nfull, diag = _causal_blocks(i, tq, tk)
    init = tuple((jnp.full((tq, 1), -jnp.inf, F32), jnp.zeros((tq, HEAD_PAD), F32)) for _ in range(MLA_HPS))
    carry = lax.fori_loop(0, nfull, step, init)
    for kb, off in diag:
        carry = step(kb, carry, off)
    outs = [acc[:, :MLA_V] / acc[:, MLA_V:MLA_V + 1] for _, acc in carry]
    o_ref[...] = jnp.concatenate(outs, axis=1).astype(o_ref.dtype)


def _mla_attn(q, k, v, B, S):
    tq = min(TQ, S)
    tk = min(TK, S)
    q3 = q.reshape(B, S, -1)
    k3 = k.reshape(B, S, -1)
    v3 = v.reshape(B, S, -1)
    w = MLA_HPS * HEAD_PAD
    return pl.pallas_call(
        functools.partial(_mla_attn_kernel, tk),
        out_shape=jax.ShapeDtypeStruct((B, S, MLA_HEADS * MLA_V), BF16),
        grid=(B, MLA_HEADS // MLA_HPS, S // tq),
        in_specs=[pl.BlockSpec((None, tq, w), lambda b, h, i: (b, i, h)),
                  pl.BlockSpec((None, S, w), lambda b, h, i: (b, 0, h)),
                  pl.BlockSpec((None, S, w), lambda b, h, i: (b, 0, h))],
        out_specs=pl.BlockSpec((None, tq, MLA_HPS * MLA_V), lambda b, h, i: (b, i, h)),
        compiler_params=_cparams("parallel", "parallel", "arbitrary"),
        name="mla_attn",
    )(q3, k3, v3)


DIFF_HPS = 2


def _diff_attn_kernel(lambda_init, tk, q_ref, k_ref, v_ref, pc_ref, pr_ref, lam_ref, g_ref, o_ref):
    i = pl.program_id(2)
    tq = q_ref.shape[0]
    sl = [slice(j * DIFF_V, (j + 1) * DIFF_V) for j in range(DIFF_HPS)]
    lane = lax.broadcasted_iota(I32, (tq, DIFF_V), 1)
    qs, nslopes = [], []
    for j in range(DIFF_HPS):
        q = q_ref[:, sl[j]]
        zero = jnp.zeros_like(q)
        qs.append((jnp.where(lane < DIFF_HD, q, zero), jnp.where(lane >= DIFF_HD, q, zero)))
        hv = jnp.full((1, 1), pl.program_id(1) * DIFF_HPS + j, I32).astype(F32)
        nslopes.append(-LOG2E * jnp.exp2(-8.0 * (hv + 1.0) / DIFF_HEADS))
    pq = pc_ref[...]

    def step(kb, carry, off=None):
        r0 = pl.multiple_of(kb * tk, tk)
        dist = jnp.abs(pq - pr_ref[:, pl.ds(r0, tk)])
        keep = None if off is None else _keep(tq, tk, off)
        out = []
        for j in range(DIFF_HPS):
            kblk = k_ref[pl.ds(r0, tk), sl[j]]
            vblk = v_ref[pl.ds(r0, tk), sl[j]]
            bias = nslopes[j] * dist
            for c in range(2):
                s = _dot_nt(qs[j][c], kblk) + bias
                if keep is not None:
                    s = jnp.where(keep, s, -jnp.inf)
                out.append(_online_update(s, *carry[2 * j + c], vblk))
        return tuple(out)

    nfull, diag = _causal_blocks(i, tq, tk)
    init1 = (jnp.full((tq, 1), -jnp.inf, F32), jnp.zeros((tq, 1), F32), jnp.zeros((tq, DIFF_V), F32))
    carry = lax.fori_loop(0, nfull, step, (init1,) * (2 * DIFF_HPS))
    for kb, off in diag:
        carry = step(kb, carry, off)

    lv = lam_ref[...]
    lam = (jnp.exp(jnp.sum(lv[0:1] * lv[1:2], axis=-1, keepdims=True))
           - jnp.exp(jnp.sum(lv[2:3] * lv[3:4], axis=-1, keepdims=True)) + lambda_init)
    outs = []
    for j in range(DIFF_HPS):
        (_, l1, a1), (_, l2, a2) = carry[2 * j], carry[2 * j + 1]
        o = a1 / l1 - lam * (a2 / l2)
        outs.append(_rms(o, SUBLN_EPS) * g_ref[...] * (1.0 - lambda_init))
    o_ref[...] = jnp.concatenate(outs, axis=1).astype(o_ref.dtype)


def _diff_attn(dq, dk, dv, pos_col, pos_row, lam4, subln_g, lambda_init, B, S):
    tq = min(TQ, S)
    tk = min(TK, S)
    nq = S // tq
    q3 = dq.reshape(B, S, -1)
    k3 = dk.reshape(B, S, -1)
    v3 = dv.reshape(B, S, -1)
    w = DIFF_HPS * DIFF_V
    return pl.pallas_call(
        functools.partial(_diff_attn_kernel, lambda_init, tk),
        out_shape=jax.ShapeDtypeStruct((B, S, DIFF_HEADS * DIFF_V), BF16),
        grid=(B, DIFF_HEADS // DIFF_HPS, nq),
        in_specs=[pl.BlockSpec((None, tq, w), lambda b, h, i: (b, i, h)),
                  pl.BlockSpec((None, S, w), lambda b, h, i: (b, 0, h)),
                  pl.BlockSpec((None, S, w), lambda b, h, i: (b, 0, h)),
                  pl.BlockSpec((tq, 1), lambda b, h, i: (b * nq + i, 0)),
                  pl.BlockSpec((None, 1, S), lambda b, h, i: (b, 0, 0)),
                  pl.BlockSpec((4, DIFF_HD), lambda b, h, i: (0, 0)),
                  pl.BlockSpec((1, DIFF_V), lambda b, h, i: (0, 0))],
        out_specs=pl.BlockSpec((None, tq, w), lambda b, h, i: (b, i, h)),
        compiler_params=_cparams("parallel", "parallel", "arbitrary"),
        name="diff_attn",
    )(q3, k3, v3, pos_col, pos_row, lam4, subln_g.reshape(1, DIFF_V))


def _first_argmax(v, io, n, axis):
    m = jnp.max(v, axis=axis, keepdims=True)
    ix = jnp.min(jnp.where(v == m, io, n), axis=axis, keepdims=True)
    return m, ix


def _post_attn_kernel(x_ref, mla_ref, dif_ref, mod_ref, woa_ref, wob_ref, pg_ref, fg_ref,
                      rwt_ref, rb_ref,
                      x1_ref, h2_ref, idx_ref, wts_ref, rank_ref, cnt_ref, run_ref):
    step = pl.program_id(0)
    tm = x_ref.shape[0]
    E, G, GS = N_EXPERTS, N_GROUPS, GROUP_SIZE

    @pl.when(step == 0)
    def _():
        run_ref[...] = jnp.zeros_like(run_ref)

    g_a = mod_ref[2:3, :]
    sh_f = mod_ref[3:4, :]
    sc_f = mod_ref[4:5, :]
    y = _dot(mla_ref[...], woa_ref[...]) + _dot(dif_ref[...], wob_ref[...])
    x1 = x_ref[...] + g_a * (_rms(y, NORM_EPS) * pg_ref[...])
    x1_ref[...] = x1
    h2 = (_rms(x1, NORM_EPS) * fg_ref[...]) * (1.0 + sc_f) + sh_f
    _rt_store(h2_ref, _pack_rows(h2))

    logits = lax.dot_general(rwt_ref[...], h2, (((1,), (1,)), ((), ())),
                             preferred_element_type=F32, precision=lax.Precision.HIGHEST)
    scores = _sigmoid(logits)
    sel = scores + rb_ref[...]

    sio = lax.broadcasted_iota(I32, (GS, tm), 0)
    gs_rows = []
    for g in range(G):
        blk = sel[g * GS:(g + 1) * GS, :]
        m1, i1 = _first_argmax(blk, sio, GS, 0)
        m2 = jnp.max(jnp.where(sio == i1, -jnp.inf, blk), axis=0, keepdims=True)
        gs_rows.append(m1 + m2)
    gs = jnp.concatenate(gs_rows, axis=0)

    gio = lax.broadcasted_iota(I32, (G, tm), 0)
    gkeep = jnp.zeros((G, tm), F32)
    for _ in range(TOPK_GROUPS):
        _, ix = _first_argmax(gs, gio, G, 0)
        pick = gio == ix
        gkeep = jnp.where(pick, 1.0, gkeep)
        gs = jnp.where(pick, -jnp.inf, gs)
    ekeep = jnp.concatenate([jnp.broadcast_to(gkeep[g:g + 1, :], (GS, tm)) for g in range(G)], axis=0)
    cand = jnp.where(ekeep > 0.0, sel, -jnp.inf)

    eio = lax.broadcasted_iota(I32, (E, tm), 0)
    idx_rows, w_rows = [], []
    for _ in range(TOP_K):
        _, ix = _first_argmax(cand, eio, E, 0)
        pick = eio == ix
        w_rows.append(jnp.sum(jnp.where(pick, scores, 0.0), axis=0, keepdims=True))
        cand = jnp.where(pick, -jnp.inf, cand)
        idx_rows.append(ix)
    idx = jnp.concatenate(idx_rows, axis=0)
    w = jnp.concatenate(w_rows, axis=0)
    wts_ref[...] = w / jnp.sum(w, axis=0, keepdims=True) * ROUTED_SCALE
    idx_ref[...] = idx

    onehot = jnp.zeros((E, tm), F32)
    for k in range(TOP_K):
        onehot = onehot + jnp.where(eio == idx_rows[k], 1.0, 0.0)
    tr = lax.broadcasted_iota(I32, (tm, tm), 0)
    tc = lax.broadcasted_iota(I32, (tm, tm), 1)
    before = jnp.where(tr < tc, 1.0, 0.0).astype(BF16)
    prior = _dot(onehot.astype(BF16), before) + run_ref[...]
    rank_rows = [jnp.sum(jnp.where(eio == idx_rows[k], prior, 0.0), axis=0, keepdims=True)
                 for k in range(TOP_K)]
    rank_ref[...] = jnp.concatenate(rank_rows, axis=0).astype(I32)
    run_ref[...] += jnp.sum(onehot, axis=1, keepdims=True)
    cnt_ref[...] = run_ref[...].astype(I32)


def _post_attn(x2, mla, dif, mod3, w_o, attn_post_g, ffn_pre_g, router_w, router_b, S):
    T, D = x2.shape
    tm = min(TM_POST, S)
    tpb = S // tm
    half = MLA_HEADS * MLA_V
    woa = w_o[:half].astype(BF16)
    wob = w_o[half:].astype(BF16)
    rwt = router_w.T
    full = lambda arr: pl.BlockSpec(arr.shape, lambda i: (0,) * arr.ndim)
    row = lambda w: pl.BlockSpec((tm, w), lambda i: (i, 0))
    col = lambda r: pl.BlockSpec((r, tm), lambda i: (0, i))
    ins = [x2, mla.reshape(T, -1), dif.reshape(T, -1), mod3, woa, wob, attn_post_g.reshape(1, D),
           ffn_pre_g.reshape(1, D), rwt, router_b.reshape(N_EXPERTS, 1)]
    in_specs = [row(D), row(half), row(D - half), pl.BlockSpec((None, 6, D), lambda i: (i // tpb, 0, 0))] + \
               [full(arr) for arr in ins[4:]]
    return pl.pallas_call(
        _post_attn_kernel,
        out_shape=[jax.ShapeDtypeStruct((T, D), F32), jax.ShapeDtypeStruct((T * RT, LANE), I32),
                   jax.ShapeDtypeStruct((TOP_K, T), I32), jax.ShapeDtypeStruct((TOP_K, T), F32),
                   jax.ShapeDtypeStruct((TOP_K, T), I32), jax.ShapeDtypeStruct((N_EXPERTS, 1), I32)],
        grid=(T // tm,),
        in_specs=in_specs,
        out_specs=[row(D), pl.BlockSpec((tm * RT, LANE), lambda i: (i, 0)), col(TOP_K), col(TOP_K), col(TOP_K),
                   pl.BlockSpec((N_EXPERTS, 1), lambda i: (0, 0))],
        scratch_shapes=[pltpu.VMEM((N_EXPERTS, 1), F32)],
        compiler_params=_cparams("arbitrary"),
        name="post_attn",
    )(*ins)


def _dest_kernel(idx_ref, rank_ref, ps_ref, o_ref):
    idx = idx_ref[...]
    tm = idx.shape[1]
    eio = lax.broadcasted_iota(I32, (N_EXPERTS, tm), 0)
    ps = ps_ref[...]
    rows = [jnp.sum(jnp.where(eio == idx[k:k + 1, :], ps, 0.0), axis=0, keepdims=True)
            for k in range(TOP_K)]
    o_ref[...] = jnp.concatenate(rows, axis=0).astype(I32) + rank_ref[...]


def _dest(idx, rank, pstart):
    K, T = idx.shape
    tm = min(2048, T)
    col = pl.BlockSpec((K, tm), lambda i: (0, i))
    return pl.pallas_call(
        _dest_kernel,
        out_shape=jax.ShapeDtypeStruct((K, T), I32),
        grid=(T // tm,),
        in_specs=[col, col, pl.BlockSpec((N_EXPERTS, 1), lambda i: (0, 0))],
        out_specs=col,
        compiler_params=_cparams("parallel"),
        name="dest",
    )(idx, rank, pstart.astype(F32).reshape(N_EXPERTS, 1))


def _dispatch_kernel(bs_ref, cnt_ref, dest_ref, h_ref, xs_ref, zeros, sem, zsem):
    tm = h_ref.shape[0] // RT
    blk = BM * RT
    nb = xs_ref.shape[0] // blk

    @pl.when(pl.program_id(0) == 0)
    def _():
        zeros[...] = jnp.zeros_like(zeros)

        def zcopy(g):
            return pltpu.make_async_copy(zeros, xs_ref.at[pl.ds(pl.multiple_of(g * blk, blk), blk), :], zsem)

        def has_pad(e):
            return cnt_ref[e] % BM != 0

        def fill(e, _):
            @pl.when(has_pad(e))
            def _():
                zcopy(bs_ref[e + 1] - 1).start()
            return 0

        def drain(e, _):
            @pl.when(has_pad(e))
            def _():
                zcopy(0).wait()
            return 0

        lax.fori_loop(0, N_EXPERTS, fill, 0)
        lax.fori_loop(bs_ref[N_EXPERTS], nb, lambda g, _: (zcopy(g).start(), 0)[1], 0)
        lax.fori_loop(0, N_EXPERTS, drain, 0)
        lax.fori_loop(bs_ref[N_EXPERTS], nb, lambda g, _: (zcopy(0).wait(), 0)[1], 0)

    def tile(ref, r):
        return ref.at[pl.ds(pl.multiple_of(r * RT, RT), RT), :]

    def issue(t, _):
        for k in range(TOP_K):
            pltpu.make_async_copy(tile(h_ref, t), tile(xs_ref, dest_ref[k, t]), sem).start(priority=k % 2)
        return 0

    lax.fori_loop(0, tm, issue, 0)
    for _ in range(TOP_K):
        pltpu.make_async_copy(h_ref, xs_ref.at[pl.ds(0, tm * RT), :], sem).wait()


def _dispatch(h2rt, dest, bstart, cnt, P):
    T = h2rt.shape[0] // RT
    tm = min(TM_DISP, T)
    nt = T // tm
    dest3 = dest.reshape(TOP_K, nt, tm).transpose(1, 0, 2)
    return pl.pallas_call(
        _dispatch_kernel,
        out_shape=jax.ShapeDtypeStruct((P * RT, LANE), I32),
        grid_spec=pltpu.PrefetchScalarGridSpec(
            num_scalar_prefetch=2,
            grid=(nt,),
            in_specs=[pl.BlockSpec((None, TOP_K, tm), lambda i, bs, cn: (i, 0, 0), memory_space=pltpu.SMEM),
                      pl.BlockSpec((tm * RT, LANE), lambda i, bs, cn: (i, 0))],
            out_specs=pl.BlockSpec(memory_space=pl.ANY),
            scratch_shapes=[pltpu.VMEM((BM * RT, LANE), I32), pltpu.SemaphoreType.DMA(()),
                            pltpu.SemaphoreType.DMA(())],
        ),
        compiler_params=_cparams("arbitrary"),
        name="dispatch",
    )(bstart, cnt, dest3, h2rt)


NBUF_X = 16
X_AHEAD = NBUF_X - 2
NBUF_Y = 8


def _experts_kernel(bs_ref, wg_ref, wu_ref, wd_ref, xs_ref, ys_ref, wgb, wub, wdb, xbuf, ybuf, xsem, ysem):
    e = pl.program_id(0)
    blk = BM * RT
    nb = xs_ref.shape[0] // blk
    nused = bs_ref[N_EXPERTS]
    g0 = bs_ref[e]
    g1 = bs_ref[e + 1]

    def rows(g):
        return pl.ds(pl.multiple_of(g * blk, blk), blk)

    def x_copy(g, slot):
        return pltpu.make_async_copy(xs_ref.at[rows(g), :], xbuf.at[slot], xsem.at[slot])

    def y_copy(g, slot):
        return pltpu.make_async_copy(ybuf.at[slot], ys_ref.at[rows(g), :], ysem.at[slot])

    @pl.when(e == 0)
    def _():
        for j in range(X_AHEAD):
            @pl.when(j < nused)
            def _():
                x_copy(j, j).start()

    def step(gs):
        for g in gs:
            x_copy(g, g % NBUF_X).wait()
            nxt = g + X_AHEAD

            @pl.when(nxt < nused)
            def _():
                x_copy(nxt, nxt % NBUF_X).start()

        x = _unpack_rows(jnp.concatenate([_rt_load(xbuf, BM, g % NBUF_X) for g in gs], axis=0), BF16)
        gate = _dot(x, wgb[...])
        up = _dot(x, wub[...])
        a = (gate * _sigmoid(gate)) * up
        y = _dot(a.astype(BF16), wdb[...])
        for j, g in enumerate(gs):
            ys = g % NBUF_Y

            @pl.when(g >= NBUF_Y)
            def _():
                y_copy(g - NBUF_Y, ys).wait()

            _rt_store(ybuf, _pack_rows(y[j * BM:(j + 1) * BM]), ys)
            y_copy(g, ys).start()

    @pl.when(g1 > g0)
    def _():
        wgb[...] = wg_ref[...].astype(BF16)
        wub[...] = wu_ref[...].astype(BF16)
        wdb[...] = wd_ref[...].astype(BF16)

        def pair(i, _):
            g = g0 + 2 * i
            step([g, g + 1])
            return 0

        lax.fori_loop(0, (g1 - g0) // 2, pair, 0)

        @pl.when((g1 - g0) % 2 == 1)
        def _():
            step([g1 - 1])

    @pl.when(e == N_EXPERTS - 1)
    def _():
        for j in range(NBUF_Y):
            @pl.when(nused - 1 - j >= 0)
            def _():
                y_copy(0, (nused - 1 - j) % NBUF_Y).wait()
        ybuf[0] = jnp.zeros(ybuf.shape[1:], I32)
        lax.fori_loop(nused, nb, lambda g, _: (y_copy(g, 0).start(), 0)[1], 0)
        lax.fori_loop(nused, nb, lambda g, _: (y_copy(0, 0).wait(), 0)[1], 0)


def _experts(xs, bstart, w_gate, w_up, w_down):
    E, D, F = w_gate.shape
    return pl.pallas_call(
        _experts_kernel,
        out_shape=jax.ShapeDtypeStruct(xs.shape, I32),
        grid_spec=pltpu.PrefetchScalarGridSpec(
            num_scalar_prefetch=1,
            grid=(E,),
            in_specs=[pl.BlockSpec((None, D, F), lambda e, bs: (e, 0, 0)),
                      pl.BlockSpec((None, D, F), lambda e, bs: (e, 0, 0)),
                      pl.BlockSpec((None, F, D), lambda e, bs: (e, 0, 0)),
                      pl.BlockSpec(memory_space=pl.ANY)],
            out_specs=pl.BlockSpec(memory_space=pl.ANY),
            scratch_shapes=[pltpu.VMEM((D, F), BF16), pltpu.VMEM((D, F), BF16), pltpu.VMEM((F, D), BF16),
                            pltpu.VMEM((NBUF_X, BM * RT, LANE), I32), pltpu.VMEM((NBUF_Y, BM * RT, LANE), I32),
                            pltpu.SemaphoreType.DMA((NBUF_X,)), pltpu.SemaphoreType.DMA((NBUF_Y,))],
        ),
        compiler_params=_cparams("arbitrary"),
        name="experts",
    )(bstart, w_gate, w_up, w_down, xs)


def _combine_kernel(dcur_ref, dnxt_ref, w_ref, x1_ref, h_ref, mod_ref, pg_ref, sg_ref, su_ref, sd_ref,
                    ys_ref, o_ref, rows_a, rows_b, sem):
    i = pl.program_id(0)
    n = pl.num_programs(0)
    tm = x1_ref.shape[0]

    def tile(r):
        return pl.ds(pl.multiple_of(r * RT, RT), RT)

    def request(dref, buf, s, t):
        for k in range(TOP_K):
            pltpu.make_async_copy(ys_ref.at[tile(dref[k, t]), :], buf.at[k, tile(t), :],
                                  sem.at[s]).start(priority=k % 2)

    def drain(buf, s):
        for k in range(TOP_K):
            pltpu.make_async_copy(ys_ref.at[pl.ds(0, tm * RT), :], buf.at[k], sem.at[s]).wait()

    @pl.when(i == 0)
    def _():
        lax.fori_loop(0, tm, lambda t, _: (request(dcur_ref, rows_a, 0, t), 0)[1], 0)

    def step(cur, s_cur, nxt, s_nxt):
        drain(cur, s_cur)
        for t in range(tm):
            request(dnxt_ref, nxt, s_nxt, t)
        w = w_ref[...]
        routed = w[:, 0:1] * _unpack_rows(_rt_load(cur, tm, 0), F32)
        for k in range(1, TOP_K):
            routed = routed + w[:, k:k + 1] * _unpack_rows(_rt_load(cur, tm, k), F32)
        hb = _unpack_rows(_rt_load(h_ref, tm), BF16)
        g = _dot(hb, sg_ref[...])
        u = _dot(hb, su_ref[...])
        shared = _dot(((g * _sigmoid(g)) * u).astype(BF16), sd_ref[...])
        y = routed + shared
        g_f = mod_ref[5:6, :]
        o_ref[...] = x1_ref[...] + g_f * (_rms(y, NORM_EPS) * pg_ref[...])

    even = i % 2 == 0

    @pl.when(even)
    def _():
        step(rows_a, 0, rows_b, 1)

    @pl.when(jnp.logical_not(even))
    def _():
        step(rows_b, 1, rows_a, 0)

    @pl.when(i == n - 1)
    def _():
        @pl.when(even)
        def _():
            drain(rows_b, 1)

        @pl.when(jnp.logical_not(even))
        def _():
            drain(rows_a, 0)


def _combine(dest, wts, x1, h2, mod3, ffn_post_g, sw_gate, sw_up, sw_down, ys, S):
    T, D = x1.shape
    tm = min(TM_COMB, S)
    nt = T // tm
    tpb = S // tm
    dest3 = dest.reshape(TOP_K, nt, tm).transpose(1, 0, 2)
    w_tk = wts.T
    full = lambda arr: pl.BlockSpec(arr.shape, lambda i: (0,) * arr.ndim)
    row = lambda w: pl.BlockSpec((tm, w), lambda i: (i, 0))
    sg, su, sd = sw_gate.astype(BF16), sw_up.astype(BF16), sw_down.astype(BF16)
    pg = ffn_post_g.reshape(1, D)
    return pl.pallas_call(
        _combine_kernel,
        out_shape=jax.ShapeDtypeStruct((T, D), F32),
        grid=(nt,),
        in_specs=[pl.BlockSpec((None, TOP_K, tm), lambda i: (i, 0, 0), memory_space=pltpu.SMEM),
                  pl.BlockSpec((None, TOP_K, tm), lambda i: (jnp.minimum(i + 1, nt - 1), 0, 0),
                               memory_space=pltpu.SMEM),
                  row(TOP_K), row(D), pl.BlockSpec((tm * RT, LANE), lambda i: (i, 0)),
                  pl.BlockSpec((None, 6, D), lambda i: (i // tpb, 0, 0)),
                  full(pg), full(sg), full(su), full(sd),
                  pl.BlockSpec(memory_space=pl.ANY)],
        out_specs=row(D),
        scratch_shapes=[pltpu.VMEM((TOP_K, tm * RT, LANE), I32), pltpu.VMEM((TOP_K, tm * RT, LANE), I32),
                        pltpu.SemaphoreType.DMA((2,))],
        compiler_params=_cparams("arbitrary"),
        name="combine",
    )(dest3, dest3, w_tk, x1, h2, mod3, pg, sg, su, sd, ys)


def _moe(h2, x1, idx, wts, rank, counts, mod3, ffn_post_g, exp_w_gate, exp_w_up, exp_w_down,
         sw_gate, sw_up, sw_down, S):
    T, D = x1.shape
    A = T * TOP_K
    P = A + N_EXPERTS * BM
    cnt = counts.reshape(N_EXPERTS)
    blocks = (cnt + BM - 1) // BM
    bstart = jnp.concatenate([jnp.zeros((1,), I32), jnp.cumsum(blocks).astype(I32)])
    pstart = bstart[:-1] * BM

    dest = _dest(idx, rank, pstart)
    xs = _dispatch(h2, dest, bstart, cnt, P)
    ys = _experts(xs, bstart, exp_w_gate, exp_w_up, exp_w_down)
    return _combine(dest, wts, x1, h2, mod3, ffn_post_g, sw_gate, sw_up, sw_down, ys, S)


def _layer(x, c, positions, lambda_init, w_ada, b_ada, attn_pre_g, attn_post_g, w_in, q_norm_g, kv_norm_g,
           w_uq, w_ukv, lam_q1, lam_k1, lam_q2, lam_k2, diff_subln_g, w_o, ffn_pre_g, ffn_post_g,
           router_w, router_b, exp_w_gate, exp_w_up, exp_w_down, sw_gate, sw_up, sw_down):
    B, S, D = x.shape
    T = B * S
    x2 = x.reshape(T, D)
    posf = positions.astype(F32)
    pos_col = posf.reshape(T, 1)
    pos_row = posf.reshape(B, 1, S)

    mod3 = _ada(c, w_ada, b_ada).reshape(B, 6, D)
    q, k, v, dq, dk, dv = _pre_attn(x2, mod3, pos_col, attn_pre_g, w_in, q_norm_g, kv_norm_g, w_uq, w_ukv, S)
    mla = _mla_attn(q, k, v, B, S)
    lam4 = jnp.stack([lam_q1, lam_k1, lam_q2, lam_k2])
    dif = _diff_attn(dq, dk, dv, pos_col, pos_row, lam4, diff_subln_g, lambda_init, B, S)
    x1, h2, idx, wts, rank, counts = _post_attn(x2, mla, dif, mod3, w_o, attn_post_g, ffn_pre_g,
                                                router_w, router_b, S)
    out = _moe(h2, x1, idx, wts, rank, counts, mod3, ffn_post_g, exp_w_gate, exp_w_up, exp_w_down,
               sw_gate, sw_up, sw_down, S)
    return out.reshape(B, S, D)


def kernel(x, c, positions, w_ada, b_ada, attn_pre_g, attn_post_g, w_in, q_norm_g, kv_norm_g, w_uq, w_ukv,
           lam_q1, lam_k1, lam_q2, lam_k2, diff_subln_g, w_o, ffn_pre_g, ffn_post_g, router_w, router_b,
           exp_w_gate, exp_w_up, exp_w_down, shared_w_gate, shared_w_up, shared_w_down):
    depth = w_ada.shape[0]
    for l in range(depth):
        lambda_init = 0.8 - 0.6 * math.exp(-0.3 * l)
        x = _layer(x, c, positions, lambda_init, w_ada[l], b_ada[l], attn_pre_g[l], attn_post_g[l], w_in[l],
                   q_norm_g[l], kv_norm_g[l], w_uq[l], w_ukv[l], lam_q1[l], lam_k1[l], lam_q2[l], lam_k2[l],
                   diff_subln_g[l], w_o[l], ffn_pre_g[l], ffn_post_g[l], router_w[l], router_b[l],
                   exp_w_gate[l], exp_w_up[l], exp_w_down[l], shared_w_gate[l], shared_w_up[l],
                   shared_w_down[l])
    return x
```

```python
import functools
import math

import jax
import jax.numpy as jnp
from jax import lax
from jax.experimental import pallas as pl
from jax.experimental.pallas import tpu as pltpu

F32 = jnp.float32
BF16 = jnp.bfloat16
I32 = jnp.int32

MLA_HEADS = 8
MLA_NOPE = 64
MLA_ROPE = 32
MLA_V = 64
MLA_Q_RANK = 256
MLA_KV_RANK = 128
ROPE_BASE = 10000.0
DIFF_HEADS = 4
DIFF_HD = 64
DIFF_V = 128
N_EXPERTS = 256
TOP_K = 8
N_GROUPS = 8
GROUP_SIZE = N_EXPERTS // N_GROUPS
TOPK_GROUPS = 4
ROUTED_SCALE = 2.5
NORM_EPS = 1e-6
SUBLN_EPS = 1e-5
LOG2E = 1.4426950408889634

LANE = 128
HEAD_PAD = 128

TM_PRE = 512
TQ = 512
TK = 512
TM_POST = 512
TM_DISP = 1024
TM_COMB = 128
BM = 128
VMEM_LIMIT = 48 * 1024 * 1024


def _cparams(*sem):
    return pltpu.CompilerParams(dimension_semantics=sem, vmem_limit_bytes=VMEM_LIMIT)


def _rms(x, eps):
    return x * lax.rsqrt(jnp.mean(x * x, axis=-1, keepdims=True) + eps)


def _sigmoid(x):
    return 1.0 / (1.0 + jnp.exp(-x))


def _dot(a, b):
    return jnp.dot(a, b, preferred_element_type=F32)


RT = 4
HI_MASK = -65536


def _pack_rows(x):
    half = x.shape[1] // 2
    lo = lax.bitcast_convert_type(x[:, :half].astype(BF16).astype(F32), I32)
    hi = lax.bitcast_convert_type(x[:, half:].astype(BF16).astype(F32), I32)
    return lax.shift_right_logical(lo, 16) | (hi & HI_MASK)


def _unpack_rows(u, dtype):
    lo = lax.bitcast_convert_type(lax.shift_left(u, 16), F32)
    hi = lax.bitcast_convert_type(u & HI_MASK, F32)
    return jnp.concatenate([lo.astype(dtype), hi.astype(dtype)], axis=1)


def _rt_load(ref, n, *lead):
    return jnp.concatenate([ref[(*lead, pl.ds(j, n, stride=RT), slice(None))] for j in range(RT)], axis=1)


def _rt_store(ref, val, *lead):
    n = val.shape[0]
    for j in range(RT):
        ref[(*lead, pl.ds(j, n, stride=RT), slice(None))] = val[:, j * LANE:(j + 1) * LANE]


def _dot_nt(a, b):
    return lax.dot_general(a, b, (((1,), (1,)), ((), ())), preferred_element_type=F32)


def _ada_kernel(c_ref, w_ref, b_ref, o_ref):
    c = c_ref[...]
    a = c * _sigmoid(c)
    o_ref[...] = jnp.dot(a, w_ref[...], preferred_element_type=F32,
                         precision=lax.Precision.HIGHEST) + b_ref[...]


def _ada(c, w_ada, b_ada):
    B, D = c.shape
    n = w_ada.shape[1]
    return pl.pallas_call(
        _ada_kernel,
        out_shape=jax.ShapeDtypeStruct((B, n), F32),
        grid=(n // D,),
        in_specs=[pl.BlockSpec((B, D), lambda j: (0, 0)),
                  pl.BlockSpec((D, D), lambda j: (0, j)),
                  pl.BlockSpec((1, D), lambda j: (0, j))],
        out_specs=pl.BlockSpec((B, D), lambda j: (0, j)),
        compiler_params=_cparams("arbitrary"),
        name="ada",
    )(c, w_ada, b_ada.reshape(1, n))


_C_CQ = 0
_C_CKV = _C_CQ + MLA_Q_RANK
_C_KRA = _C_CKV + MLA_KV_RANK
_C_KRB = _C_KRA + LANE
_C_DQ = _C_KRB + LANE
_C_DK = _C_DQ + DIFF_HEADS * DIFF_V
_C_DV = _C_DK + DIFF_HEADS * DIFF_V
_C_END = _C_DV + DIFF_HEADS * DIFF_V


def _pre_attn_kernel(x_ref, mod_ref, pos_ref, inv_ref, g_ref, w1_ref, qg_ref, kvg_ref,
                     wqa_ref, wqb_ref, wkn_ref, wv_ref,
                     q_ref, k_ref, v_ref, dq_ref, dk_ref, dv_ref):
    x = x_ref[...]
    sh = mod_ref[0:1, :]
    sc = mod_ref[1:2, :]
    h = _rms(x, NORM_EPS) * g_ref[...]
    h = h * (1.0 + sc) + sh
    p = _dot(h.astype(BF16), w1_ref[...])

    ang = pos_ref[...] * inv_ref[...]
    lane = lax.broadcasted_iota(I32, ang.shape, 1)
    in_rope = (lane >= MLA_NOPE) & (lane < MLA_NOPE + MLA_ROPE)
    cos_r = jnp.where(in_rope, jnp.cos(ang), 0.0)
    sin_r = jnp.where(in_rope, jnp.sin(ang), 0.0)
    cos_q = jnp.where(lane < MLA_NOPE, 1.0, cos_r)

    cqn = (_rms(p[:, _C_CQ:_C_CKV], NORM_EPS) * qg_ref[...]).astype(BF16)
    qa = _dot(cqn, wqa_ref[...])
    qb = _dot(cqn, wqb_ref[...])
    q_scale = LOG2E / math.sqrt(MLA_NOPE + MLA_ROPE)
    cos_t = jnp.concatenate([cos_q] * MLA_HEADS, axis=1)
    sin_t = jnp.concatenate([sin_r] * MLA_HEADS, axis=1)
    q_ref[...] = ((qa * cos_t + qb * sin_t) * q_scale).astype(BF16)

    ckvn = (_rms(p[:, _C_CKV:_C_KRA], NORM_EPS) * kvg_ref[...]).astype(BF16)
    kn = _dot(ckvn, wkn_ref[...])
    kr = p[:, _C_KRA:_C_KRB] * cos_r + p[:, _C_KRB:_C_DQ] * sin_r
    k_ref[...] = (kn + jnp.concatenate([kr] * MLA_HEADS, axis=1)).astype(BF16)
    lane_t = lax.broadcasted_iota(I32, kn.shape, 1)
    ones_col = jnp.where(lane_t % HEAD_PAD == MLA_V, 1.0, 0.0)
    v_ref[...] = (_dot(ckvn, wv_ref[...]) + ones_col).astype(BF16)

    dq_ref[...] = (p[:, _C_DQ:_C_DK] * (LOG2E / math.sqrt(DIFF_HD))).astype(BF16)
    dk_ref[...] = p[:, _C_DK:_C_DV].astype(BF16)
    dv_ref[...] = p[:, _C_DV:_C_END].astype(BF16)


def _pre_attn(x2, mod3, pos_col, attn_pre_g, w_in, q_norm_g, kv_norm_g, w_uq, w_ukv, S):
    T, D = x2.shape
    tm = min(TM_PRE, S)
    tpb = S // tm
    f = lambda a: a.astype(BF16)
    d3 = DIFF_HEADS * DIFF_V
    a = MLA_Q_RANK
    b = a + MLA_KV_RANK
    c = b + MLA_ROPE

    def swap(r):
        hlf = MLA_ROPE // 2
        return jnp.concatenate([-r[..., hlf:], r[..., :hlf]], axis=-1)

    def pad_rope(r):
        return jnp.pad(r, ((0, 0), (MLA_NOPE, LANE - MLA_NOPE - MLA_ROPE)))

    w_kr = w_in[:, b:c]
    w1 = jnp.concatenate([w_in[:, :b], pad_rope(w_kr), pad_rope(swap(w_kr)), w_in[:, c:]], axis=1)
    assert w1.shape[1] == _C_END
    padq = HEAD_PAD - MLA_NOPE - MLA_ROPE
    wqa = jnp.pad(w_uq, ((0, 0), (0, 0), (0, padq))).reshape(MLA_Q_RANK, MLA_HEADS * HEAD_PAD)
    q_rope = w_uq[..., MLA_NOPE:]
    wqb = jnp.pad(swap(q_rope), ((0, 0), (0, 0), (MLA_NOPE, padq))).reshape(MLA_Q_RANK, MLA_HEADS * HEAD_PAD)
    wkn = jnp.pad(w_ukv[..., :MLA_NOPE], ((0, 0), (0, 0), (0, HEAD_PAD - MLA_NOPE))).reshape(
        MLA_KV_RANK, MLA_HEADS * HEAD_PAD)
    wv = jnp.pad(w_ukv[..., MLA_NOPE:], ((0, 0), (0, 0), (0, HEAD_PAD - MLA_V))).reshape(
        MLA_KV_RANK, MLA_HEADS * HEAD_PAD)

    inv = 1.0 / (ROPE_BASE ** (jnp.arange(0, MLA_ROPE, 2, dtype=F32) / MLA_ROPE))
    inv_lane = jnp.pad(jnp.concatenate([inv, inv]), (MLA_NOPE, LANE - MLA_NOPE - MLA_ROPE)).reshape(1, LANE)

    full = lambda arr: pl.BlockSpec(arr.shape, lambda i: (0,) * arr.ndim)
    row = lambda w: pl.BlockSpec((tm, w), lambda i: (i, 0))
    ins = [x2, mod3, pos_col, inv_lane, attn_pre_g.reshape(1, D), f(w1), q_norm_g.reshape(1, -1),
           kv_norm_g.reshape(1, -1), f(wqa), f(wqb), f(wkn), f(wv)]
    in_specs = [row(D), pl.BlockSpec((None, 6, D), lambda i: (i // tpb, 0, 0)), row(1)] + \
               [full(arr) for arr in ins[3:]]
    widths = [MLA_HEADS * HEAD_PAD, MLA_HEADS * HEAD_PAD, MLA_HEADS * HEAD_PAD, d3, d3, d3]
    return pl.pallas_call(
        _pre_attn_kernel,
        out_shape=[jax.ShapeDtypeStruct((T, w), BF16) for w in widths],
        grid=(T // tm,),
        in_specs=in_specs,
        out_specs=[row(w) for w in widths],
        compiler_params=_cparams("parallel"),
        name="pre_attn",
    )(*ins)


def _online_update(s, m, l, acc, v):
    m_new = jnp.maximum(m, jnp.max(s, axis=-1, keepdims=True))
    p = jnp.exp2(s - m_new)
    alpha = jnp.exp2(m - m_new)
    l_new = alpha * l + jnp.sum(p, axis=-1, keepdims=True)
    acc_new = alpha * acc + _dot(p.astype(BF16), v)
    return m_new, l_new, acc_new


MLA_HPS = 4


def _causal_blocks(i, tq, tk):
    nfull = (i * tq) // tk
    diag = [(nfull + j, i * tq - (nfull + j) * tk) for j in range(max(1, tq // tk))]
    return nfull, diag


def _keep(tq, tk, off):
    r = lax.broadcasted_iota(I32, (tq, tk), 0)
    c = lax.broadcasted_iota(I32, (tq, tk), 1)
    return r + off >= c


def _mla_attn_kernel(tk, q_ref, k_ref, v_ref, o_ref):
    i = pl.program_id(2)
    tq = q_ref.shape[0]
    sl = [slice(j * HEAD_PAD, (j + 1) * HEAD_PAD) for j in range(MLA_HPS)]
    qs = [q_ref[:, sl[j]] for j in range(MLA_HPS)]

    def step(kb, carry, off=None):
        r0 = pl.multiple_of(kb * tk, tk)
        out = []
        for j in range(MLA_HPS):
            m, acc = carry[j]
            s = _dot_nt(qs[j], k_ref[pl.ds(r0, tk), sl[j]])
            if off is not None:
                s = jnp.where(_keep(tq, tk, off), s, -jnp.inf)
            m_new = jnp.maximum(m, jnp.max(s, axis=-1, keepdims=True))
            p = jnp.exp2(s - m_new)
            acc = jnp.exp2(m - m_new) * acc + _dot(p.astype(BF16), v_ref[pl.ds(r0, tk), sl[j]])
            out.append((m_new, acc))
        return tuple(out)

    nfull, diag = _causal_blocks(i, tq, tk)
    init = tuple((jnp.full((tq, 1), -jnp.inf, F32), jnp.zeros((tq, HEAD_PAD), F32)) for _ in range(MLA_HPS))
    carry = lax.fori_loop(0, nfull, step, init)
    for kb, off in diag:
        carry = step(kb, carry, off)
    outs = [acc[:, :MLA_V] / acc[:, MLA_V:MLA_V + 1] for _, acc in carry]
    o_ref[...] = jnp.concatenate(outs, axis=1).astype(o_ref.dtype)


def _mla_attn(q, k, v, B, S):
    tq = min(TQ, S)
    tk = min(TK, S)
    q3 = q.reshape(B, S, -1)
    k3 = k.reshape(B, S, -1)
    v3 = v.reshape(B, S, -1)
    w = MLA_HPS * HEAD_PAD
    return pl.pallas_call(
        functools.partial(_mla_attn_kernel, tk),
        out_shape=jax.ShapeDtypeStruct((B, S, MLA_HEADS * MLA_V), BF16),
        grid=(B, MLA_HEADS // MLA_HPS, S // tq),
        in_specs=[pl.BlockSpec((None, tq, w), lambda b, h, i: (b, i, h)),
                  pl.BlockSpec((None, S, w), lambda b, h, i: (b, 0, h)),
                  pl.BlockSpec((None, S, w), lambda b, h, i: (b, 0, h))],
        out_specs=pl.BlockSpec((None, tq, MLA_HPS * MLA_V), lambda b, h, i: (b, i, h)),
        compiler_params=_cparams("parallel", "parallel", "arbitrary"),
        name="mla_attn",
    )(q3, k3, v3)


DIFF_HPS = 2


def _diff_attn_kernel(lambda_init, tk, q_ref, k_ref, v_ref, pc_ref, pr_ref, lam_ref, g_ref, o_ref):
    i = pl.program_id(2)
    tq = q_ref.shape[0]
    sl = [slice(j * DIFF_V, (j + 1) * DIFF_V) for j in range(DIFF_HPS)]
    lane = lax.broadcasted_iota(I32, (tq, DIFF_V), 1)
    qs, nslopes = [], []
    for j in range(DIFF_HPS):
        q = q_ref[:, sl[j]]
        zero = jnp.zeros_like(q)
        qs.append((jnp.where(lane < DIFF_HD, q, zero), jnp.where(lane >= DIFF_HD, q, zero)))
        hv = jnp.full((1, 1), pl.program_id(1) * DIFF_HPS + j, I32).astype(F32)
        nslopes.append(-LOG2E * jnp.exp2(-8.0 * (hv + 1.0) / DIFF_HEADS))
    pq = pc_ref[...]

    def step(kb, carry, off=None):
        r0 = pl.multiple_of(kb * tk, tk)
        dist = jnp.abs(pq - pr_ref[:, pl.ds(r0, tk)])
        keep = None if off is None else _keep(tq, tk, off)
        out = []
        for j in range(DIFF_HPS):
            kblk = k_ref[pl.ds(r0, tk), sl[j]]
            vblk = v_ref[pl.ds(r0, tk), sl[j]]
            bias = nslopes[j] * dist
            for c in range(2):
                s = _dot_nt(qs[j][c], kblk) + bias
                if keep is not None:
                    s = jnp.where(keep, s, -jnp.inf)
                out.append(_online_update(s, *carry[2 * j + c], vblk))
        return tuple(out)

    nfull, diag = _causal_blocks(i, tq, tk)
    init1 = (jnp.full((tq, 1), -jnp.inf, F32), jnp.zeros((tq, 1), F32), jnp.zeros((tq, DIFF_V), F32))
    carry = lax.fori_loop(0, nfull, step, (init1,) * (2 * DIFF_HPS))
    for kb, off in diag:
        carry = step(kb, carry, off)

    lv = lam_ref[...]
    lam = (jnp.exp(jnp.sum(lv[0:1] * lv[1:2], axis=-1, keepdims=True))
           - jnp.exp(jnp.sum(lv[2:3] * lv[3:4], axis=-1, keepdims=True)) + lambda_init)
    outs = []
    for j in range(DIFF_HPS):
        (_, l1, a1), (_, l2, a2) = carry[2 * j], carry[2 * j + 1]
        o = a1 / l1 - lam * (a2 / l2)
        outs.append(_rms(o, SUBLN_EPS) * g_ref[...] * (1.0 - lambda_init))
    o_ref[...] = jnp.concatenate(outs, axis=1).astype(o_ref.dtype)


def _diff_attn(dq, dk, dv, pos_col, pos_row, lam4, subln_g, lambda_init, B, S):
    tq = min(TQ, S)
    tk = min(TK, S)
    nq = S // tq
    q3 = dq.reshape(B, S, -1)
    k3 = dk.reshape(B, S, -1)
    v3 = dv.reshape(B, S, -1)
    w = DIFF_HPS * DIFF_V
    return pl.pallas_call(
        functools.partial(_diff_attn_kernel, lambda_init, tk),
        out_shape=jax.ShapeDtypeStruct((B, S, DIFF_HEADS * DIFF_V), BF16),
        grid=(B, DIFF_HEADS // DIFF_HPS, nq),
        in_specs=[pl.BlockSpec((None, tq, w), lambda b, h, i: (b, i, h)),
                  pl.BlockSpec((None, S, w), lambda b, h, i: (b, 0, h)),
                  pl.BlockSpec((None, S, w), lambda b, h, i: (b, 0, h)),
                  pl.BlockSpec((tq, 1), lambda b, h, i: (b * nq + i, 0)),
                  pl.BlockSpec((None, 1, S), lambda b, h, i: (b, 0, 0)),
                  pl.BlockSpec((4, DIFF_HD), lambda b, h, i: (0, 0)),
                  pl.BlockSpec((1, DIFF_V), lambda b, h, i: (0, 0))],
        out_specs=pl.BlockSpec((None, tq, w), lambda b, h, i: (b, i, h)),
        compiler_params=_cparams("parallel", "parallel", "arbitrary"),
        name="diff_attn",
    )(q3, k3, v3, pos_col, pos_row, lam4, subln_g.reshape(1, DIFF_V))


def _first_argmax(v, io, n, axis):
    m = jnp.max(v, axis=axis, keepdims=True)
    ix = jnp.min(jnp.where(v == m, io, n), axis=axis, keepdims=True)
    return m, ix


def _post_attn_kernel(x_ref, mla_ref, dif_ref, mod_ref, woa_ref, wob_ref, pg_ref, fg_ref,
                      rwt_ref, rb_ref,
                      x1_ref, h2_ref, idx_ref, wts_ref, rank_ref, cnt_ref, run_ref):
    step = pl.program_id(0)
    tm = x_ref.shape[0]
    E, G, GS = N_EXPERTS, N_GROUPS, GROUP_SIZE

    @pl.when(step == 0)
    def _():
        run_ref[...] = jnp.zeros_like(run_ref)

    g_a = mod_ref[2:3, :]
    sh_f = mod_ref[3:4, :]
    sc_f = mod_ref[4:5, :]
    y = _dot(mla_ref[...], woa_ref[...]) + _dot(dif_ref[...], wob_ref[...])
    x1 = x_ref[...] + g_a * (_rms(y, NORM_EPS) * pg_ref[...])
    x1_ref[...] = x1
    h2 = (_rms(x1, NORM_EPS) * fg_ref[...]) * (1.0 + sc_f) + sh_f
    _rt_store(h2_ref, _pack_rows(h2))

    logits = lax.dot_general(rwt_ref[...], h2, (((1,), (1,)), ((), ())),
                             preferred_element_type=F32, precision=lax.Precision.HIGHEST)
    scores = _sigmoid(logits)
    sel = scores + rb_ref[...]

    sio = lax.broadcasted_iota(I32, (GS, tm), 0)
    gs_rows = []
    for g in range(G):
        blk = sel[g * GS:(g + 1) * GS, :]
        m1, i1 = _first_argmax(blk, sio, GS, 0)
        m2 = jnp.max(jnp.where(sio == i1, -jnp.inf, blk), axis=0, keepdims=True)
        gs_rows.append(m1 + m2)
    gs = jnp.concatenate(gs_rows, axis=0)

    gio = lax.broadcasted_iota(I32, (G, tm), 0)
    gkeep = jnp.zeros((G, tm), F32)
    for _ in range(TOPK_GROUPS):
        _, ix = _first_argmax(gs, gio, G, 0)
        pick = gio == ix
        gkeep = jnp.where(pick, 1.0, gkeep)
        gs = jnp.where(pick, -jnp.inf, gs)
    ekeep = jnp.concatenate([jnp.broadcast_to(gkeep[g:g + 1, :], (GS, tm)) for g in range(G)], axis=0)
    cand = jnp.where(ekeep > 0.0, sel, -jnp.inf)

    eio = lax.broadcasted_iota(I32, (E, tm), 0)
    idx_rows, w_rows = [], []
    for _ in range(TOP_K):
        _, ix = _first_argmax(cand, eio, E, 0)
        pick = eio == ix
        w_rows.append(jnp.sum(jnp.where(pick, scores, 0.0), axis=0, keepdims=True))
        cand = jnp.where(pick, -jnp.inf, cand)
        idx_rows.append(ix)
    idx = jnp.concatenate(idx_rows, axis=0)
    w = jnp.concatenate(w_rows, axis=0)
    wts_ref[...] = w / jnp.sum(w, axis=0, keepdims=True) * ROUTED_SCALE
    idx_ref[...] = idx

    onehot = jnp.zeros((E, tm), F32)
    for k in range(TOP_K):
        onehot = onehot + jnp.where(eio == idx_rows[k], 1.0, 0.0)
    tr = lax.broadcasted_iota(I32, (tm, tm), 0)
    tc = lax.broadcasted_iota(I32, (tm, tm), 1)
    before = jnp.where(tr < tc, 1.0, 0.0).astype(BF16)
    prior = _dot(onehot.astype(BF16), before) + run_ref[...]
    rank_rows = [jnp.sum(jnp.where(eio == idx_rows[k], prior, 0.0), axis=0, keepdims=True)
                 for k in range(TOP_K)]
    rank_ref[...] = jnp.concatenate(rank_rows, axis=0).astype(I32)
    run_ref[...] += jnp.sum(onehot, axis=1, keepdims=True)
    cnt_ref[...] = run_ref[...].astype(I32)


def _post_attn(x2, mla, dif, mod3, w_o, attn_post_g, ffn_pre_g, router_w, router_b, S):
    T, D = x2.shape
    tm = min(TM_POST, S)
    tpb = S // tm
    half = MLA_HEADS * MLA_V
    woa = w_o[:half].astype(BF16)
    wob = w_o[half:].astype(BF16)
    rwt = router_w.T
    full = lambda arr: pl.BlockSpec(arr.shape, lambda i: (0,) * arr.ndim)
    row = lambda w: pl.BlockSpec((tm, w), lambda i: (i, 0))
    col = lambda r: pl.BlockSpec((r, tm), lambda i: (0, i))
    ins = [x2, mla.reshape(T, -1), dif.reshape(T, -1), mod3, woa, wob, attn_post_g.reshape(1, D),
           ffn_pre_g.reshape(1, D), rwt, router_b.reshape(N_EXPERTS, 1)]
    in_specs = [row(D), row(half), row(D - half), pl.BlockSpec((None, 6, D), lambda i: (i // tpb, 0, 0))] + \
               [full(arr) for arr in ins[4:]]
    return pl.pallas_call(
        _post_attn_kernel,
        out_shape=[jax.ShapeDtypeStruct((T, D), F32), jax.ShapeDtypeStruct((T * RT, LANE), I32),
                   jax.ShapeDtypeStruct((TOP_K, T), I32), jax.ShapeDtypeStruct((TOP_K, T), F32),
                   jax.ShapeDtypeStruct((TOP_K, T), I32), jax.ShapeDtypeStruct((N_EXPERTS, 1), I32)],
        grid=(T // tm,),
        in_specs=in_specs,
        out_specs=[row(D), pl.BlockSpec((tm * RT, LANE), lambda i: (i, 0)), col(TOP_K), col(TOP_K), col(TOP_K),
                   pl.BlockSpec((N_EXPERTS, 1), lambda i: (0, 0))],
        scratch_shapes=[pltpu.VMEM((N_EXPERTS, 1), F32)],
        compiler_params=_cparams("arbitrary"),
        name="post_attn",
    )(*ins)


def _dest_kernel(idx_ref, rank_ref, ps_ref, o_ref):
    idx = idx_ref[...]
    tm = idx.shape[1]
    eio = lax.broadcasted_iota(I32, (N_EXPERTS, tm), 0)
    ps = ps_ref[...]
    rows = [jnp.sum(jnp.where(eio == idx[k:k + 1, :], ps, 0.0), axis=0, keepdims=True)
            for k in range(TOP_K)]
    o_ref[...] = jnp.concatenate(rows, axis=0).astype(I32) + rank_ref[...]


def _dest(idx, rank, pstart):
    K, T = idx.shape
    tm = min(2048, T)
    col = pl.BlockSpec((K, tm), lambda i: (0, i))
    return pl.pallas_call(
        _dest_kernel,
        out_shape=jax.ShapeDtypeStruct((K, T), I32),
        grid=(T // tm,),
        in_specs=[col, col, pl.BlockSpec((N_EXPERTS, 1), lambda i: (0, 0))],
        out_specs=col,
        compiler_params=_cparams("parallel"),
        name="dest",
    )(idx, rank, pstart.astype(F32).reshape(N_EXPERTS, 1))


def _dispatch_kernel(bs_ref, cnt_ref, dest_ref, h_ref, xs_ref, zeros, sem, zsem):
    tm = h_ref.shape[0] // RT
    blk = BM * RT
    nb = xs_ref.shape[0] // blk

    @pl.when(pl.program_id(0) == 0)
    def _():
        zeros[...] = jnp.zeros_like(zeros)

        def zcopy(g):
            return pltpu.make_async_copy(zeros, xs_ref.at[pl.ds(pl.multiple_of(g * blk, blk), blk), :], zsem)

        def has_pad(e):
            return cnt_ref[e] % BM != 0

        def fill(e, _):
            @pl.when(has_pad(e))
            def _():
                zcopy(bs_ref[e + 1] - 1).start()
            return 0

        def drain(e, _):
            @pl.when(has_pad(e))
            def _():
                zcopy(0).wait()
            return 0

        lax.fori_loop(0, N_EXPERTS, fill, 0)
        lax.fori_loop(bs_ref[N_EXPERTS], nb, lambda g, _: (zcopy(g).start(), 0)[1], 0)
        lax.fori_loop(0, N_EXPERTS, drain, 0)
        lax.fori_loop(bs_ref[N_EXPERTS], nb, lambda g, _: (zcopy(0).wait(), 0)[1], 0)

    def tile(ref, r):
        return ref.at[pl.ds(pl.multiple_of(r * RT, RT), RT), :]

    def issue(t, _):
        for k in range(TOP_K):
            pltpu.make_async_copy(tile(h_ref, t), tile(xs_ref, dest_ref[k, t]), sem).start(priority=k % 2)
        return 0

    lax.fori_loop(0, tm, issue, 0)
    for _ in range(TOP_K):
        pltpu.make_async_copy(h_ref, xs_ref.at[pl.ds(0, tm * RT), :], sem).wait()


def _dispatch(h2rt, dest, bstart, cnt, P):
    T = h2rt.shape[0] // RT
    tm = min(TM_DISP, T)
    nt = T // tm
    dest3 = dest.reshape(TOP_K, nt, tm).transpose(1, 0, 2)
    return pl.pallas_call(
        _dispatch_kernel,
        out_shape=jax.ShapeDtypeStruct((P * RT, LANE), I32),
        grid_spec=pltpu.PrefetchScalarGridSpec(
            num_scalar_prefetch=2,
            grid=(nt,),
            in_specs=[pl.BlockSpec((None, TOP_K, tm), lambda i, bs, cn: (i, 0, 0), memory_space=pltpu.SMEM),
                      pl.BlockSpec((tm * RT, LANE), lambda i, bs, cn: (i, 0))],
            out_specs=pl.BlockSpec(memory_space=pl.ANY),
            scratch_shapes=[pltpu.VMEM((BM * RT, LANE), I32), pltpu.SemaphoreType.DMA(()),
                            pltpu.SemaphoreType.DMA(())],
        ),
        compiler_params=_cparams("arbitrary"),
        name="dispatch",
    )(bstart, cnt, dest3, h2rt)


NBUF_X = 16
X_AHEAD = NBUF_X - 2
NBUF_Y = 8
EPS = 2


def _experts_kernel(bs_ref, wg_ref, wu_ref, wd_ref, xs_ref, ys_ref, wgb, wub, wdb, xbuf, ybuf, xsem, ysem):
    s = pl.program_id(0)
    blk = BM * RT
    nb = xs_ref.shape[0] // blk
    nused = bs_ref[N_EXPERTS]

    def rows(g):
        return pl.ds(pl.multiple_of(g * blk, blk), blk)

    def x_copy(g, slot):
        return pltpu.make_async_copy(xs_ref.at[rows(g), :], xbuf.at[slot], xsem.at[slot])

    def y_copy(g, slot):
        return pltpu.make_async_copy(ybuf.at[slot], ys_ref.at[rows(g), :], ysem.at[slot])

    @pl.when(s == 0)
    def _():
        for j in range(X_AHEAD):
            @pl.when(j < nused)
            def _():
                x_copy(j, j).start()

    def step(gs):
        for g in gs:
            x_copy(g, g % NBUF_X).wait()
            nxt = g + X_AHEAD

            @pl.when(nxt < nused)
            def _():
                x_copy(nxt, nxt % NBUF_X).start()

        x = _unpack_rows(jnp.concatenate([_rt_load(xbuf, BM, g % NBUF_X) for g in gs], axis=0), BF16)
        gate = _dot(x, wgb[...])
        up = _dot(x, wub[...])
        a = (gate * _sigmoid(gate)) * up
        y = _dot(a.astype(BF16), wdb[...])
        for j, g in enumerate(gs):
            ys = g % NBUF_Y

            @pl.when(g >= NBUF_Y)
            def _():
                y_copy(g - NBUF_Y, ys).wait()

            _rt_store(ybuf, _pack_rows(y[j * BM:(j + 1) * BM]), ys)
            y_copy(g, ys).start()

    def expert(sub):
        e = s * EPS + sub
        g0 = bs_ref[e]
        g1 = bs_ref[e + 1]

        @pl.when(g1 > g0)
        def _():
            wgb[...] = wg_ref[sub].astype(BF16)
            wub[...] = wu_ref[sub].astype(BF16)
            wdb[...] = wd_ref[sub].astype(BF16)

            def pair(i, _):
                g = g0 + 2 * i
                step([g, g + 1])
                return 0

            lax.fori_loop(0, (g1 - g0) // 2, pair, 0)

            @pl.when((g1 - g0) % 2 == 1)
            def _():
                step([g1 - 1])

    for sub in range(EPS):
        expert(sub)

    @pl.when(s == pl.num_programs(0) - 1)
    def _():
        for j in range(NBUF_Y):
            @pl.when(nused - 1 - j >= 0)
            def _():
                y_copy(0, (nused - 1 - j) % NBUF_Y).wait()
        ybuf[0] = jnp.zeros(ybuf.shape[1:], I32)
        lax.fori_loop(nused, nb, lambda g, _: (y_copy(g, 0).start(), 0)[1], 0)
        lax.fori_loop(nused, nb, lambda g, _: (y_copy(0, 0).wait(), 0)[1], 0)


def _experts(xs, bstart, w_gate, w_up, w_down):
    E, D, F = w_gate.shape
    return pl.pallas_call(
        _experts_kernel,
        out_shape=jax.ShapeDtypeStruct(xs.shape, I32),
        grid_spec=pltpu.PrefetchScalarGridSpec(
            num_scalar_prefetch=1,
            grid=(E // EPS,),
            in_specs=[pl.BlockSpec((EPS, D, F), lambda s, bs: (s, 0, 0)),
                      pl.BlockSpec((EPS, D, F), lambda s, bs: (s, 0, 0)),
                      pl.BlockSpec((EPS, F, D), lambda s, bs: (s, 0, 0)),
                      pl.BlockSpec(memory_space=pl.ANY)],
            out_specs=pl.BlockSpec(memory_space=pl.ANY),
            scratch_shapes=[pltpu.VMEM((D, F), BF16), pltpu.VMEM((D, F), BF16), pltpu.VMEM((F, D), BF16),
                            pltpu.VMEM((NBUF_X, BM * RT, LANE), I32), pltpu.VMEM((NBUF_Y, BM * RT, LANE), I32),
                            pltpu.SemaphoreType.DMA((NBUF_X,)), pltpu.SemaphoreType.DMA((NBUF_Y,))],
        ),
        compiler_params=_cparams("arbitrary"),
        name="experts",
    )(bstart, w_gate, w_up, w_down, xs)


def _combine_kernel(dcur_ref, dnxt_ref, w_ref, x1_ref, h_ref, mod_ref, pg_ref, sg_ref, su_ref, sd_ref,
                    ys_ref, o_ref, rows_a, rows_b, sem):
    i = pl.program_id(0)
    n = pl.num_programs(0)
    tm = x1_ref.shape[0]

    def tile(r):
        return pl.ds(pl.multiple_of(r * RT, RT), RT)

    def request(dref, buf, s, t):
        for k in range(TOP_K):
            pltpu.make_async_copy(ys_ref.at[tile(dref[k, t]), :], buf.at[k, tile(t), :],
                                  sem.at[s]).start(priority=k % 2)

    def drain(buf, s):
        for k in range(TOP_K):
            pltpu.make_async_copy(ys_ref.at[pl.ds(0, tm * RT), :], buf.at[k], sem.at[s]).wait()

    @pl.when(i == 0)
    def _():
        lax.fori_loop(0, tm, lambda t, _: (request(dcur_ref, rows_a, 0, t), 0)[1], 0)

    def step(cur, s_cur, nxt, s_nxt):
        drain(cur, s_cur)
        for t in range(tm):
            request(dnxt_ref, nxt, s_nxt, t)
        w = w_ref[...]
        routed = w[:, 0:1] * _unpack_rows(_rt_load(cur, tm, 0), F32)
        for k in range(1, TOP_K):
            routed = routed + w[:, k:k + 1] * _unpack_rows(_rt_load(cur, tm, k), F32)
        hb = _unpack_rows(_rt_load(h_ref, tm), BF16)
        g = _dot(hb, sg_ref[...])
        u = _dot(hb, su_ref[...])
        shared = _dot(((g * _sigmoid(g)) * u).astype(BF16), sd_ref[...])
        y = routed + shared
        g_f = mod_ref[5:6, :]
        o_ref[...] = x1_ref[...] + g_f * (_rms(y, NORM_EPS) * pg_ref[...])

    even = i % 2 == 0

    @pl.when(even)
    def _():
        step(rows_a, 0, rows_b, 1)

    @pl.when(jnp.logical_not(even))
    def _():
        step(rows_b, 1, rows_a, 0)

    @pl.when(i == n - 1)
    def _():
        @pl.when(even)
        def _():
            drain(rows_b, 1)

        @pl.when(jnp.logical_not(even))
        def _():
            drain(rows_a, 0)


def _combine(dest, wts, x1, h2, mod3, ffn_post_g, sw_gate, sw_up, sw_down, ys, S):
    T, D = x1.shape
    tm = min(TM_COMB, S)
    nt = T // tm
    tpb = S // tm
    dest3 = dest.reshape(TOP_K, nt, tm).transpose(1, 0, 2)
    w_tk = wts.T
    full = lambda arr: pl.BlockSpec(arr.shape, lambda i: (0,) * arr.ndim)
    row = lambda w: pl.BlockSpec((tm, w), lambda i: (i, 0))
    sg, su, sd = sw_gate.astype(BF16), sw_up.astype(BF16), sw_down.astype(BF16)
    pg = ffn_post_g.reshape(1, D)
    return pl.pallas_call(
        _combine_kernel,
        out_shape=jax.ShapeDtypeStruct((T, D), F32),
        grid=(nt,),
        in_specs=[pl.BlockSpec((None, TOP_K, tm), lambda i: (i, 0, 0), memory_space=pltpu.SMEM),
                  pl.BlockSpec((None, TOP_K, tm), lambda i: (jnp.minimum(i + 1, nt - 1), 0, 0),
                               memory_space=pltpu.SMEM),
                  row(TOP_K), row(D), pl.BlockSpec((tm * RT, LANE), lambda i: (i, 0)),
                  pl.BlockSpec((None, 6, D), lambda i: (i // tpb, 0, 0)),
                  full(pg), full(sg), full(su), full(sd),
                  pl.BlockSpec(memory_space=pl.ANY)],
        out_specs=row(D),
        scratch_shapes=[pltpu.VMEM((TOP_K, tm * RT, LANE), I32), pltpu.VMEM((TOP_K, tm * RT, LANE), I32),
                        pltpu.SemaphoreType.DMA((2,))],
        compiler_params=_cparams("arbitrary"),
        name="combine",
    )(dest3, dest3, w_tk, x1, h2, mod3, pg, sg, su, sd, ys)


def _moe(h2, x1, idx, wts, rank, counts, mod3, ffn_post_g, exp_w_gate, exp_w_up, exp_w_down,
         sw_gate, sw_up, sw_down, S):
    T, D = x1.shape
    A = T * TOP_K
    P = A + N_EXPERTS * BM
    cnt = counts.reshape(N_EXPERTS)
    blocks = (cnt + BM - 1) // BM
    bstart = jnp.concatenate([jnp.zeros((1,), I32), jnp.cumsum(blocks).astype(I32)])
    pstart = bstart[:-1] * BM

    dest = _dest(idx, rank, pstart)
    xs = _dispatch(h2, dest, bstart, cnt, P)
    ys = _experts(xs, bstart, exp_w_gate, exp_w_up, exp_w_down)
    return _combine(dest, wts, x1, h2, mod3, ffn_post_g, sw_gate, sw_up, sw_down, ys, S)


def _layer(x, c, positions, lambda_init, w_ada, b_ada, attn_pre_g, attn_post_g, w_in, q_norm_g, kv_norm_g,
           w_uq, w_ukv, lam_q1, lam_k1, lam_q2, lam_k2, diff_subln_g, w_o, ffn_pre_g, ffn_post_g,
           router_w, router_b, exp_w_gate, exp_w_up, exp_w_down, sw_gate, sw_up, sw_down):
    B, S, D = x.shape
    T = B * S
    x2 = x.reshape(T, D)
    posf = positions.astype(F32)
    pos_col = posf.reshape(T, 1)
    pos_row = posf.reshape(B, 1, S)

    mod3 = _ada(c, w_ada, b_ada).reshape(B, 6, D)
    q, k, v, dq, dk, dv = _pre_attn(x2, mod3, pos_col, attn_pre_g, w_in, q_norm_g, kv_norm_g, w_uq, w_ukv, S)
    mla = _mla_attn(q, k, v, B, S)
    lam4 = jnp.stack([lam_q1, lam_k1, lam_q2, lam_k2])
    dif = _diff_attn(dq, dk, dv, pos_col, pos_row, lam4, diff_subln_g, lambda_init, B, S)
    x1, h2, idx, wts, rank, counts = _post_attn(x2, mla, dif, mod3, w_o, attn_post_g, ffn_pre_g,
                                                router_w, router_b, S)
    out = _moe(h2, x1, idx, wts, rank, counts, mod3, ffn_post_g, exp_w_gate, exp_w_up, exp_w_down,
               sw_gate, sw_up, sw_down, S)
    return out.reshape(B, S, D)


def kernel(x, c, positions, w_ada, b_ada, attn_pre_g, attn_post_g, w_in, q_norm_g, kv_norm_g, w_uq, w_ukv,
           lam_q1, lam_k1, lam_q2, lam_k2, diff_subln_g, w_o, ffn_pre_g, ffn_post_g, router_w, router_b,
           exp_w_gate, exp_w_up, exp_w_down, shared_w_gate, shared_w_up, shared_w_down):
    depth = w_ada.shape[0]
    for l in range(depth):
        lambda_init = 0.8 - 0.6 * math.exp(-0.3 * l)
        x = _layer(x, c, positions, lambda_init, w_ada[l], b_ada[l], attn_pre_g[l], attn_post_g[l], w_in[l],
                   q_norm_g[l], kv_norm_g[l], w_uq[l], w_ukv[l], lam_q1[l], lam_k1[l], lam_q2[l], lam_k2[l],
                   diff_subln_g[l], w_o[l], ffn_pre_g[l], ffn_post_g[l], router_w[l], router_b[l],
                   exp_w_gate[l], exp_w_up[l], exp_w_down[l], shared_w_gate[l], shared_w_up[l],
                   shared_w_down[l])
    return x
```

```python
import functools
import math

import jax
import jax.numpy as jnp
from jax import lax
from jax.experimental import pallas as pl
from jax.experimental.pallas import tpu as pltpu

F32 = jnp.float32
BF16 = jnp.bfloat16
I32 = jnp.int32

MLA_HEADS = 8
MLA_NOPE = 64
MLA_ROPE = 32
MLA_V = 64
MLA_Q_RANK = 256
MLA_KV_RANK = 128
ROPE_BASE = 10000.0
DIFF_HEADS = 4
DIFF_HD = 64
DIFF_V = 128
N_EXPERTS = 256
TOP_K = 8
N_GROUPS = 8
GROUP_SIZE = N_EXPERTS // N_GROUPS
TOPK_GROUPS = 4
ROUTED_SCALE = 2.5
NORM_EPS = 1e-6
SUBLN_EPS = 1e-5
LOG2E = 1.4426950408889634

LANE = 128
HEAD_PAD = 128

TM_PRE = 512
TQ = 512
TK = 512
TM_POST = 512
TM_DISP = 1024
TM_COMB = 128
BM = 128
VMEM_LIMIT = 48 * 1024 * 1024


def _cparams(*sem):
    return pltpu.CompilerParams(dimension_semantics=sem, vmem_limit_bytes=VMEM_LIMIT)


def _rms(x, eps):
    return x * lax.rsqrt(jnp.mean(x * x, axis=-1, keepdims=True) + eps)


def _sigmoid(x):
    return 1.0 / (1.0 + jnp.exp(-x))


def _dot(a, b):
    return jnp.dot(a, b, preferred_element_type=F32)


RT = 4
HI_MASK = -65536


def _pack_rows(x):
    half = x.shape[1] // 2
    lo = lax.bitcast_convert_type(x[:, :half].astype(BF16).astype(F32), I32)
    hi = lax.bitcast_convert_type(x[:, half:].astype(BF16).astype(F32), I32)
    return lax.shift_right_logical(lo, 16) | (hi & HI_MASK)


def _unpack_rows(u, dtype):
    lo = lax.bitcast_convert_type(lax.shift_left(u, 16), F32)
    hi = lax.bitcast_convert_type(u & HI_MASK, F32)
    return jnp.concatenate([lo.astype(dtype), hi.astype(dtype)], axis=1)


def _rt_load(ref, n, *lead):
    return jnp.concatenate([ref[(*lead, pl.ds(j, n, stride=RT), slice(None))] for j in range(RT)], axis=1)


def _rt_store(ref, val, *lead):
    n = val.shape[0]
    for j in range(RT):
        ref[(*lead, pl.ds(j, n, stride=RT), slice(None))] = val[:, j * LANE:(j + 1) * LANE]


def _dot_nt(a, b):
    return lax.dot_general(a, b, (((1,), (1,)), ((), ())), preferred_element_type=F32)


def _ada_kernel(c_ref, w_ref, b_ref, o_ref):
    c = c_ref[...]
    a = c * _sigmoid(c)
    o_ref[...] = jnp.dot(a, w_ref[...], preferred_element_type=F32,
                         precision=lax.Precision.HIGHEST) + b_ref[...]


def _ada(c, w_ada, b_ada):
    B, D = c.shape
    n = w_ada.shape[1]
    return pl.pallas_call(
        _ada_kernel,
        out_shape=jax.ShapeDtypeStruct((B, n), F32),
        grid=(n // D,),
        in_specs=[pl.BlockSpec((B, D), lambda j: (0, 0)),
                  pl.BlockSpec((D, D), lambda j: (0, j)),
                  pl.BlockSpec((1, D), lambda j: (0, j))],
        out_specs=pl.BlockSpec((B, D), lambda j: (0, j)),
        compiler_params=_cparams("arbitrary"),
        name="ada",
    )(c, w_ada, b_ada.reshape(1, n))


_C_CQ = 0
_C_CKV = _C_CQ + MLA_Q_RANK
_C_KRA = _C_CKV + MLA_KV_RANK
_C_KRB = _C_KRA + LANE
_C_DQ = _C_KRB + LANE
_C_DK = _C_DQ + DIFF_HEADS * DIFF_V
_C_DV = _C_DK + DIFF_HEADS * DIFF_V
_C_END = _C_DV + DIFF_HEADS * DIFF_V


def _pre_attn_kernel(x_ref, mod_ref, pos_ref, inv_ref, g_ref, w1_ref, qg_ref, kvg_ref,
                     wqa_ref, wqb_ref, wkn_ref, wv_ref,
                     q_ref, k_ref, v_ref, dq_ref, dk_ref, dv_ref):
    x = x_ref[...]
    sh = mod_ref[0:1, :]
    sc = mod_ref[1:2, :]
    h = _rms(x, NORM_EPS) * g_ref[...]
    h = h * (1.0 + sc) + sh
    p = _dot(h.astype(BF16), w1_ref[...])

    ang = pos_ref[...] * inv_ref[...]
    lane = lax.broadcasted_iota(I32, ang.shape, 1)
    in_rope = (lane >= MLA_NOPE) & (lane < MLA_NOPE + MLA_ROPE)
    cos_r = jnp.where(in_rope, jnp.cos(ang), 0.0)
    sin_r = jnp.where(in_rope, jnp.sin(ang), 0.0)
    cos_q = jnp.where(lane < MLA_NOPE, 1.0, cos_r)

    cqn = (_rms(p[:, _C_CQ:_C_CKV], NORM_EPS) * qg_ref[...]).astype(BF16)
    qa = _dot(cqn, wqa_ref[...])
    qb = _dot(cqn, wqb_ref[...])
    q_scale = LOG2E / math.sqrt(MLA_NOPE + MLA_ROPE)
    cos_t = jnp.concatenate([cos_q] * MLA_HEADS, axis=1)
    sin_t = jnp.concatenate([sin_r] * MLA_HEADS, axis=1)
    q_ref[...] = ((qa * cos_t + qb * sin_t) * q_scale).astype(BF16)

    ckvn = (_rms(p[:, _C_CKV:_C_KRA], NORM_EPS) * kvg_ref[...]).astype(BF16)
    kn = _dot(ckvn, wkn_ref[...])
    kr = p[:, _C_KRA:_C_KRB] * cos_r + p[:, _C_KRB:_C_DQ] * sin_r
    k_ref[...] = (kn + jnp.concatenate([kr] * MLA_HEADS, axis=1)).astype(BF16)
    lane_t = lax.broadcasted_iota(I32, kn.shape, 1)
    ones_col = jnp.where(lane_t % HEAD_PAD == MLA_V, 1.0, 0.0)
    v_ref[...] = (_dot(ckvn, wv_ref[...]) + ones_col).astype(BF16)

    dq_ref[...] = (p[:, _C_DQ:_C_DK] * (LOG2E / math.sqrt(DIFF_HD))).astype(BF16)
    dk_ref[...] = p[:, _C_DK:_C_DV].astype(BF16)
    dv_ref[...] = p[:, _C_DV:_C_END].astype(BF16)


def _pre_attn(x2, mod3, pos_col, attn_pre_g, w_in, q_norm_g, kv_norm_g, w_uq, w_ukv, S):
    T, D = x2.shape
    tm = min(TM_PRE, S)
    tpb = S // tm
    f = lambda a: a.astype(BF16)
    d3 = DIFF_HEADS * DIFF_V
    a = MLA_Q_RANK
    b = a + MLA_KV_RANK
    c = b + MLA_ROPE

    def swap(r):
        hlf = MLA_ROPE // 2
        return jnp.concatenate([-r[..., hlf:], r[..., :hlf]], axis=-1)

    def pad_rope(r):
        return jnp.pad(r, ((0, 0), (MLA_NOPE, LANE - MLA_NOPE - MLA_ROPE)))

    w_kr = w_in[:, b:c]
    w1 = jnp.concatenate([w_in[:, :b], pad_rope(w_kr), pad_rope(swap(w_kr)), w_in[:, c:]], axis=1)
    assert w1.shape[1] == _C_END
    padq = HEAD_PAD - MLA_NOPE - MLA_ROPE
    wqa = jnp.pad(w_uq, ((0, 0), (0, 0), (0, padq))).reshape(MLA_Q_RANK, MLA_HEADS * HEAD_PAD)
    q_rope = w_uq[..., MLA_NOPE:]
    wqb = jnp.pad(swap(q_rope), ((0, 0), (0, 0), (MLA_NOPE, padq))).reshape(MLA_Q_RANK, MLA_HEADS * HEAD_PAD)
    wkn = jnp.pad(w_ukv[..., :MLA_NOPE], ((0, 0), (0, 0), (0, HEAD_PAD - MLA_NOPE))).reshape(
        MLA_KV_RANK, MLA_HEADS * HEAD_PAD)
    wv = jnp.pad(w_ukv[..., MLA_NOPE:], ((0, 0), (0, 0), (0, HEAD_PAD - MLA_V))).reshape(
        MLA_KV_RANK, MLA_HEADS * HEAD_PAD)

    inv = 1.0 / (ROPE_BASE ** (jnp.arange(0, MLA_ROPE, 2, dtype=F32) / MLA_ROPE))
    inv_lane = jnp.pad(jnp.concatenate([inv, inv]), (MLA_NOPE, LANE - MLA_NOPE - MLA_ROPE)).reshape(1, LANE)

    full = lambda arr: pl.BlockSpec(arr.shape, lambda i: (0,) * arr.ndim)
    row = lambda w: pl.BlockSpec((tm, w), lambda i: (i, 0))
    ins = [x2, mod3, pos_col, inv_lane, attn_pre_g.reshape(1, D), f(w1), q_norm_g.reshape(1, -1),
           kv_norm_g.reshape(1, -1), f(wqa), f(wqb), f(wkn), f(wv)]
    in_specs = [row(D), pl.BlockSpec((None, 6, D), lambda i: (i // tpb, 0, 0)), row(1)] + \
               [full(arr) for arr in ins[3:]]
    widths = [MLA_HEADS * HEAD_PAD, MLA_HEADS * HEAD_PAD, MLA_HEADS * HEAD_PAD, d3, d3, d3]
    return pl.pallas_call(
        _pre_attn_kernel,
        out_shape=[jax.ShapeDtypeStruct((T, w), BF16) for w in widths],
        grid=(T // tm,),
        in_specs=in_specs,
        out_specs=[row(w) for w in widths],
        compiler_params=_cparams("parallel"),
        name="pre_attn",
    )(*ins)


def _online_update(s, m, l, acc, v):
    m_new = jnp.maximum(m, jnp.max(s, axis=-1, keepdims=True))
    p = jnp.exp2(s - m_new)
    alpha = jnp.exp2(m - m_new)
    l_new = alpha * l + jnp.sum(p, axis=-1, keepdims=True)
    acc_new = alpha * acc + _dot(p.astype(BF16), v)
    return m_new, l_new, acc_new


MLA_HPS = 4


def _causal_blocks(i, tq, tk):
    nfull = (i * tq) // tk
    diag = [(nfull + j, i * tq - (nfull + j) * tk) for j in range(max(1, tq // tk))]
    return nfull, diag


def _keep(tq, tk, off):
    r = lax.broadcasted_iota(I32, (tq, tk), 0)
    c = lax.broadcasted_iota(I32, (tq, tk), 1)
    return r + off >= c


def _mla_attn_kernel(tk, q_ref, k_ref, v_ref, o_ref):
    i = pl.program_id(2)
    tq = q_ref.shape[0]
    sl = [slice(j * HEAD_PAD, (j + 1) * HEAD_PAD) for j in range(MLA_HPS)]
    qs = [q_ref[:, sl[j]] for j in range(MLA_HPS)]

    def step(kb, carry, off=None):
        r0 = pl.multiple_of(kb * tk, tk)
        out = []
        for j in range(MLA_HPS):
            m, acc = carry[j]
            s = _dot_nt(qs[j], k_ref[pl.ds(r0, tk), sl[j]])
            if off is not None:
                s = jnp.where(_keep(tq, tk, off), s, -jnp.inf)
            m_new = jnp.maximum(m, jnp.max(s, axis=-1, keepdims=True))
            p = jnp.exp2(s - m_new)
            acc = jnp.exp2(m - m_new) * acc + _dot(p.astype(BF16), v_ref[pl.ds(r0, tk), sl[j]])
            out.append((m_new, acc))
        return tuple(out)

    nfull, diag = _causal_blocks(i, tq, tk)
    init = tuple((jnp.full((tq, 1), -jnp.inf, F32), jnp.zeros((tq, HEAD_PAD), F32)) for _ in range(MLA_HPS))
    carry = lax.fori_loop(0, nfull, step, init)
    for kb, off in diag:
        carry = step(kb, carry, off)
    outs = [acc[:, :MLA_V] / acc[:, MLA_V:MLA_V + 1] for _, acc in carry]
    o_ref[...] = jnp.concatenate(outs, axis=1).astype(o_ref.dtype)


def _mla_attn(q, k, v, B, S):
    tq = min(TQ, S)
    tk = min(TK, S)
    q3 = q.reshape(B, S, -1)
    k3 = k.reshape(B, S, -1)
    v3 = v.reshape(B, S, -1)
    w = MLA_HPS * HEAD_PAD
    return pl.pallas_call(
        functools.partial(_mla_attn_kernel, tk),
        out_shape=jax.ShapeDtypeStruct((B, S, MLA_HEADS * MLA_V), BF16),
        grid=(B, MLA_HEADS // MLA_HPS, S // tq),
        in_specs=[pl.BlockSpec((None, tq, w), lambda b, h, i: (b, i, h)),
                  pl.BlockSpec((None, S, w), lambda b, h, i: (b, 0, h)),
                  pl.BlockSpec((None, S, w), lambda b, h, i: (b, 0, h))],
        out_specs=pl.BlockSpec((None, tq, MLA_HPS * MLA_V), lambda b, h, i: (b, i, h)),
        compiler_params=_cparams("parallel", "parallel", "arbitrary"),
        name="mla_attn",
    )(q3, k3, v3)


DIFF_HPS = 2


def _diff_attn_kernel(lambda_init, tk, q_ref, k_ref, v_ref, pc_ref, pr_ref, lam_ref, g_ref, o_ref):
    i = pl.program_id(2)
    tq = q_ref.shape[0]
    sl = [slice(j * DIFF_V, (j + 1) * DIFF_V) for j in range(DIFF_HPS)]
    lane = lax.broadcasted_iota(I32, (tq, DIFF_V), 1)
    qs, nslopes = [], []
    for j in range(DIFF_HPS):
        q = q_ref[:, sl[j]]
        zero = jnp.zeros_like(q)
        qs.append((jnp.where(lane < DIFF_HD, q, zero), jnp.where(lane >= DIFF_HD, q, zero)))
        hv = jnp.full((1, 1), pl.program_id(1) * DIFF_HPS + j, I32).astype(F32)
        nslopes.append(-LOG2E * jnp.exp2(-8.0 * (hv + 1.0) / DIFF_HEADS))
    pq = pc_ref[...]

    def step(kb, carry, off=None):
        r0 = pl.multiple_of(kb * tk, tk)
        dist = jnp.abs(pq - pr_ref[:, pl.ds(r0, tk)])
        keep = None if off is None else _keep(tq, tk, off)
        out = []
        for j in range(DIFF_HPS):
            kblk = k_ref[pl.ds(r0, tk), sl[j]]
            vblk = v_ref[pl.ds(r0, tk), sl[j]]
            bias = nslopes[j] * dist
            for c in range(2):
                s = _dot_nt(qs[j][c], kblk) + bias
                if keep is not None:
                    s = jnp.where(keep, s, -jnp.inf)
                out.append(_online_update(s, *carry[2 * j + c], vblk))
        return tuple(out)

    nfull, diag = _causal_blocks(i, tq, tk)
    init1 = (jnp.full((tq, 1), -jnp.inf, F32), jnp.zeros((tq, 1), F32), jnp.zeros((tq, DIFF_V), F32))
    carry = lax.fori_loop(0, nfull, step, (init1,) * (2 * DIFF_HPS))
    for kb, off in diag:
        carry = step(kb, carry, off)

    lv = lam_ref[...]
    lam = (jnp.exp(jnp.sum(lv[0:1] * lv[1:2], axis=-1, keepdims=True))
           - jnp.exp(jnp.sum(lv[2:3] * lv[3:4], axis=-1, keepdims=True)) + lambda_init)
    outs = []
    for j in range(DIFF_HPS):
        (_, l1, a1), (_, l2, a2) = carry[2 * j], carry[2 * j + 1]
        o = a1 / l1 - lam * (a2 / l2)
        outs.append(_rms(o, SUBLN_EPS) * g_ref[...] * (1.0 - lambda_init))
    o_ref[...] = jnp.concatenate(outs, axis=1).astype(o_ref.dtype)


def _diff_attn(dq, dk, dv, pos_col, pos_row, lam4, subln_g, lambda_init, B, S):
    tq = min(TQ, S)
    tk = min(TK, S)
    nq = S // tq
    q3 = dq.reshape(B, S, -1)
    k3 = dk.reshape(B, S, -1)
    v3 = dv.reshape(B, S, -1)
    w = DIFF_HPS * DIFF_V
    return pl.pallas_call(
        functools.partial(_diff_attn_kernel, lambda_init, tk),
        out_shape=jax.ShapeDtypeStruct((B, S, DIFF_HEADS * DIFF_V), BF16),
        grid=(B, DIFF_HEADS // DIFF_HPS, nq),
        in_specs=[pl.BlockSpec((None, tq, w), lambda b, h, i: (b, i, h)),
                  pl.BlockSpec((None, S, w), lambda b, h, i: (b, 0, h)),
                  pl.BlockSpec((None, S, w), lambda b, h, i: (b, 0, h)),
                  pl.BlockSpec((tq, 1), lambda b, h, i: (b * nq + i, 0)),
                  pl.BlockSpec((None, 1, S), lambda b, h, i: (b, 0, 0)),
                  pl.BlockSpec((4, DIFF_HD), lambda b, h, i: (0, 0)),
                  pl.BlockSpec((1, DIFF_V), lambda b, h, i: (0, 0))],
        out_specs=pl.BlockSpec((None, tq, w), lambda b, h, i: (b, i, h)),
        compiler_params=_cparams("parallel", "parallel", "arbitrary"),
        name="diff_attn",
    )(q3, k3, v3, pos_col, pos_row, lam4, subln_g.reshape(1, DIFF_V))


def _first_argmax(v, io, n, axis):
    m = jnp.max(v, axis=axis, keepdims=True)
    ix = jnp.min(jnp.where(v == m, io, n), axis=axis, keepdims=True)
    return m, ix


def _post_attn_kernel(x_ref, mla_ref, dif_ref, mod_ref, woa_ref, wob_ref, pg_ref, fg_ref,
                      rwt_ref, rb_ref,
                      x1_ref, h2_ref, idx_ref, wts_ref, rank_ref, cnt_ref, run_ref):
    step = pl.program_id(0)
    tm = x_ref.shape[0]
    E, G, GS = N_EXPERTS, N_GROUPS, GROUP_SIZE

    @pl.when(step == 0)
    def _():
        run_ref[...] = jnp.zeros_like(run_ref)

    g_a = mod_ref[2:3, :]
    sh_f = mod_ref[3:4, :]
    sc_f = mod_ref[4:5, :]
    y = _dot(mla_ref[...], woa_ref[...]) + _dot(dif_ref[...], wob_ref[...])
    x1 = x_ref[...] + g_a * (_rms(y, NORM_EPS) * pg_ref[...])
    x1_ref[...] = x1
    h2 = (_rms(x1, NORM_EPS) * fg_ref[...]) * (1.0 + sc_f) + sh_f
    _rt_store(h2_ref, _pack_rows(h2))

    logits = lax.dot_general(rwt_ref[...], h2, (((1,), (1,)), ((), ())),
                             preferred_element_type=F32, precision=lax.Precision.HIGHEST)
    scores = _sigmoid(logits)
    sel = scores + rb_ref[...]

    sio = lax.broadcasted_iota(I32, (GS, tm), 0)
    gs_rows = []
    for g in range(G):
        blk = sel[g * GS:(g + 1) * GS, :]
        m1, i1 = _first_argmax(blk, sio, GS, 0)
        m2 = jnp.max(jnp.where(sio == i1, -jnp.inf, blk), axis=0, keepdims=True)
        gs_rows.append(m1 + m2)
    gs = jnp.concatenate(gs_rows, axis=0)

    gio = lax.broadcasted_iota(I32, (G, tm), 0)
    gkeep = jnp.zeros((G, tm), F32)
    for _ in range(TOPK_GROUPS):
        _, ix = _first_argmax(gs, gio, G, 0)
        pick = gio == ix
        gkeep = jnp.where(pick, 1.0, gkeep)
        gs = jnp.where(pick, -jnp.inf, gs)
    ekeep = jnp.concatenate([jnp.broadcast_to(gkeep[g:g + 1, :], (GS, tm)) for g in range(G)], axis=0)
    cand = jnp.where(ekeep > 0.0, sel, -jnp.inf)

    eio = lax.broadcasted_iota(I32, (E, tm), 0)
    idx_rows, w_rows = [], []
    for _ in range(TOP_K):
        _, ix = _first_argmax(cand, eio, E, 0)
        pick = eio == ix
        w_rows.append(jnp.sum(jnp.where(pick, scores, 0.0), axis=0, keepdims=True))
        cand = jnp.where(pick, -jnp.inf, cand)
        idx_rows.append(ix)
    idx = jnp.concatenate(idx_rows, axis=0)
    w = jnp.concatenate(w_rows, axis=0)
    wts_ref[...] = w / jnp.sum(w, axis=0, keepdims=True) * ROUTED_SCALE
    idx_ref[...] = idx

    onehot = jnp.zeros((E, tm), F32)
    for k in range(TOP_K):
        onehot = onehot + jnp.where(eio == idx_rows[k], 1.0, 0.0)
    tr = lax.broadcasted_iota(I32, (tm, tm), 0)
    tc = lax.broadcasted_iota(I32, (tm, tm), 1)
    before = jnp.where(tr < tc, 1.0, 0.0).astype(BF16)
    prior = _dot(onehot.astype(BF16), before) + run_ref[...]
    rank_rows = [jnp.sum(jnp.where(eio == idx_rows[k], prior, 0.0), axis=0, keepdims=True)
                 for k in range(TOP_K)]
    rank_ref[...] = jnp.concatenate(rank_rows, axis=0).astype(I32)
    run_ref[...] += jnp.sum(onehot, axis=1, keepdims=True)
    cnt_ref[...] = run_ref[...].astype(I32)


def _post_attn(x2, mla, dif, mod3, w_o, attn_post_g, ffn_pre_g, router_w, router_b, S):
    T, D = x2.shape
    tm = min(TM_POST, S)
    tpb = S // tm
    half = MLA_HEADS * MLA_V
    woa = w_o[:half].astype(BF16)
    wob = w_o[half:].astype(BF16)
    rwt = router_w.T
    full = lambda arr: pl.BlockSpec(arr.shape, lambda i: (0,) * arr.ndim)
    row = lambda w: pl.BlockSpec((tm, w), lambda i: (i, 0))
    col = lambda r: pl.BlockSpec((r, tm), lambda i: (0, i))
    ins = [x2, mla.reshape(T, -1), dif.reshape(T, -1), mod3, woa, wob, attn_post_g.reshape(1, D),
           ffn_pre_g.reshape(1, D), rwt, router_b.reshape(N_EXPERTS, 1)]
    in_specs = [row(D), row(half), row(D - half), pl.BlockSpec((None, 6, D), lambda i: (i // tpb, 0, 0))] + \
               [full(arr) for arr in ins[4:]]
    return pl.pallas_call(
        _post_attn_kernel,
        out_shape=[jax.ShapeDtypeStruct((T, D), F32), jax.ShapeDtypeStruct((T * RT, LANE), I32),
                   jax.ShapeDtypeStruct((TOP_K, T), I32), jax.ShapeDtypeStruct((TOP_K, T), F32),
                   jax.ShapeDtypeStruct((TOP_K, T), I32), jax.ShapeDtypeStruct((N_EXPERTS, 1), I32)],
        grid=(T // tm,),
        in_specs=in_specs,
        out_specs=[row(D), pl.BlockSpec((tm * RT, LANE), lambda i: (i, 0)), col(TOP_K), col(TOP_K), col(TOP_K),
                   pl.BlockSpec((N_EXPERTS, 1), lambda i: (0, 0))],
        scratch_shapes=[pltpu.VMEM((N_EXPERTS, 1), F32)],
        compiler_params=_cparams("arbitrary"),
        name="post_attn",
    )(*ins)


def _dest_kernel(idx_ref, rank_ref, ps_ref, o_ref):
    idx = idx_ref[...]
    tm = idx.shape[1]
    eio = lax.broadcasted_iota(I32, (N_EXPERTS, tm), 0)
    ps = ps_ref[...]
    rows = [jnp.sum(jnp.where(eio == idx[k:k + 1, :], ps, 0.0), axis=0, keepdims=True)
            for k in range(TOP_K)]
    o_ref[...] = jnp.concatenate(rows, axis=0).astype(I32) + rank_ref[...]


def _dest(idx, rank, pstart):
    K, T = idx.shape
    tm = min(2048, T)
    col = pl.BlockSpec((K, tm), lambda i: (0, i))
    return pl.pallas_call(
        _dest_kernel,
        out_shape=jax.ShapeDtypeStruct((K, T), I32),
        grid=(T // tm,),
        in_specs=[col, col, pl.BlockSpec((N_EXPERTS, 1), lambda i: (0, 0))],
        out_specs=col,
        compiler_params=_cparams("parallel"),
        name="dest",
    )(idx, rank, pstart.astype(F32).reshape(N_EXPERTS, 1))


def _dispatch_kernel(bs_ref, cnt_ref, dest_ref, h_ref, xs_ref, zeros, sem, zsem):
    tm = h_ref.shape[0] // RT
    blk = BM * RT
    nb = xs_ref.shape[0] // blk

    @pl.when(pl.program_id(0) == 0)
    def _():
        zeros[...] = jnp.zeros_like(zeros)

        def zcopy(g):
            return pltpu.make_async_copy(zeros, xs_ref.at[pl.ds(pl.multiple_of(g * blk, blk), blk), :], zsem)

        def has_pad(e):
            return cnt_ref[e] % BM != 0

        def fill(e, _):
            @pl.when(has_pad(e))
            def _():
                zcopy(bs_ref[e + 1] - 1).start()
            return 0

        def drain(e, _):
            @pl.when(has_pad(e))
            def _():
                zcopy(0).wait()
            return 0

        lax.fori_loop(0, N_EXPERTS, fill, 0)
        lax.fori_loop(bs_ref[N_EXPERTS], nb, lambda g, _: (zcopy(g).start(), 0)[1], 0)
        lax.fori_loop(0, N_EXPERTS, drain, 0)
        lax.fori_loop(bs_ref[N_EXPERTS], nb, lambda g, _: (zcopy(0).wait(), 0)[1], 0)

    def tile(ref, r):
        return ref.at[pl.ds(pl.multiple_of(r * RT, RT), RT), :]

    def issue(t, _):
        for k in range(TOP_K):
            pltpu.make_async_copy(tile(h_ref, t), tile(xs_ref, dest_ref[k, t]), sem).start(priority=k % 2)
        return 0

    lax.fori_loop(0, tm, issue, 0)
    for _ in range(TOP_K):
        pltpu.make_async_copy(h_ref, xs_ref.at[pl.ds(0, tm * RT), :], sem).wait()


def _dispatch(h2rt, dest, bstart, cnt, P):
    T = h2rt.shape[0] // RT
    tm = min(TM_DISP, T)
    nt = T // tm
    dest3 = dest.reshape(TOP_K, nt, tm).transpose(1, 0, 2)
    return pl.pallas_call(
        _dispatch_kernel,
        out_shape=jax.ShapeDtypeStruct((P * RT, LANE), I32),
        grid_spec=pltpu.PrefetchScalarGridSpec(
            num_scalar_prefetch=2,
            grid=(nt,),
            in_specs=[pl.BlockSpec((None, TOP_K, tm), lambda i, bs, cn: (i, 0, 0), memory_space=pltpu.SMEM),
                      pl.BlockSpec((tm * RT, LANE), lambda i, bs, cn: (i, 0))],
            out_specs=pl.BlockSpec(memory_space=pl.ANY),
            scratch_shapes=[pltpu.VMEM((BM * RT, LANE), I32), pltpu.SemaphoreType.DMA(()),
                            pltpu.SemaphoreType.DMA(())],
        ),
        compiler_params=_cparams("arbitrary"),
        name="dispatch",
    )(bstart, cnt, dest3, h2rt)


NBUF_X = 16
X_AHEAD = NBUF_X - 2
NBUF_Y = 8
EPS = 4


def _experts_kernel(bs_ref, wg_ref, wu_ref, wd_ref, xs_ref, ys_ref, wgb, wub, wdb, xbuf, ybuf, xsem, ysem):
    s = pl.program_id(0)
    blk = BM * RT
    nb = xs_ref.shape[0] // blk
    nused = bs_ref[N_EXPERTS]

    def rows(g):
        return pl.ds(pl.multiple_of(g * blk, blk), blk)

    def x_copy(g, slot):
        return pltpu.make_async_copy(xs_ref.at[rows(g), :], xbuf.at[slot], xsem.at[slot])

    def y_copy(g, slot):
        return pltpu.make_async_copy(ybuf.at[slot], ys_ref.at[rows(g), :], ysem.at[slot])

    @pl.when(s == 0)
    def _():
        for j in range(X_AHEAD):
            @pl.when(j < nused)
            def _():
                x_copy(j, j).start()

    def step(gs):
        for g in gs:
            x_copy(g, g % NBUF_X).wait()
            nxt = g + X_AHEAD

            @pl.when(nxt < nused)
            def _():
                x_copy(nxt, nxt % NBUF_X).start()

        x = _unpack_rows(jnp.concatenate([_rt_load(xbuf, BM, g % NBUF_X) for g in gs], axis=0), BF16)
        gate = _dot(x, wgb[...])
        up = _dot(x, wub[...])
        a = (gate * _sigmoid(gate)) * up
        y = _dot(a.astype(BF16), wdb[...])
        for j, g in enumerate(gs):
            ys = g % NBUF_Y

            @pl.when(g >= NBUF_Y)
            def _():
                y_copy(g - NBUF_Y, ys).wait()

            _rt_store(ybuf, _pack_rows(y[j * BM:(j + 1) * BM]), ys)
            y_copy(g, ys).start()

    def expert(sub):
        e = s * EPS + sub
        g0 = bs_ref[e]
        g1 = bs_ref[e + 1]

        @pl.when(g1 > g0)
        def _():
            wgb[...] = wg_ref[sub].astype(BF16)
            wub[...] = wu_ref[sub].astype(BF16)
            wdb[...] = wd_ref[sub].astype(BF16)

            def pair(i, _):
                g = g0 + 2 * i
                step([g, g + 1])
                return 0

            lax.fori_loop(0, (g1 - g0) // 2, pair, 0)

            @pl.when((g1 - g0) % 2 == 1)
            def _():
                step([g1 - 1])

    for sub in range(EPS):
        expert(sub)

    @pl.when(s == pl.num_programs(0) - 1)
    def _():
        for j in range(NBUF_Y):
            @pl.when(nused - 1 - j >= 0)
            def _():
                y_copy(0, (nused - 1 - j) % NBUF_Y).wait()
        ybuf[0] = jnp.zeros(ybuf.shape[1:], I32)
        lax.fori_loop(nused, nb, lambda g, _: (y_copy(g, 0).start(), 0)[1], 0)
        lax.fori_loop(nused, nb, lambda g, _: (y_copy(0, 0).wait(), 0)[1], 0)


def _experts(xs, bstart, w_gate, w_up, w_down):
    E, D, F = w_gate.shape
    return pl.pallas_call(
        _experts_kernel,
        out_shape=jax.ShapeDtypeStruct(xs.shape, I32),
        grid_spec=pltpu.PrefetchScalarGridSpec(
            num_scalar_prefetch=1,
            grid=(E // EPS,),
            in_specs=[pl.BlockSpec((EPS, D, F), lambda s, bs: (s, 0, 0)),
                      pl.BlockSpec((EPS, D, F), lambda s, bs: (s, 0, 0)),
                      pl.BlockSpec((EPS, F, D), lambda s, bs: (s, 0, 0)),
                      pl.BlockSpec(memory_space=pl.ANY)],
            out_specs=pl.BlockSpec(memory_space=pl.ANY),
            scratch_shapes=[pltpu.VMEM((D, F), BF16), pltpu.VMEM((D, F), BF16), pltpu.VMEM((F, D), BF16),
                            pltpu.VMEM((NBUF_X, BM * RT, LANE), I32), pltpu.VMEM((NBUF_Y, BM * RT, LANE), I32),
                            pltpu.SemaphoreType.DMA((NBUF_X,)), pltpu.SemaphoreType.DMA((NBUF_Y,))],
        ),
        compiler_params=_cparams("arbitrary"),
        name="experts",
    )(bstart, w_gate, w_up, w_down, xs)


def _combine_kernel(dcur_ref, dnxt_ref, w_ref, x1_ref, h_ref, mod_ref, pg_ref, sg_ref, su_ref, sd_ref,
                    ys_ref, o_ref, rows_a, rows_b, sem):
    i = pl.program_id(0)
    n = pl.num_programs(0)
    tm = x1_ref.shape[0]

    def tile(r):
        return pl.ds(pl.multiple_of(r * RT, RT), RT)

    def request(dref, buf, s, t):
        for k in range(TOP_K):
            pltpu.make_async_copy(ys_ref.at[tile(dref[k, t]), :], buf.at[k, tile(t), :],
                                  sem.at[s]).start(priority=k % 2)

    def drain(buf, s):
        for k in range(TOP_K):
            pltpu.make_async_copy(ys_ref.at[pl.ds(0, tm * RT), :], buf.at[k], sem.at[s]).wait()

    @pl.when(i == 0)
    def _():
        lax.fori_loop(0, tm, lambda t, _: (request(dcur_ref, rows_a, 0, t), 0)[1], 0)

    def step(cur, s_cur, nxt, s_nxt):
        drain(cur, s_cur)
        for t in range(tm):
            request(dnxt_ref, nxt, s_nxt, t)
        w = w_ref[...]
        routed = w[:, 0:1] * _unpack_rows(_rt_load(cur, tm, 0), F32)
        for k in range(1, TOP_K):
            routed = routed + w[:, k:k + 1] * _unpack_rows(_rt_load(cur, tm, k), F32)
        hb = _unpack_rows(_rt_load(h_ref, tm), BF16)
        g = _dot(hb, sg_ref[...])
        u = _dot(hb, su_ref[...])
        shared = _dot(((g * _sigmoid(g)) * u).astype(BF16), sd_ref[...])
        y = routed + shared
        g_f = mod_ref[5:6, :]
        o_ref[...] = x1_ref[...] + g_f * (_rms(y, NORM_EPS) * pg_ref[...])

    even = i % 2 == 0

    @pl.when(even)
    def _():
        step(rows_a, 0, rows_b, 1)

    @pl.when(jnp.logical_not(even))
    def _():
        step(rows_b, 1, rows_a, 0)

    @pl.when(i == n - 1)
    def _():
        @pl.when(even)
        def _():
            drain(rows_b, 1)

        @pl.when(jnp.logical_not(even))
        def _():
            drain(rows_a, 0)


def _combine(dest, wts, x1, h2, mod3, ffn_post_g, sw_gate, sw_up, sw_down, ys, S):
    T, D = x1.shape
    tm = min(TM_COMB, S)
    nt = T // tm
    tpb = S // tm
    dest3 = dest.reshape(TOP_K, nt, tm).transpose(1, 0, 2)
    w_tk = wts.T
    full = lambda arr: pl.BlockSpec(arr.shape, lambda i: (0,) * arr.ndim)
    row = lambda w: pl.BlockSpec((tm, w), lambda i: (i, 0))
    sg, su, sd = sw_gate.astype(BF16), sw_up.astype(BF16), sw_down.astype(BF16)
    pg = ffn_post_g.reshape(1, D)
    return pl.pallas_call(
        _combine_kernel,
        out_shape=jax.ShapeDtypeStruct((T, D), F32),
        grid=(nt,),
        in_specs=[pl.BlockSpec((None, TOP_K, tm), lambda i: (i, 0, 0), memory_space=pltpu.SMEM),
                  pl.BlockSpec((None, TOP_K, tm), lambda i: (jnp.minimum(i + 1, nt - 1), 0, 0),
                               memory_space=pltpu.SMEM),
                  row(TOP_K), row(D), pl.BlockSpec((tm * RT, LANE), lambda i: (i, 0)),
                  pl.BlockSpec((None, 6, D), lambda i: (i // tpb, 0, 0)),
                  full(pg), full(sg), full(su), full(sd),
                  pl.BlockSpec(memory_space=pl.ANY)],
        out_specs=row(D),
        scratch_shapes=[pltpu.VMEM((TOP_K, tm * RT, LANE), I32), pltpu.VMEM((TOP_K, tm * RT, LANE), I32),
                        pltpu.SemaphoreType.DMA((2,))],
        compiler_params=_cparams("arbitrary"),
        name="combine",
    )(dest3, dest3, w_tk, x1, h2, mod3, pg, sg, su, sd, ys)


def _moe(h2, x1, idx, wts, rank, counts, mod3, ffn_post_g, exp_w_gate, exp_w_up, exp_w_down,
         sw_gate, sw_up, sw_down, S):
    T, D = x1.shape
    A = T * TOP_K
    P = A + N_EXPERTS * BM
    cnt = counts.reshape(N_EXPERTS)
    blocks = (cnt + BM - 1) // BM
    bstart = jnp.concatenate([jnp.zeros((1,), I32), jnp.cumsum(blocks).astype(I32)])
    pstart = bstart[:-1] * BM

    dest = _dest(idx, rank, pstart)
    xs = _dispatch(h2, dest, bstart, cnt, P)
    ys = _experts(xs, bstart, exp_w_gate, exp_w_up, exp_w_down)
    return _combine(dest, wts, x1, h2, mod3, ffn_post_g, sw_gate, sw_up, sw_down, ys, S)


def _layer(x, c, positions, lambda_init, w_ada, b_ada, attn_pre_g, attn_post_g, w_in, q_norm_g, kv_norm_g,
           w_uq, w_ukv, lam_q1, lam_k1, lam_q2, lam_k2, diff_subln_g, w_o, ffn_pre_g, ffn_post_g,
           router_w, router_b, exp_w_gate, exp_w_up, exp_w_down, sw_gate, sw_up, sw_down):
    B, S, D = x.shape
    T = B * S
    x2 = x.reshape(T, D)
    posf = positions.astype(F32)
    pos_col = posf.reshape(T, 1)
    pos_row = posf.reshape(B, 1, S)

    mod3 = _ada(c, w_ada, b_ada).reshape(B, 6, D)
    q, k, v, dq, dk, dv = _pre_attn(x2, mod3, pos_col, attn_pre_g, w_in, q_norm_g, kv_norm_g, w_uq, w_ukv, S)
    mla = _mla_attn(q, k, v, B, S)
    lam4 = jnp.stack([lam_q1, lam_k1, lam_q2, lam_k2])
    dif = _diff_attn(dq, dk, dv, pos_col, pos_row, lam4, diff_subln_g, lambda_init, B, S)
    x1, h2, idx, wts, rank, counts = _post_attn(x2, mla, dif, mod3, w_o, attn_post_g, ffn_pre_g,
                                                router_w, router_b, S)
    out = _moe(h2, x1, idx, wts, rank, counts, mod3, ffn_post_g, exp_w_gate, exp_w_up, exp_w_down,
               sw_gate, sw_up, sw_down, S)
    return out.reshape(B, S, D)


def kernel(x, c, positions, w_ada, b_ada, attn_pre_g, attn_post_g, w_in, q_norm_g, kv_norm_g, w_uq, w_ukv,
           lam_q1, lam_k1, lam_q2, lam_k2, diff_subln_g, w_o, ffn_pre_g, ffn_post_g, router_w, router_b,
           exp_w_gate, exp_w_up, exp_w_down, shared_w_gate, shared_w_up, shared_w_down):
    depth = w_ada.shape[0]
    for l in range(depth):
        lambda_init = 0.8 - 0.6 * math.exp(-0.3 * l)
        x = _layer(x, c, positions, lambda_init, w_ada[l], b_ada[l], attn_pre_g[l], attn_post_g[l], w_in[l],
                   q_norm_g[l], kv_norm_g[l], w_uq[l], w_ukv[l], lam_q1[l], lam_k1[l], lam_q2[l], lam_k2[l],
                   diff_subln_g[l], w_o[l], ffn_pre_g[l], ffn_post_g[l], router_w[l], router_b[l],
                   exp_w_gate[l], exp_w_up[l], exp_w_down[l], shared_w_gate[l], shared_w_up[l],
                   shared_w_down[l])
    return x
```

```python
import functools
import math

import jax
import jax.numpy as jnp
from jax import lax
from jax.experimental import pallas as pl
from jax.experimental.pallas import tpu as pltpu

F32 = jnp.float32
BF16 = jnp.bfloat16
I32 = jnp.int32

MLA_HEADS = 8
MLA_NOPE = 64
MLA_ROPE = 32
MLA_V = 64
MLA_Q_RANK = 256
MLA_KV_RANK = 128
ROPE_BASE = 10000.0
DIFF_HEADS = 4
DIFF_HD = 64
DIFF_V = 128
N_EXPERTS = 256
TOP_K = 8
N_GROUPS = 8
GROUP_SIZE = N_EXPERTS // N_GROUPS
TOPK_GROUPS = 4
ROUTED_SCALE = 2.5
NORM_EPS = 1e-6
SUBLN_EPS = 1e-5
LOG2E = 1.4426950408889634

LANE = 128
HEAD_PAD = 128

TM_PRE = 512
TQ = 512
TK = 512
TM_POST = 512
TM_DISP = 1024
TM_COMB = 128
BM = 128
VMEM_LIMIT = 48 * 1024 * 1024


def _cparams(*sem):
    return pltpu.CompilerParams(dimension_semantics=sem, vmem_limit_bytes=VMEM_LIMIT)


def _rms(x, eps):
    return x * lax.rsqrt(jnp.mean(x * x, axis=-1, keepdims=True) + eps)


def _sigmoid(x):
    return 1.0 / (1.0 + jnp.exp(-x))


def _dot(a, b):
    return jnp.dot(a, b, preferred_element_type=F32)


RT = 4
HI_MASK = -65536


def _pack_rows(x):
    half = x.shape[1] // 2
    lo = lax.bitcast_convert_type(x[:, :half].astype(BF16).astype(F32), I32)
    hi = lax.bitcast_convert_type(x[:, half:].astype(BF16).astype(F32), I32)
    return lax.shift_right_logical(lo, 16) | (hi & HI_MASK)


def _unpack_rows(u, dtype):
    lo = lax.bitcast_convert_type(lax.shift_left(u, 16), F32)
    hi = lax.bitcast_convert_type(u & HI_MASK, F32)
    return jnp.concatenate([lo.astype(dtype), hi.astype(dtype)], axis=1)


def _rt_load(ref, n, *lead):
    return jnp.concatenate([ref[(*lead, pl.ds(j, n, stride=RT), slice(None))] for j in range(RT)], axis=1)


def _rt_store(ref, val, *lead):
    n = val.shape[0]
    for j in range(RT):
        ref[(*lead, pl.ds(j, n, stride=RT), slice(None))] = val[:, j * LANE:(j + 1) * LANE]


def _dot_nt(a, b):
    return lax.dot_general(a, b, (((1,), (1,)), ((), ())), preferred_element_type=F32)


def _ada_kernel(c_ref, w_ref, b_ref, o_ref):
    c = c_ref[...]
    a = c * _sigmoid(c)
    o_ref[...] = jnp.dot(a, w_ref[...], preferred_element_type=F32,
                         precision=lax.Precision.HIGHEST) + b_ref[...]


def _ada(c, w_ada, b_ada):
    B, D = c.shape
    n = w_ada.shape[1]
    return pl.pallas_call(
        _ada_kernel,
        out_shape=jax.ShapeDtypeStruct((B, n), F32),
        grid=(n // D,),
        in_specs=[pl.BlockSpec((B, D), lambda j: (0, 0)),
                  pl.BlockSpec((D, D), lambda j: (0, j)),
                  pl.BlockSpec((1, D), lambda j: (0, j))],
        out_specs=pl.BlockSpec((B, D), lambda j: (0, j)),
        compiler_params=_cparams("arbitrary"),
        name="ada",
    )(c, w_ada, b_ada.reshape(1, n))


_C_CQ = 0
_C_CKV = _C_CQ + MLA_Q_RANK
_C_KRA = _C_CKV + MLA_KV_RANK
_C_KRB = _C_KRA + LANE
_C_DQ = _C_KRB + LANE
_C_DK = _C_DQ + DIFF_HEADS * DIFF_V
_C_DV = _C_DK + DIFF_HEADS * DIFF_V
_C_END = _C_DV + DIFF_HEADS * DIFF_V


def _pre_attn_kernel(x_ref, mod_ref, pos_ref, inv_ref, g_ref, w1_ref, qg_ref, kvg_ref,
                     wqa_ref, wqb_ref, wkn_ref, wv_ref,
                     q_ref, k_ref, v_ref, dq_ref, dk_ref, dv_ref):
    x = x_ref[...]
    sh = mod_ref[0:1, :]
    sc = mod_ref[1:2, :]
    h = _rms(x, NORM_EPS) * g_ref[...]
    h = h * (1.0 + sc) + sh
    p = _dot(h.astype(BF16), w1_ref[...])

    ang = pos_ref[...] * inv_ref[...]
    lane = lax.broadcasted_iota(I32, ang.shape, 1)
    in_rope = (lane >= MLA_NOPE) & (lane < MLA_NOPE + MLA_ROPE)
    cos_r = jnp.where(in_rope, jnp.cos(ang), 0.0)
    sin_r = jnp.where(in_rope, jnp.sin(ang), 0.0)
    cos_q = jnp.where(lane < MLA_NOPE, 1.0, cos_r)

    cqn = (_rms(p[:, _C_CQ:_C_CKV], NORM_EPS) * qg_ref[...]).astype(BF16)
    qa = _dot(cqn, wqa_ref[...])
    qb = _dot(cqn, wqb_ref[...])
    q_scale = LOG2E / math.sqrt(MLA_NOPE + MLA_ROPE)
    cos_t = jnp.concatenate([cos_q] * MLA_HEADS, axis=1)
    sin_t = jnp.concatenate([sin_r] * MLA_HEADS, axis=1)
    q_ref[...] = ((qa * cos_t + qb * sin_t) * q_scale).astype(BF16)

    ckvn = (_rms(p[:, _C_CKV:_C_KRA], NORM_EPS) * kvg_ref[...]).astype(BF16)
    kn = _dot(ckvn, wkn_ref[...])
    kr = p[:, _C_KRA:_C_KRB] * cos_r + p[:, _C_KRB:_C_DQ] * sin_r
    k_ref[...] = (kn + jnp.concatenate([kr] * MLA_HEADS, axis=1)).astype(BF16)
    lane_t = lax.broadcasted_iota(I32, kn.shape, 1)
    ones_col = jnp.where(lane_t % HEAD_PAD == MLA_V, 1.0, 0.0)
    v_ref[...] = (_dot(ckvn, wv_ref[...]) + ones_col).astype(BF16)

    dq_ref[...] = (p[:, _C_DQ:_C_DK] * (LOG2E / math.sqrt(DIFF_HD))).astype(BF16)
    dk_ref[...] = p[:, _C_DK:_C_DV].astype(BF16)
    dv_ref[...] = p[:, _C_DV:_C_END].astype(BF16)


def _pre_attn(x2, mod3, pos_col, attn_pre_g, w_in, q_norm_g, kv_norm_g, w_uq, w_ukv, S):
    T, D = x2.shape
    tm = min(TM_PRE, S)
    tpb = S // tm
    f = lambda a: a.astype(BF16)
    d3 = DIFF_HEADS * DIFF_V
    a = MLA_Q_RANK
    b = a + MLA_KV_RANK
    c = b + MLA_ROPE

    def swap(r):
        hlf = MLA_ROPE // 2
        return jnp.concatenate([-r[..., hlf:], r[..., :hlf]], axis=-1)

    def pad_rope(r):
        return jnp.pad(r, ((0, 0), (MLA_NOPE, LANE - MLA_NOPE - MLA_ROPE)))

    w_kr = w_in[:, b:c]
    w1 = jnp.concatenate([w_in[:, :b], pad_rope(w_kr), pad_rope(swap(w_kr)), w_in[:, c:]], axis=1)
    assert w1.shape[1] == _C_END
    padq = HEAD_PAD - MLA_NOPE - MLA_ROPE
    wqa = jnp.pad(w_uq, ((0, 0), (0, 0), (0, padq))).reshape(MLA_Q_RANK, MLA_HEADS * HEAD_PAD)
    q_rope = w_uq[..., MLA_NOPE:]
    wqb = jnp.pad(swap(q_rope), ((0, 0), (0, 0), (MLA_NOPE, padq))).reshape(MLA_Q_RANK, MLA_HEADS * HEAD_PAD)
    wkn = jnp.pad(w_ukv[..., :MLA_NOPE], ((0, 0), (0, 0), (0, HEAD_PAD - MLA_NOPE))).reshape(
        MLA_KV_RANK, MLA_HEADS * HEAD_PAD)
    wv = jnp.pad(w_ukv[..., MLA_NOPE:], ((0, 0), (0, 0), (0, HEAD_PAD - MLA_V))).reshape(
        MLA_KV_RANK, MLA_HEADS * HEAD_PAD)

    inv = 1.0 / (ROPE_BASE ** (jnp.arange(0, MLA_ROPE, 2, dtype=F32) / MLA_ROPE))
    inv_lane = jnp.pad(jnp.concatenate([inv, inv]), (MLA_NOPE, LANE - MLA_NOPE - MLA_ROPE)).reshape(1, LANE)

    full = lambda arr: pl.BlockSpec(arr.shape, lambda i: (0,) * arr.ndim)
    row = lambda w: pl.BlockSpec((tm, w), lambda i: (i, 0))
    ins = [x2, mod3, pos_col, inv_lane, attn_pre_g.reshape(1, D), f(w1), q_norm_g.reshape(1, -1),
           kv_norm_g.reshape(1, -1), f(wqa), f(wqb), f(wkn), f(wv)]
    in_specs = [row(D), pl.BlockSpec((None, 6, D), lambda i: (i // tpb, 0, 0)), row(1)] + \
               [full(arr) for arr in ins[3:]]
    widths = [MLA_HEADS * HEAD_PAD, MLA_HEADS * HEAD_PAD, MLA_HEADS * HEAD_PAD, d3, d3, d3]
    return pl.pallas_call(
        _pre_attn_kernel,
        out_shape=[jax.ShapeDtypeStruct((T, w), BF16) for w in widths],
        grid=(T // tm,),
        in_specs=in_specs,
        out_specs=[row(w) for w in widths],
        compiler_params=_cparams("parallel"),
        name="pre_attn",
    )(*ins)


def _online_update(s, m, l, acc, v):
    m_new = jnp.maximum(m, jnp.max(s, axis=-1, keepdims=True))
    p = jnp.exp2(s - m_new)
    alpha = jnp.exp2(m - m_new)
    l_new = alpha * l + jnp.sum(p, axis=-1, keepdims=True)
    acc_new = alpha * acc + _dot(p.astype(BF16), v)
    return m_new, l_new, acc_new


MLA_HPS = 8


def _causal_blocks(i, tq, tk):
    nfull = (i * tq) // tk
    diag = [(nfull + j, i * tq - (nfull + j) * tk) for j in range(max(1, tq // tk))]
    return nfull, diag


def _keep(tq, tk, off):
    r = lax.broadcasted_iota(I32, (tq, tk), 0)
    c = lax.broadcasted_iota(I32, (tq, tk), 1)
    return r + off >= c


def _mla_attn_kernel(tk, q_ref, k_ref, v_ref, o_ref):
    i = pl.program_id(2)
    tq = q_ref.shape[0]
    sl = [slice(j * HEAD_PAD, (j + 1) * HEAD_PAD) for j in range(MLA_HPS)]
    qs = [q_ref[:, sl[j]] for j in range(MLA_HPS)]

    def step(kb, carry, off=None):
        r0 = pl.multiple_of(kb * tk, tk)
        out = []
        for j in range(MLA_HPS):
            m, acc = carry[j]
            s = _dot_nt(qs[j], k_ref[pl.ds(r0, tk), sl[j]])
            if off is not None:
                s = jnp.where(_keep(tq, tk, off), s, -jnp.inf)
            m_new = jnp.maximum(m, jnp.max(s, axis=-1, keepdims=True))
            p = jnp.exp2(s - m_new)
            acc = jnp.exp2(m - m_new) * acc + _dot(p.astype(BF16), v_ref[pl.ds(r0, tk), sl[j]])
            out.append((m_new, acc))
        return tuple(out)

    nfull, diag = _causal_blocks(i, tq, tk)
    init = tuple((jnp.full((tq, 1), -jnp.inf, F32), jnp.zeros((tq, HEAD_PAD), F32)) for _ in range(MLA_HPS))
    carry = lax.fori_loop(0, nfull, step, init)
    for kb, off in diag:
        carry = step(kb, carry, off)
    outs = [acc[:, :MLA_V] / acc[:, MLA_V:MLA_V + 1] for _, acc in carry]
    o_ref[...] = jnp.concatenate(outs, axis=1).astype(o_ref.dtype)


def _mla_attn(q, k, v, B, S):
    tq = min(TQ, S)
    tk = min(TK, S)
    q3 = q.reshape(B, S, -1)
    k3 = k.reshape(B, S, -1)
    v3 = v.reshape(B, S, -1)
    w = MLA_HPS * HEAD_PAD
    return pl.pallas_call(
        functools.partial(_mla_attn_kernel, tk),
        out_shape=jax.ShapeDtypeStruct((B, S, MLA_HEADS * MLA_V), BF16),
        grid=(B, MLA_HEADS // MLA_HPS, S // tq),
        in_specs=[pl.BlockSpec((None, tq, w), lambda b, h, i: (b, i, h)),
                  pl.BlockSpec((None, S, w), lambda b, h, i: (b, 0, h)),
                  pl.BlockSpec((None, S, w), lambda b, h, i: (b, 0, h))],
        out_specs=pl.BlockSpec((None, tq, MLA_HPS * MLA_V), lambda b, h, i: (b, i, h)),
        compiler_params=_cparams("parallel", "parallel", "arbitrary"),
        name="mla_attn",
    )(q3, k3, v3)


DIFF_HPS = 4


def _diff_attn_kernel(lambda_init, tk, q_ref, k_ref, v_ref, pc_ref, pr_ref, lam_ref, g_ref, o_ref):
    i = pl.program_id(2)
    tq = q_ref.shape[0]
    sl = [slice(j * DIFF_V, (j + 1) * DIFF_V) for j in range(DIFF_HPS)]
    lane = lax.broadcasted_iota(I32, (tq, DIFF_V), 1)
    qs, nslopes = [], []
    for j in range(DIFF_HPS):
        q = q_ref[:, sl[j]]
        zero = jnp.zeros_like(q)
        qs.append((jnp.where(lane < DIFF_HD, q, zero), jnp.where(lane >= DIFF_HD, q, zero)))
        hv = jnp.full((1, 1), pl.program_id(1) * DIFF_HPS + j, I32).astype(F32)
        nslopes.append(-LOG2E * jnp.exp2(-8.0 * (hv + 1.0) / DIFF_HEADS))
    pq = pc_ref[...]

    def step(kb, carry, off=None):
        r0 = pl.multiple_of(kb * tk, tk)
        dist = jnp.abs(pq - pr_ref[:, pl.ds(r0, tk)])
        keep = None if off is None else _keep(tq, tk, off)
        out = []
        for j in range(DIFF_HPS):
            kblk = k_ref[pl.ds(r0, tk), sl[j]]
            vblk = v_ref[pl.ds(r0, tk), sl[j]]
            bias = nslopes[j] * dist
            for c in range(2):
                s = _dot_nt(qs[j][c], kblk) + bias
                if keep is not None:
                    s = jnp.where(keep, s, -jnp.inf)
                out.append(_online_update(s, *carry[2 * j + c], vblk))
        return tuple(out)

    nfull, diag = _causal_blocks(i, tq, tk)
    init1 = (jnp.full((tq, 1), -jnp.inf, F32), jnp.zeros((tq, 1), F32), jnp.zeros((tq, DIFF_V), F32))
    carry = lax.fori_loop(0, nfull, step, (init1,) * (2 * DIFF_HPS))
    for kb, off in diag:
        carry = step(kb, carry, off)

    lv = lam_ref[...]
    lam = (jnp.exp(jnp.sum(lv[0:1] * lv[1:2], axis=-1, keepdims=True))
           - jnp.exp(jnp.sum(lv[2:3] * lv[3:4], axis=-1, keepdims=True)) + lambda_init)
    outs = []
    for j in range(DIFF_HPS):
        (_, l1, a1), (_, l2, a2) = carry[2 * j], carry[2 * j + 1]
        o = a1 / l1 - lam * (a2 / l2)
        outs.append(_rms(o, SUBLN_EPS) * g_ref[...] * (1.0 - lambda_init))
    o_ref[...] = jnp.concatenate(outs, axis=1).astype(o_ref.dtype)


def _diff_attn(dq, dk, dv, pos_col, pos_row, lam4, subln_g, lambda_init, B, S):
    tq = min(TQ, S)
    tk = min(TK, S)
    nq = S // tq
    q3 = dq.reshape(B, S, -1)
    k3 = dk.reshape(B, S, -1)
    v3 = dv.reshape(B, S, -1)
    w = DIFF_HPS * DIFF_V
    return pl.pallas_call(
        functools.partial(_diff_attn_kernel, lambda_init, tk),
        out_shape=jax.ShapeDtypeStruct((B, S, DIFF_HEADS * DIFF_V), BF16),
        grid=(B, DIFF_HEADS // DIFF_HPS, nq),
        in_specs=[pl.BlockSpec((None, tq, w), lambda b, h, i: (b, i, h)),
                  pl.BlockSpec((None, S, w), lambda b, h, i: (b, 0, h)),
                  pl.BlockSpec((None, S, w), lambda b, h, i: (b, 0, h)),
                  pl.BlockSpec((tq, 1), lambda b, h, i: (b * nq + i, 0)),
                  pl.BlockSpec((None, 1, S), lambda b, h, i: (b, 0, 0)),
                  pl.BlockSpec((4, DIFF_HD), lambda b, h, i: (0, 0)),
                  pl.BlockSpec((1, DIFF_V), lambda b, h, i: (0, 0))],
        out_specs=pl.BlockSpec((None, tq, w), lambda b, h, i: (b, i, h)),
        compiler_params=_cparams("parallel", "parallel", "arbitrary"),
        name="diff_attn",
    )(q3, k3, v3, pos_col, pos_row, lam4, subln_g.reshape(1, DIFF_V))


def _first_argmax(v, io, n, axis):
    m = jnp.max(v, axis=axis, keepdims=True)
    ix = jnp.min(jnp.where(v == m, io, n), axis=axis, keepdims=True)
    return m, ix


def _post_attn_kernel(x_ref, mla_ref, dif_ref, mod_ref, woa_ref, wob_ref, pg_ref, fg_ref,
                      rwt_ref, rb_ref,
                      x1_ref, h2_ref, idx_ref, wts_ref, rank_ref, cnt_ref, run_ref):
    step = pl.program_id(0)
    tm = x_ref.shape[0]
    E, G, GS = N_EXPERTS, N_GROUPS, GROUP_SIZE

    @pl.when(step == 0)
    def _():
        run_ref[...] = jnp.zeros_like(run_ref)

    g_a = mod_ref[2:3, :]
    sh_f = mod_ref[3:4, :]
    sc_f = mod_ref[4:5, :]
    y = _dot(mla_ref[...], woa_ref[...]) + _dot(dif_ref[...], wob_ref[...])
    x1 = x_ref[...] + g_a * (_rms(y, NORM_EPS) * pg_ref[...])
    x1_ref[...] = x1
    h2 = (_rms(x1, NORM_EPS) * fg_ref[...]) * (1.0 + sc_f) + sh_f
    _rt_store(h2_ref, _pack_rows(h2))

    logits = lax.dot_general(rwt_ref[...], h2, (((1,), (1,)), ((), ())),
                             preferred_element_type=F32, precision=lax.Precision.HIGHEST)
    scores = _sigmoid(logits)
    sel = scores + rb_ref[...]

    sio = lax.broadcasted_iota(I32, (GS, tm), 0)
    gs_rows = []
    for g in range(G):
        blk = sel[g * GS:(g + 1) * GS, :]
        m1, i1 = _first_argmax(blk, sio, GS, 0)
        m2 = jnp.max(jnp.where(sio == i1, -jnp.inf, blk), axis=0, keepdims=True)
        gs_rows.append(m1 + m2)
    gs = jnp.concatenate(gs_rows, axis=0)

    gio = lax.broadcasted_iota(I32, (G, tm), 0)
    gkeep = jnp.zeros((G, tm), F32)
    for _ in range(TOPK_GROUPS):
        _, ix = _first_argmax(gs, gio, G, 0)
        pick = gio == ix
        gkeep = jnp.where(pick, 1.0, gkeep)
        gs = jnp.where(pick, -jnp.inf, gs)
    ekeep = jnp.concatenate([jnp.broadcast_to(gkeep[g:g + 1, :], (GS, tm)) for g in range(G)], axis=0)
    cand = jnp.where(ekeep > 0.0, sel, -jnp.inf)

    eio = lax.broadcasted_iota(I32, (E, tm), 0)
    idx_rows, w_rows = [], []
    for _ in range(TOP_K):
        _, ix = _first_argmax(cand, eio, E, 0)
        pick = eio == ix
        w_rows.append(jnp.sum(jnp.where(pick, scores, 0.0), axis=0, keepdims=True))
        cand = jnp.where(pick, -jnp.inf, cand)
        idx_rows.append(ix)
    idx = jnp.concatenate(idx_rows, axis=0)
    w = jnp.concatenate(w_rows, axis=0)
    wts_ref[...] = w / jnp.sum(w, axis=0, keepdims=True) * ROUTED_SCALE
    idx_ref[...] = idx

    onehot = jnp.zeros((E, tm), F32)
    for k in range(TOP_K):
        onehot = onehot + jnp.where(eio == idx_rows[k], 1.0, 0.0)
    tr = lax.broadcasted_iota(I32, (tm, tm), 0)
    tc = lax.broadcasted_iota(I32, (tm, tm), 1)
    before = jnp.where(tr < tc, 1.0, 0.0).astype(BF16)
    prior = _dot(onehot.astype(BF16), before) + run_ref[...]
    rank_rows = [jnp.sum(jnp.where(eio == idx_rows[k], prior, 0.0), axis=0, keepdims=True)
                 for k in range(TOP_K)]
    rank_ref[...] = jnp.concatenate(rank_rows, axis=0).astype(I32)
    run_ref[...] += jnp.sum(onehot, axis=1, keepdims=True)
    cnt_ref[...] = run_ref[...].astype(I32)


def _post_attn(x2, mla, dif, mod3, w_o, attn_post_g, ffn_pre_g, router_w, router_b, S):
    T, D = x2.shape
    tm = min(TM_POST, S)
    tpb = S // tm
    half = MLA_HEADS * MLA_V
    woa = w_o[:half].astype(BF16)
    wob = w_o[half:].astype(BF16)
    rwt = router_w.T
    full = lambda arr: pl.BlockSpec(arr.shape, lambda i: (0,) * arr.ndim)
    row = lambda w: pl.BlockSpec((tm, w), lambda i: (i, 0))
    col = lambda r: pl.BlockSpec((r, tm), lambda i: (0, i))
    ins = [x2, mla.reshape(T, -1), dif.reshape(T, -1), mod3, woa, wob, attn_post_g.reshape(1, D),
           ffn_pre_g.reshape(1, D), rwt, router_b.reshape(N_EXPERTS, 1)]
    in_specs = [row(D), row(half), row(D - half), pl.BlockSpec((None, 6, D), lambda i: (i // tpb, 0, 0))] + \
               [full(arr) for arr in ins[4:]]
    return pl.pallas_call(
        _post_attn_kernel,
        out_shape=[jax.ShapeDtypeStruct((T, D), F32), jax.ShapeDtypeStruct((T * RT, LANE), I32),
                   jax.ShapeDtypeStruct((TOP_K, T), I32), jax.ShapeDtypeStruct((TOP_K, T), F32),
                   jax.ShapeDtypeStruct((TOP_K, T), I32), jax.ShapeDtypeStruct((N_EXPERTS, 1), I32)],
        grid=(T // tm,),
        in_specs=in_specs,
        out_specs=[row(D), pl.BlockSpec((tm * RT, LANE), lambda i: (i, 0)), col(TOP_K), col(TOP_K), col(TOP_K),
                   pl.BlockSpec((N_EXPERTS, 1), lambda i: (0, 0))],
        scratch_shapes=[pltpu.VMEM((N_EXPERTS, 1), F32)],
        compiler_params=_cparams("arbitrary"),
        name="post_attn",
    )(*ins)


def _dest_kernel(idx_ref, rank_ref, ps_ref, o_ref):
    idx = idx_ref[...]
    tm = idx.shape[1]
    eio = lax.broadcasted_iota(I32, (N_EXPERTS, tm), 0)
    ps = ps_ref[...]
    rows = [jnp.sum(jnp.where(eio == idx[k:k + 1, :], ps, 0.0), axis=0, keepdims=True)
            for k in range(TOP_K)]
    o_ref[...] = jnp.concatenate(rows, axis=0).astype(I32) + rank_ref[...]


def _dest(idx, rank, pstart):
    K, T = idx.shape
    tm = min(2048, T)
    col = pl.BlockSpec((K, tm), lambda i: (0, i))
    return pl.pallas_call(
        _dest_kernel,
        out_shape=jax.ShapeDtypeStruct((K, T), I32),
        grid=(T // tm,),
        in_specs=[col, col, pl.BlockSpec((N_EXPERTS, 1), lambda i: (0, 0))],
        out_specs=col,
        compiler_params=_cparams("parallel"),
        name="dest",
    )(idx, rank, pstart.astype(F32).reshape(N_EXPERTS, 1))


def _dispatch_kernel(bs_ref, cnt_ref, dest_ref, h_ref, xs_ref, zeros, sem, zsem):
    tm = h_ref.shape[0] // RT
    blk = BM * RT
    nb = xs_ref.shape[0] // blk

    @pl.when(pl.program_id(0) == 0)
    def _():
        zeros[...] = jnp.zeros_like(zeros)

        def zcopy(g):
            return pltpu.make_async_copy(zeros, xs_ref.at[pl.ds(pl.multiple_of(g * blk, blk), blk), :], zsem)

        def has_pad(e):
            return cnt_ref[e] % BM != 0

        def fill(e, _):
            @pl.when(has_pad(e))
            def _():
                zcopy(bs_ref[e + 1] - 1).start()
            return 0

        def drain(e, _):
            @pl.when(has_pad(e))
            def _():
                zcopy(0).wait()
            return 0

        lax.fori_loop(0, N_EXPERTS, fill, 0)
        lax.fori_loop(bs_ref[N_EXPERTS], nb, lambda g, _: (zcopy(g).start(), 0)[1], 0)
        lax.fori_loop(0, N_EXPERTS, drain, 0)
        lax.fori_loop(bs_ref[N_EXPERTS], nb, lambda g, _: (zcopy(0).wait(), 0)[1], 0)

    def tile(ref, r):
        return ref.at[pl.ds(pl.multiple_of(r * RT, RT), RT), :]

    def issue(t, _):
        for k in range(TOP_K):
            pltpu.make_async_copy(tile(h_ref, t), tile(xs_ref, dest_ref[k, t]), sem).start(priority=k % 2)
        return 0

    lax.fori_loop(0, tm, issue, 0)
    for _ in range(TOP_K):
        pltpu.make_async_copy(h_ref, xs_ref.at[pl.ds(0, tm * RT), :], sem).wait()


def _dispatch(h2rt, dest, bstart, cnt, P):
    T = h2rt.shape[0] // RT
    tm = min(TM_DISP, T)
    nt = T // tm
    dest3 = dest.reshape(TOP_K, nt, tm).transpose(1, 0, 2)
    return pl.pallas_call(
        _dispatch_kernel,
        out_shape=jax.ShapeDtypeStruct((P * RT, LANE), I32),
        grid_spec=pltpu.PrefetchScalarGridSpec(
            num_scalar_prefetch=2,
            grid=(nt,),
            in_specs=[pl.BlockSpec((None, TOP_K, tm), lambda i, bs, cn: (i, 0, 0), memory_space=pltpu.SMEM),
                      pl.BlockSpec((tm * RT, LANE), lambda i, bs, cn: (i, 0))],
            out_specs=pl.BlockSpec(memory_space=pl.ANY),
            scratch_shapes=[pltpu.VMEM((BM * RT, LANE), I32), pltpu.SemaphoreType.DMA(()),
                            pltpu.SemaphoreType.DMA(())],
        ),
        compiler_params=_cparams("arbitrary"),
        name="dispatch",
    )(bstart, cnt, dest3, h2rt)


NBUF_X = 16
X_AHEAD = NBUF_X - 2
NBUF_Y = 8
EPS = 2


def _experts_kernel(bs_ref, wg_ref, wu_ref, wd_ref, xs_ref, ys_ref, wgb, wub, wdb, xbuf, ybuf, xsem, ysem):
    s = pl.program_id(0)
    blk = BM * RT
    nb = xs_ref.shape[0] // blk
    nused = bs_ref[N_EXPERTS]

    def rows(g):
        return pl.ds(pl.multiple_of(g * blk, blk), blk)

    def x_copy(g, slot):
        return pltpu.make_async_copy(xs_ref.at[rows(g), :], xbuf.at[slot], xsem.at[slot])

    def y_copy(g, slot):
        return pltpu.make_async_copy(ybuf.at[slot], ys_ref.at[rows(g), :], ysem.at[slot])

    @pl.when(s == 0)
    def _():
        for j in range(X_AHEAD):
            @pl.when(j < nused)
            def _():
                x_copy(j, j).start()

    def step(gs):
        for g in gs:
            x_copy(g, g % NBUF_X).wait()
            nxt = g + X_AHEAD

            @pl.when(nxt < nused)
            def _():
                x_copy(nxt, nxt % NBUF_X).start()

        x = _unpack_rows(jnp.concatenate([_rt_load(xbuf, BM, g % NBUF_X) for g in gs], axis=0), BF16)
        gate = _dot(x, wgb[...])
        up = _dot(x, wub[...])
        a = (gate * _sigmoid(gate)) * up
        y = _dot(a.astype(BF16), wdb[...])
        for j, g in enumerate(gs):
            ys = g % NBUF_Y

            @pl.when(g >= NBUF_Y)
            def _():
                y_copy(g - NBUF_Y, ys).wait()

            _rt_store(ybuf, _pack_rows(y[j * BM:(j + 1) * BM]), ys)
            y_copy(g, ys).start()

    def expert(sub):
        e = s * EPS + sub
        g0 = bs_ref[e]
        g1 = bs_ref[e + 1]

        @pl.when(g1 > g0)
        def _():
            wgb[...] = wg_ref[sub].astype(BF16)
            wub[...] = wu_ref[sub].astype(BF16)
            wdb[...] = wd_ref[sub].astype(BF16)

            def pair(i, _):
                g = g0 + 2 * i
                step([g, g + 1])
                return 0

            lax.fori_loop(0, (g1 - g0) // 2, pair, 0)

            @pl.when((g1 - g0) % 2 == 1)
            def _():
                step([g1 - 1])

    for sub in range(EPS):
        expert(sub)

    @pl.when(s == pl.num_programs(0) - 1)
    def _():
        for j in range(NBUF_Y):
            @pl.when(nused - 1 - j >= 0)
            def _():
                y_copy(0, (nused - 1 - j) % NBUF_Y).wait()
        ybuf[0] = jnp.zeros(ybuf.shape[1:], I32)
        lax.fori_loop(nused, nb, lambda g, _: (y_copy(g, 0).start(), 0)[1], 0)
        lax.fori_loop(nused, nb, lambda g, _: (y_copy(0, 0).wait(), 0)[1], 0)


def _experts(xs, bstart, w_gate, w_up, w_down):
    E, D, F = w_gate.shape
    return pl.pallas_call(
        _experts_kernel,
        out_shape=jax.ShapeDtypeStruct(xs.shape, I32),
        grid_spec=pltpu.PrefetchScalarGridSpec(
            num_scalar_prefetch=1,
            grid=(E // EPS,),
            in_specs=[pl.BlockSpec((EPS, D, F), lambda s, bs: (s, 0, 0)),
                      pl.BlockSpec((EPS, D, F), lambda s, bs: (s, 0, 0)),
                      pl.BlockSpec((EPS, F, D), lambda s, bs: (s, 0, 0)),
                      pl.BlockSpec(memory_space=pl.ANY)],
            out_specs=pl.BlockSpec(memory_space=pl.ANY),
            scratch_shapes=[pltpu.VMEM((D, F), BF16), pltpu.VMEM((D, F), BF16), pltpu.VMEM((F, D), BF16),
                            pltpu.VMEM((NBUF_X, BM * RT, LANE), I32), pltpu.VMEM((NBUF_Y, BM * RT, LANE), I32),
                            pltpu.SemaphoreType.DMA((NBUF_X,)), pltpu.SemaphoreType.DMA((NBUF_Y,))],
        ),
        compiler_params=_cparams("arbitrary"),
        name="experts",
    )(bstart, w_gate, w_up, w_down, xs)


def _combine_kernel(dcur_ref, dnxt_ref, w_ref, x1_ref, h_ref, mod_ref, pg_ref, sg_ref, su_ref, sd_ref,
                    ys_ref, o_ref, rows_a, rows_b, sem):
    i = pl.program_id(0)
    n = pl.num_programs(0)
    tm = x1_ref.shape[0]

    def tile(r):
        return pl.ds(pl.multiple_of(r * RT, RT), RT)

    def request(dref, buf, s, t):
        for k in range(TOP_K):
            pltpu.make_async_copy(ys_ref.at[tile(dref[k, t]), :], buf.at[k, tile(t), :],
                                  sem.at[s]).start(priority=k % 2)

    def drain(buf, s):
        for k in range(TOP_K):
            pltpu.make_async_copy(ys_ref.at[pl.ds(0, tm * RT), :], buf.at[k], sem.at[s]).wait()

    @pl.when(i == 0)
    def _():
        lax.fori_loop(0, tm, lambda t, _: (request(dcur_ref, rows_a, 0, t), 0)[1], 0)

    def step(cur, s_cur, nxt, s_nxt):
        drain(cur, s_cur)
        for t in range(tm):
            request(dnxt_ref, nxt, s_nxt, t)
        w = w_ref[...]
        routed = w[:, 0:1] * _unpack_rows(_rt_load(cur, tm, 0), F32)
        for k in range(1, TOP_K):
            routed = routed + w[:, k:k + 1] * _unpack_rows(_rt_load(cur, tm, k), F32)
        hb = _unpack_rows(_rt_load(h_ref, tm), BF16)
        g = _dot(hb, sg_ref[...])
        u = _dot(hb, su_ref[...])
        shared = _dot(((g * _sigmoid(g)) * u).astype(BF16), sd_ref[...])
        y = routed + shared
        g_f = mod_ref[5:6, :]
        o_ref[...] = x1_ref[...] + g_f * (_rms(y, NORM_EPS) * pg_ref[...])

    even = i % 2 == 0

    @pl.when(even)
    def _():
        step(rows_a, 0, rows_b, 1)

    @pl.when(jnp.logical_not(even))
    def _():
        step(rows_b, 1, rows_a, 0)

    @pl.when(i == n - 1)
    def _():
        @pl.when(even)
        def _():
            drain(rows_b, 1)

        @pl.when(jnp.logical_not(even))
        def _():
            drain(rows_a, 0)


def _combine(dest, wts, x1, h2, mod3, ffn_post_g, sw_gate, sw_up, sw_down, ys, S):
    T, D = x1.shape
    tm = min(TM_COMB, S)
    nt = T // tm
    tpb = S // tm
    dest3 = dest.reshape(TOP_K, nt, tm).transpose(1, 0, 2)
    w_tk = wts.T
    full = lambda arr: pl.BlockSpec(arr.shape, lambda i: (0,) * arr.ndim)
    row = lambda w: pl.BlockSpec((tm, w), lambda i: (i, 0))
    sg, su, sd = sw_gate.astype(BF16), sw_up.astype(BF16), sw_down.astype(BF16)
    pg = ffn_post_g.reshape(1, D)
    return pl.pallas_call(
        _combine_kernel,
        out_shape=jax.ShapeDtypeStruct((T, D), F32),
        grid=(nt,),
        in_specs=[pl.BlockSpec((None, TOP_K, tm), lambda i: (i, 0, 0), memory_space=pltpu.SMEM),
                  pl.BlockSpec((None, TOP_K, tm), lambda i: (jnp.minimum(i + 1, nt - 1), 0, 0),
                               memory_space=pltpu.SMEM),
                  row(TOP_K), row(D), pl.BlockSpec((tm * RT, LANE), lambda i: (i, 0)),
                  pl.BlockSpec((None, 6, D), lambda i: (i // tpb, 0, 0)),
                  full(pg), full(sg), full(su), full(sd),
                  pl.BlockSpec(memory_space=pl.ANY)],
        out_specs=row(D),
        scratch_shapes=[pltpu.VMEM((TOP_K, tm * RT, LANE), I32), pltpu.VMEM((TOP_K, tm * RT, LANE), I32),
                        pltpu.SemaphoreType.DMA((2,))],
        compiler_params=_cparams("arbitrary"),
        name="combine",
    )(dest3, dest3, w_tk, x1, h2, mod3, pg, sg, su, sd, ys)


def _moe(h2, x1, idx, wts, rank, counts, mod3, ffn_post_g, exp_w_gate, exp_w_up, exp_w_down,
         sw_gate, sw_up, sw_down, S):
    T, D = x1.shape
    A = T * TOP_K
    P = A + N_EXPERTS * BM
    cnt = counts.reshape(N_EXPERTS)
    blocks = (cnt + BM - 1) // BM
    bstart = jnp.concatenate([jnp.zeros((1,), I32), jnp.cumsum(blocks).astype(I32)])
    pstart = bstart[:-1] * BM

    dest = _dest(idx, rank, pstart)
    xs = _dispatch(h2, dest, bstart, cnt, P)
    ys = _experts(xs, bstart, exp_w_gate, exp_w_up, exp_w_down)
    return _combine(dest, wts, x1, h2, mod3, ffn_post_g, sw_gate, sw_up, sw_down, ys, S)


def _layer(x, c, positions, lambda_init, w_ada, b_ada, attn_pre_g, attn_post_g, w_in, q_norm_g, kv_norm_g,
           w_uq, w_ukv, lam_q1, lam_k1, lam_q2, lam_k2, diff_subln_g, w_o, ffn_pre_g, ffn_post_g,
           router_w, router_b, exp_w_gate, exp_w_up, exp_w_down, sw_gate, sw_up, sw_down):
    B, S, D = x.shape
    T = B * S
    x2 = x.reshape(T, D)
    posf = positions.astype(F32)
    pos_col = posf.reshape(T, 1)
    pos_row = posf.reshape(B, 1, S)

    mod3 = _ada(c, w_ada, b_ada).reshape(B, 6, D)
    q, k, v, dq, dk, dv = _pre_attn(x2, mod3, pos_col, attn_pre_g, w_in, q_norm_g, kv_norm_g, w_uq, w_ukv, S)
    mla = _mla_attn(q, k, v, B, S)
    lam4 = jnp.stack([lam_q1, lam_k1, lam_q2, lam_k2])
    dif = _diff_attn(dq, dk, dv, pos_col, pos_row, lam4, diff_subln_g, lambda_init, B, S)
    x1, h2, idx, wts, rank, counts = _post_attn(x2, mla, dif, mod3, w_o, attn_post_g, ffn_pre_g,
                                                router_w, router_b, S)
    out = _moe(h2, x1, idx, wts, rank, counts, mod3, ffn_post_g, exp_w_gate, exp_w_up, exp_w_down,
               sw_gate, sw_up, sw_down, S)
    return out.reshape(B, S, D)


def kernel(x, c, positions, w_ada, b_ada, attn_pre_g, attn_post_g, w_in, q_norm_g, kv_norm_g, w_uq, w_ukv,
           lam_q1, lam_k1, lam_q2, lam_k2, diff_subln_g, w_o, ffn_pre_g, ffn_post_g, router_w, router_b,
           exp_w_gate, exp_w_up, exp_w_down, shared_w_gate, shared_w_up, shared_w_down):
    depth = w_ada.shape[0]
    for l in range(depth):
        lambda_init = 0.8 - 0.6 * math.exp(-0.3 * l)
        x = _layer(x, c, positions, lambda_init, w_ada[l], b_ada[l], attn_pre_g[l], attn_post_g[l], w_in[l],
                   q_norm_g[l], kv_norm_g[l], w_uq[l], w_ukv[l], lam_q1[l], lam_k1[l], lam_q2[l], lam_k2[l],
                   diff_subln_g[l], w_o[l], ffn_pre_g[l], ffn_post_g[l], router_w[l], router_b[l],
                   exp_w_gate[l], exp_w_up[l], exp_w_down[l], shared_w_gate[l], shared_w_up[l],
                   shared_w_down[l])
    return x
```

```python
import functools
import math

import jax
import jax.numpy as jnp
from jax import lax
from jax.experimental import pallas as pl
from jax.experimental.pallas import tpu as pltpu

F32 = jnp.float32
BF16 = jnp.bfloat16
I32 = jnp.int32

MLA_HEADS = 8
MLA_NOPE = 64
MLA_ROPE = 32
MLA_V = 64
MLA_Q_RANK = 256
MLA_KV_RANK = 128
ROPE_BASE = 10000.0
DIFF_HEADS = 4
DIFF_HD = 64
DIFF_V = 128
N_EXPERTS = 256
TOP_K = 8
N_GROUPS = 8
GROUP_SIZE = N_EXPERTS // N_GROUPS
TOPK_GROUPS = 4
ROUTED_SCALE = 2.5
NORM_EPS = 1e-6
SUBLN_EPS = 1e-5
LOG2E = 1.4426950408889634

LANE = 128
HEAD_PAD = 128

TM_PRE = 512
TQ = 512
TK = 512
TM_POST = 512
TM_DISP = 2048
TM_COMB = 128
BM = 128
VMEM_LIMIT = 48 * 1024 * 1024


def _cparams(*sem):
    return pltpu.CompilerParams(dimension_semantics=sem, vmem_limit_bytes=VMEM_LIMIT)


def _rms(x, eps):
    return x * lax.rsqrt(jnp.mean(x * x, axis=-1, keepdims=True) + eps)


def _sigmoid(x):
    return 1.0 / (1.0 + jnp.exp(-x))


def _dot(a, b):
    return jnp.dot(a, b, preferred_element_type=F32)


RT = 4
HI_MASK = -65536


def _pack_rows(x):
    half = x.shape[1] // 2
    lo = lax.bitcast_convert_type(x[:, :half].astype(BF16).astype(F32), I32)
    hi = lax.bitcast_convert_type(x[:, half:].astype(BF16).astype(F32), I32)
    return lax.shift_right_logical(lo, 16) | (hi & HI_MASK)


def _unpack_rows(u, dtype):
    lo = lax.bitcast_convert_type(lax.shift_left(u, 16), F32)
    hi = lax.bitcast_convert_type(u & HI_MASK, F32)
    return jnp.concatenate([lo.astype(dtype), hi.astype(dtype)], axis=1)


def _rt_load(ref, n, *lead):
    return jnp.concatenate([ref[(*lead, pl.ds(j, n, stride=RT), slice(None))] for j in range(RT)], axis=1)


def _rt_store(ref, val, *lead):
    n = val.shape[0]
    for j in range(RT):
        ref[(*lead, pl.ds(j, n, stride=RT), slice(None))] = val[:, j * LANE:(j + 1) * LANE]


def _dot_nt(a, b):
    return lax.dot_general(a, b, (((1,), (1,)), ((), ())), preferred_element_type=F32)


def _ada_kernel(c_ref, w_ref, b_ref, o_ref):
    c = c_ref[...]
    a = c * _sigmoid(c)
    o_ref[...] = jnp.dot(a, w_ref[...], preferred_element_type=F32,
                         precision=lax.Precision.HIGHEST) + b_ref[...]


def _ada(c, w_ada, b_ada):
    B, D = c.shape
    n = w_ada.shape[1]
    return pl.pallas_call(
        _ada_kernel,
        out_shape=jax.ShapeDtypeStruct((B, n), F32),
        grid=(n // D,),
        in_specs=[pl.BlockSpec((B, D), lambda j: (0, 0)),
                  pl.BlockSpec((D, D), lambda j: (0, j)),
                  pl.BlockSpec((1, D), lambda j: (0, j))],
        out_specs=pl.BlockSpec((B, D), lambda j: (0, j)),
        compiler_params=_cparams("arbitrary"),
        name="ada",
    )(c, w_ada, b_ada.reshape(1, n))


_C_CQ = 0
_C_CKV = _C_CQ + MLA_Q_RANK
_C_KRA = _C_CKV + MLA_KV_RANK
_C_KRB = _C_KRA + LANE
_C_DQ = _C_KRB + LANE
_C_DK = _C_DQ + DIFF_HEADS * DIFF_V
_C_DV = _C_DK + DIFF_HEADS * DIFF_V
_C_END = _C_DV + DIFF_HEADS * DIFF_V


def _pre_attn_kernel(x_ref, mod_ref, pos_ref, inv_ref, g_ref, w1_ref, qg_ref, kvg_ref,
                     wqa_ref, wqb_ref, wkn_ref, wv_ref,
                     q_ref, k_ref, v_ref, dq_ref, dk_ref, dv_ref):
    x = x_ref[...]
    sh = mod_ref[0:1, :]
    sc = mod_ref[1:2, :]
    h = _rms(x, NORM_EPS) * g_ref[...]
    h = h * (1.0 + sc) + sh
    p = _dot(h.astype(BF16), w1_ref[...])

    ang = pos_ref[...] * inv_ref[...]
    lane = lax.broadcasted_iota(I32, ang.shape, 1)
    in_rope = (lane >= MLA_NOPE) & (lane < MLA_NOPE + MLA_ROPE)
    cos_r = jnp.where(in_rope, jnp.cos(ang), 0.0)
    sin_r = jnp.where(in_rope, jnp.sin(ang), 0.0)
    cos_q = jnp.where(lane < MLA_NOPE, 1.0, cos_r)

    cqn = (_rms(p[:, _C_CQ:_C_CKV], NORM_EPS) * qg_ref[...]).astype(BF16)
    qa = _dot(cqn, wqa_ref[...])
    qb = _dot(cqn, wqb_ref[...])
    q_scale = LOG2E / math.sqrt(MLA_NOPE + MLA_ROPE)
    cos_t = jnp.concatenate([cos_q] * MLA_HEADS, axis=1)
    sin_t = jnp.concatenate([sin_r] * MLA_HEADS, axis=1)
    q_ref[...] = ((qa * cos_t + qb * sin_t) * q_scale).astype(BF16)

    ckvn = (_rms(p[:, _C_CKV:_C_KRA], NORM_EPS) * kvg_ref[...]).astype(BF16)
    kn = _dot(ckvn, wkn_ref[...])
    kr = p[:, _C_KRA:_C_KRB] * cos_r + p[:, _C_KRB:_C_DQ] * sin_r
    k_ref[...] = (kn + jnp.concatenate([kr] * MLA_HEADS, axis=1)).astype(BF16)
    lane_t = lax.broadcasted_iota(I32, kn.shape, 1)
    ones_col = jnp.where(lane_t % HEAD_PAD == MLA_V, 1.0, 0.0)
    v_ref[...] = (_dot(ckvn, wv_ref[...]) + ones_col).astype(BF16)

    dq_ref[...] = (p[:, _C_DQ:_C_DK] * (LOG2E / math.sqrt(DIFF_HD))).astype(BF16)
    dk_ref[...] = p[:, _C_DK:_C_DV].astype(BF16)
    dv_ref[...] = p[:, _C_DV:_C_END].astype(BF16)


def _pre_attn(x2, mod3, pos_col, attn_pre_g, w_in, q_norm_g, kv_norm_g, w_uq, w_ukv, S):
    T, D = x2.shape
    tm = min(TM_PRE, S)
    tpb = S // tm
    f = lambda a: a.astype(BF16)
    d3 = DIFF_HEADS * DIFF_V
    a = MLA_Q_RANK
    b = a + MLA_KV_RANK
    c = b + MLA_ROPE

    def swap(r):
        hlf = MLA_ROPE // 2
        return jnp.concatenate([-r[..., hlf:], r[..., :hlf]], axis=-1)

    def pad_rope(r):
        return jnp.pad(r, ((0, 0), (MLA_NOPE, LANE - MLA_NOPE - MLA_ROPE)))

    w_kr = w_in[:, b:c]
    w1 = jnp.concatenate([w_in[:, :b], pad_rope(w_kr), pad_rope(swap(w_kr)), w_in[:, c:]], axis=1)
    assert w1.shape[1] == _C_END
    padq = HEAD_PAD - MLA_NOPE - MLA_ROPE
    wqa = jnp.pad(w_uq, ((0, 0), (0, 0), (0, padq))).reshape(MLA_Q_RANK, MLA_HEADS * HEAD_PAD)
    q_rope = w_uq[..., MLA_NOPE:]
    wqb = jnp.pad(swap(q_rope), ((0, 0), (0, 0), (MLA_NOPE, padq))).reshape(MLA_Q_RANK, MLA_HEADS * HEAD_PAD)
    wkn = jnp.pad(w_ukv[..., :MLA_NOPE], ((0, 0), (0, 0), (0, HEAD_PAD - MLA_NOPE))).reshape(
        MLA_KV_RANK, MLA_HEADS * HEAD_PAD)
    wv = jnp.pad(w_ukv[..., MLA_NOPE:], ((0, 0), (0, 0), (0, HEAD_PAD - MLA_V))).reshape(
        MLA_KV_RANK, MLA_HEADS * HEAD_PAD)

    inv = 1.0 / (ROPE_BASE ** (jnp.arange(0, MLA_ROPE, 2, dtype=F32) / MLA_ROPE))
    inv_lane = jnp.pad(jnp.concatenate([inv, inv]), (MLA_NOPE, LANE - MLA_NOPE - MLA_ROPE)).reshape(1, LANE)

    full = lambda arr: pl.BlockSpec(arr.shape, lambda i: (0,) * arr.ndim)
    row = lambda w: pl.BlockSpec((tm, w), lambda i: (i, 0))
    ins = [x2, mod3, pos_col, inv_lane, attn_pre_g.reshape(1, D), f(w1), q_norm_g.reshape(1, -1),
           kv_norm_g.reshape(1, -1), f(wqa), f(wqb), f(wkn), f(wv)]
    in_specs = [row(D), pl.BlockSpec((None, 6, D), lambda i: (i // tpb, 0, 0)), row(1)] + \
               [full(arr) for arr in ins[3:]]
    widths = [MLA_HEADS * HEAD_PAD, MLA_HEADS * HEAD_PAD, MLA_HEADS * HEAD_PAD, d3, d3, d3]
    return pl.pallas_call(
        _pre_attn_kernel,
        out_shape=[jax.ShapeDtypeStruct((T, w), BF16) for w in widths],
        grid=(T // tm,),
        in_specs=in_specs,
        out_specs=[row(w) for w in widths],
        compiler_params=_cparams("parallel"),
        name="pre_attn",
    )(*ins)


def _online_update(s, m, l, acc, v):
    m_new = jnp.maximum(m, jnp.max(s, axis=-1, keepdims=True))
    p = jnp.exp2(s - m_new)
    alpha = jnp.exp2(m - m_new)
    l_new = alpha * l + jnp.sum(p, axis=-1, keepdims=True)
    acc_new = alpha * acc + _dot(p.astype(BF16), v)
    return m_new, l_new, acc_new


MLA_HPS = 8


def _causal_blocks(i, tq, tk):
    nfull = (i * tq) // tk
    diag = [(nfull + j, i * tq - (nfull + j) * tk) for j in range(max(1, tq // tk))]
    return nfull, diag


def _keep(tq, tk, off):
    r = lax.broadcasted_iota(I32, (tq, tk), 0)
    c = lax.broadcasted_iota(I32, (tq, tk), 1)
    return r + off >= c


def _mla_attn_kernel(tk, q_ref, k_ref, v_ref, o_ref):
    i = pl.program_id(2)
    tq = q_ref.shape[0]
    sl = [slice(j * HEAD_PAD, (j + 1) * HEAD_PAD) for j in range(MLA_HPS)]
    qs = [q_ref[:, sl[j]] for j in range(MLA_HPS)]

    def step(kb, carry, off=None):
        r0 = pl.multiple_of(kb * tk, tk)
        out = []
        for j in range(MLA_HPS):
            m, acc = carry[j]
            s = _dot_nt(qs[j], k_ref[pl.ds(r0, tk), sl[j]])
            if off is not None:
                s = jnp.where(_keep(tq, tk, off), s, -jnp.inf)
            m_new = jnp.maximum(m, jnp.max(s, axis=-1, keepdims=True))
            p = jnp.exp2(s - m_new)
            acc = jnp.exp2(m - m_new) * acc + _dot(p.astype(BF16), v_ref[pl.ds(r0, tk), sl[j]])
            out.append((m_new, acc))
        return tuple(out)

    nfull, diag = _causal_blocks(i, tq, tk)
    init = tuple((jnp.full((tq, 1), -jnp.inf, F32), jnp.zeros((tq, HEAD_PAD), F32)) for _ in range(MLA_HPS))
    carry = lax.fori_loop(0, nfull, step, init)
    for kb, off in diag:
        carry = step(kb, carry, off)
    outs = [acc[:, :MLA_V] / acc[:, MLA_V:MLA_V + 1] for _, acc in carry]
    o_ref[...] = jnp.concatenate(outs, axis=1).astype(o_ref.dtype)


def _mla_attn(q, k, v, B, S):
    tq = min(TQ, S)
    tk = min(TK, S)
    q3 = q.reshape(B, S, -1)
    k3 = k.reshape(B, S, -1)
    v3 = v.reshape(B, S, -1)
    w = MLA_HPS * HEAD_PAD
    return pl.pallas_call(
        functools.partial(_mla_attn_kernel, tk),
        out_shape=jax.ShapeDtypeStruct((B, S, MLA_HEADS * MLA_V), BF16),
        grid=(B, MLA_HEADS // MLA_HPS, S // tq),
        in_specs=[pl.BlockSpec((None, tq, w), lambda b, h, i: (b, i, h)),
                  pl.BlockSpec((None, S, w), lambda b, h, i: (b, 0, h)),
                  pl.BlockSpec((None, S, w), lambda b, h, i: (b, 0, h))],
        out_specs=pl.BlockSpec((None, tq, MLA_HPS * MLA_V), lambda b, h, i: (b, i, h)),
        compiler_params=_cparams("parallel", "parallel", "arbitrary"),
        name="mla_attn",
    )(q3, k3, v3)


DIFF_HPS = 4


def _diff_attn_kernel(lambda_init, tk, q_ref, k_ref, v_ref, pc_ref, pr_ref, lam_ref, g_ref, o_ref):
    i = pl.program_id(2)
    tq = q_ref.shape[0]
    sl = [slice(j * DIFF_V, (j + 1) * DIFF_V) for j in range(DIFF_HPS)]
    lane = lax.broadcasted_iota(I32, (tq, DIFF_V), 1)
    qs, nslopes = [], []
    for j in range(DIFF_HPS):
        q = q_ref[:, sl[j]]
        zero = jnp.zeros_like(q)
        qs.append((jnp.where(lane < DIFF_HD, q, zero), jnp.where(lane >= DIFF_HD, q, zero)))
        hv = jnp.full((1, 1), pl.program_id(1) * DIFF_HPS + j, I32).astype(F32)
        nslopes.append(-LOG2E * jnp.exp2(-8.0 * (hv + 1.0) / DIFF_HEADS))
    pq = pc_ref[...]

    def step(kb, carry, off=None):
        r0 = pl.multiple_of(kb * tk, tk)
        dist = jnp.abs(pq - pr_ref[:, pl.ds(r0, tk)])
        keep = None if off is None else _keep(tq, tk, off)
        out = []
        for j in range(DIFF_HPS):
            kblk = k_ref[pl.ds(r0, tk), sl[j]]
            vblk = v_ref[pl.ds(r0, tk), sl[j]]
            bias = nslopes[j] * dist
            for c in range(2):
                s = _dot_nt(qs[j][c], kblk) + bias
                if keep is not None:
                    s = jnp.where(keep, s, -jnp.inf)
                out.append(_online_update(s, *carry[2 * j + c], vblk))
        return tuple(out)

    nfull, diag = _causal_blocks(i, tq, tk)
    init1 = (jnp.full((tq, 1), -jnp.inf, F32), jnp.zeros((tq, 1), F32), jnp.zeros((tq, DIFF_V), F32))
    carry = lax.fori_loop(0, nfull, step, (init1,) * (2 * DIFF_HPS))
    for kb, off in diag:
        carry = step(kb, carry, off)

    lv = lam_ref[...]
    lam = (jnp.exp(jnp.sum(lv[0:1] * lv[1:2], axis=-1, keepdims=True))
           - jnp.exp(jnp.sum(lv[2:3] * lv[3:4], axis=-1, keepdims=True)) + lambda_init)
    outs = []
    for j in range(DIFF_HPS):
        (_, l1, a1), (_, l2, a2) = carry[2 * j], carry[2 * j + 1]
        o = a1 / l1 - lam * (a2 / l2)
        outs.append(_rms(o, SUBLN_EPS) * g_ref[...] * (1.0 - lambda_init))
    o_ref[...] = jnp.concatenate(outs, axis=1).astype(o_ref.dtype)


def _diff_attn(dq, dk, dv, pos_col, pos_row, lam4, subln_g, lambda_init, B, S):
    tq = min(TQ, S)
    tk = min(TK, S)
    nq = S // tq
    q3 = dq.reshape(B, S, -1)
    k3 = dk.reshape(B, S, -1)
    v3 = dv.reshape(B, S, -1)
    w = DIFF_HPS * DIFF_V
    return pl.pallas_call(
        functools.partial(_diff_attn_kernel, lambda_init, tk),
        out_shape=jax.ShapeDtypeStruct((B, S, DIFF_HEADS * DIFF_V), BF16),
        grid=(B, DIFF_HEADS // DIFF_HPS, nq),
        in_specs=[pl.BlockSpec((None, tq, w), lambda b, h, i: (b, i, h)),
                  pl.BlockSpec((None, S, w), lambda b, h, i: (b, 0, h)),
                  pl.BlockSpec((None, S, w), lambda b, h, i: (b, 0, h)),
                  pl.BlockSpec((tq, 1), lambda b, h, i: (b * nq + i, 0)),
                  pl.BlockSpec((None, 1, S), lambda b, h, i: (b, 0, 0)),
                  pl.BlockSpec((4, DIFF_HD), lambda b, h, i: (0, 0)),
                  pl.BlockSpec((1, DIFF_V), lambda b, h, i: (0, 0))],
        out_specs=pl.BlockSpec((None, tq, w), lambda b, h, i: (b, i, h)),
        compiler_params=_cparams("parallel", "parallel", "arbitrary"),
        name="diff_attn",
    )(q3, k3, v3, pos_col, pos_row, lam4, subln_g.reshape(1, DIFF_V))


def _first_argmax(v, io, n, axis):
    m = jnp.max(v, axis=axis, keepdims=True)
    ix = jnp.min(jnp.where(v == m, io, n), axis=axis, keepdims=True)
    return m, ix


def _post_attn_kernel(x_ref, mla_ref, dif_ref, mod_ref, woa_ref, wob_ref, pg_ref, fg_ref,
                      rwt_ref, rb_ref,
                      x1_ref, h2_ref, idx_ref, wts_ref, rank_ref, cnt_ref, run_ref):
    step = pl.program_id(0)
    tm = x_ref.shape[0]
    E, G, GS = N_EXPERTS, N_GROUPS, GROUP_SIZE

    @pl.when(step == 0)
    def _():
        run_ref[...] = jnp.zeros_like(run_ref)

    g_a = mod_ref[2:3, :]
    sh_f = mod_ref[3:4, :]
    sc_f = mod_ref[4:5, :]
    y = _dot(mla_ref[...], woa_ref[...]) + _dot(dif_ref[...], wob_ref[...])
    x1 = x_ref[...] + g_a * (_rms(y, NORM_EPS) * pg_ref[...])
    x1_ref[...] = x1
    h2 = (_rms(x1, NORM_EPS) * fg_ref[...]) * (1.0 + sc_f) + sh_f
    _rt_store(h2_ref, _pack_rows(h2))

    logits = lax.dot_general(rwt_ref[...], h2, (((1,), (1,)), ((), ())),
                             preferred_element_type=F32, precision=lax.Precision.HIGHEST)
    scores = _sigmoid(logits)
    sel = scores + rb_ref[...]

    sio = lax.broadcasted_iota(I32, (GS, tm), 0)
    gs_rows = []
    for g in range(G):
        blk = sel[g * GS:(g + 1) * GS, :]
        m1, i1 = _first_argmax(blk, sio, GS, 0)
        m2 = jnp.max(jnp.where(sio == i1, -jnp.inf, blk), axis=0, keepdims=True)
        gs_rows.append(m1 + m2)
    gs = jnp.concatenate(gs_rows, axis=0)

    gio = lax.broadcasted_iota(I32, (G, tm), 0)
    gkeep = jnp.zeros((G, tm), F32)
    for _ in range(TOPK_GROUPS):
        _, ix = _first_argmax(gs, gio, G, 0)
        pick = gio == ix
        gkeep = jnp.where(pick, 1.0, gkeep)
        gs = jnp.where(pick, -jnp.inf, gs)
    ekeep = jnp.concatenate([jnp.broadcast_to(gkeep[g:g + 1, :], (GS, tm)) for g in range(G)], axis=0)
    cand = jnp.where(ekeep > 0.0, sel, -jnp.inf)

    eio = lax.broadcasted_iota(I32, (E, tm), 0)
    idx_rows, w_rows = [], []
    for _ in range(TOP_K):
        _, ix = _first_argmax(cand, eio, E, 0)
        pick = eio == ix
        w_rows.append(jnp.sum(jnp.where(pick, scores, 0.0), axis=0, keepdims=True))
        cand = jnp.where(pick, -jnp.inf, cand)
        idx_rows.append(ix)
    idx = jnp.concatenate(idx_rows, axis=0)
    w = jnp.concatenate(w_rows, axis=0)
    wts_ref[...] = w / jnp.sum(w, axis=0, keepdims=True) * ROUTED_SCALE
    idx_ref[...] = idx

    onehot = jnp.zeros((E, tm), F32)
    for k in range(TOP_K):
        onehot = onehot + jnp.where(eio == idx_rows[k], 1.0, 0.0)
    tr = lax.broadcasted_iota(I32, (tm, tm), 0)
    tc = lax.broadcasted_iota(I32, (tm, tm), 1)
    before = jnp.where(tr < tc, 1.0, 0.0).astype(BF16)
    prior = _dot(onehot.astype(BF16), before) + run_ref[...]
    rank_rows = [jnp.sum(jnp.where(eio == idx_rows[k], prior, 0.0), axis=0, keepdims=True)
                 for k in range(TOP_K)]
    rank_ref[...] = jnp.concatenate(rank_rows, axis=0).astype(I32)
    run_ref[...] += jnp.sum(onehot, axis=1, keepdims=True)
    cnt_ref[...] = run_ref[...].astype(I32)


def _post_attn(x2, mla, dif, mod3, w_o, attn_post_g, ffn_pre_g, router_w, router_b, S):
    T, D = x2.shape
    tm = min(TM_POST, S)
    tpb = S // tm
    half = MLA_HEADS * MLA_V
    woa = w_o[:half].astype(BF16)
    wob = w_o[half:].astype(BF16)
    rwt = router_w.T
    full = lambda arr: pl.BlockSpec(arr.shape, lambda i: (0,) * arr.ndim)
    row = lambda w: pl.BlockSpec((tm, w), lambda i: (i, 0))
    col = lambda r: pl.BlockSpec((r, tm), lambda i: (0, i))
    ins = [x2, mla.reshape(T, -1), dif.reshape(T, -1), mod3, woa, wob, attn_post_g.reshape(1, D),
           ffn_pre_g.reshape(1, D), rwt, router_b.reshape(N_EXPERTS, 1)]
    in_specs = [row(D), row(half), row(D - half), pl.BlockSpec((None, 6, D), lambda i: (i // tpb, 0, 0))] + \
               [full(arr) for arr in ins[4:]]
    return pl.pallas_call(
        _post_attn_kernel,
        out_shape=[jax.ShapeDtypeStruct((T, D), F32), jax.ShapeDtypeStruct((T * RT, LANE), I32),
                   jax.ShapeDtypeStruct((TOP_K, T), I32), jax.ShapeDtypeStruct((TOP_K, T), F32),
                   jax.ShapeDtypeStruct((TOP_K, T), I32), jax.ShapeDtypeStruct((N_EXPERTS, 1), I32)],
        grid=(T // tm,),
        in_specs=in_specs,
        out_specs=[row(D), pl.BlockSpec((tm * RT, LANE), lambda i: (i, 0)), col(TOP_K), col(TOP_K), col(TOP_K),
                   pl.BlockSpec((N_EXPERTS, 1), lambda i: (0, 0))],
        scratch_shapes=[pltpu.VMEM((N_EXPERTS, 1), F32)],
        compiler_params=_cparams("arbitrary"),
        name="post_attn",
    )(*ins)


def _dest_kernel(idx_ref, rank_ref, ps_ref, o_ref):
    idx = idx_ref[...]
    tm = idx.shape[1]
    eio = lax.broadcasted_iota(I32, (N_EXPERTS, tm), 0)
    ps = ps_ref[...]
    rows = [jnp.sum(jnp.where(eio == idx[k:k + 1, :], ps, 0.0), axis=0, keepdims=True)
            for k in range(TOP_K)]
    o_ref[...] = jnp.concatenate(rows, axis=0).astype(I32) + rank_ref[...]


def _dest(idx, rank, pstart):
    K, T = idx.shape
    tm = min(2048, T)
    col = pl.BlockSpec((K, tm), lambda i: (0, i))
    return pl.pallas_call(
        _dest_kernel,
        out_shape=jax.ShapeDtypeStruct((K, T), I32),
        grid=(T // tm,),
        in_specs=[col, col, pl.BlockSpec((N_EXPERTS, 1), lambda i: (0, 0))],
        out_specs=col,
        compiler_params=_cparams("parallel"),
        name="dest",
    )(idx, rank, pstart.astype(F32).reshape(N_EXPERTS, 1))


def _dispatch_kernel(bs_ref, cnt_ref, dest_ref, h_ref, xs_ref, zeros, sem, zsem):
    tm = h_ref.shape[0] // RT
    blk = BM * RT
    nb = xs_ref.shape[0] // blk

    @pl.when(pl.program_id(0) == 0)
    def _():
        zeros[...] = jnp.zeros_like(zeros)

        def zcopy(g):
            return pltpu.make_async_copy(zeros, xs_ref.at[pl.ds(pl.multiple_of(g * blk, blk), blk), :], zsem)

        def has_pad(e):
            return cnt_ref[e] % BM != 0

        def fill(e, _):
            @pl.when(has_pad(e))
            def _():
                zcopy(bs_ref[e + 1] - 1).start()
            return 0

        def drain(e, _):
            @pl.when(has_pad(e))
            def _():
                zcopy(0).wait()
            return 0

        lax.fori_loop(0, N_EXPERTS, fill, 0)
        lax.fori_loop(bs_ref[N_EXPERTS], nb, lambda g, _: (zcopy(g).start(), 0)[1], 0)
        lax.fori_loop(0, N_EXPERTS, drain, 0)
        lax.fori_loop(bs_ref[N_EXPERTS], nb, lambda g, _: (zcopy(0).wait(), 0)[1], 0)

    def tile(ref, r):
        return ref.at[pl.ds(pl.multiple_of(r * RT, RT), RT), :]

    def issue(t, _):
        for k in range(TOP_K):
            pltpu.make_async_copy(tile(h_ref, t), tile(xs_ref, dest_ref[k, t]), sem).start(priority=k % 2)
        return 0

    lax.fori_loop(0, tm, issue, 0)
    for _ in range(TOP_K):
        pltpu.make_async_copy(h_ref, xs_ref.at[pl.ds(0, tm * RT), :], sem).wait()


def _dispatch(h2rt, dest, bstart, cnt, P):
    T = h2rt.shape[0] // RT
    tm = min(TM_DISP, T)
    nt = T // tm
    dest3 = dest.reshape(TOP_K, nt, tm).transpose(1, 0, 2)
    return pl.pallas_call(
        _dispatch_kernel,
        out_shape=jax.ShapeDtypeStruct((P * RT, LANE), I32),
        grid_spec=pltpu.PrefetchScalarGridSpec(
            num_scalar_prefetch=2,
            grid=(nt,),
            in_specs=[pl.BlockSpec((None, TOP_K, tm), lambda i, bs, cn: (i, 0, 0), memory_space=pltpu.SMEM),
                      pl.BlockSpec((tm * RT, LANE), lambda i, bs, cn: (i, 0))],
            out_specs=pl.BlockSpec(memory_space=pl.ANY),
            scratch_shapes=[pltpu.VMEM((BM * RT, LANE), I32), pltpu.SemaphoreType.DMA(()),
                            pltpu.SemaphoreType.DMA(())],
        ),
        compiler_params=_cparams("arbitrary"),
        name="dispatch",
    )(bstart, cnt, dest3, h2rt)


NBUF_X = 16
X_AHEAD = NBUF_X - 2
NBUF_Y = 8
EPS = 2


def _experts_kernel(bs_ref, wg_ref, wu_ref, wd_ref, xs_ref, ys_ref, wgb, wub, wdb, xbuf, ybuf, xsem, ysem):
    s = pl.program_id(0)
    blk = BM * RT
    nb = xs_ref.shape[0] // blk
    nused = bs_ref[N_EXPERTS]

    def rows(g):
        return pl.ds(pl.multiple_of(g * blk, blk), blk)

    def x_copy(g, slot):
        return pltpu.make_async_copy(xs_ref.at[rows(g), :], xbuf.at[slot], xsem.at[slot])

    def y_copy(g, slot):
        return pltpu.make_async_copy(ybuf.at[slot], ys_ref.at[rows(g), :], ysem.at[slot])

    @pl.when(s == 0)
    def _():
        for j in range(X_AHEAD):
            @pl.when(j < nused)
            def _():
                x_copy(j, j).start()

    def step(gs):
        for g in gs:
            x_copy(g, g % NBUF_X).wait()
            nxt = g + X_AHEAD

            @pl.when(nxt < nused)
            def _():
                x_copy(nxt, nxt % NBUF_X).start()

        x = _unpack_rows(jnp.concatenate([_rt_load(xbuf, BM, g % NBUF_X) for g in gs], axis=0), BF16)
        gate = _dot(x, wgb[...])
        up = _dot(x, wub[...])
        a = (gate * _sigmoid(gate)) * up
        y = _dot(a.astype(BF16), wdb[...])
        for j, g in enumerate(gs):
            ys = g % NBUF_Y

            @pl.when(g >= NBUF_Y)
            def _():
                y_copy(g - NBUF_Y, ys).wait()

            _rt_store(ybuf, _pack_rows(y[j * BM:(j + 1) * BM]), ys)
            y_copy(g, ys).start()

    def expert(sub):
        e = s * EPS + sub
        g0 = bs_ref[e]
        g1 = bs_ref[e + 1]

        @pl.when(g1 > g0)
        def _():
            wgb[...] = wg_ref[sub].astype(BF16)
            wub[...] = wu_ref[sub].astype(BF16)
            wdb[...] = wd_ref[sub].astype(BF16)

            def pair(i, _):
                g = g0 + 2 * i
                step([g, g + 1])
                return 0

            lax.fori_loop(0, (g1 - g0) // 2, pair, 0)

            @pl.when((g1 - g0) % 2 == 1)
            def _():
                step([g1 - 1])

    for sub in range(EPS):
        expert(sub)

    @pl.when(s == pl.num_programs(0) - 1)
    def _():
        for j in range(NBUF_Y):
            @pl.when(nused - 1 - j >= 0)
            def _():
                y_copy(0, (nused - 1 - j) % NBUF_Y).wait()
        ybuf[0] = jnp.zeros(ybuf.shape[1:], I32)
        lax.fori_loop(nused, nb, lambda g, _: (y_copy(g, 0).start(), 0)[1], 0)
        lax.fori_loop(nused, nb, lambda g, _: (y_copy(0, 0).wait(), 0)[1], 0)


def _experts(xs, bstart, w_gate, w_up, w_down):
    E, D, F = w_gate.shape
    return pl.pallas_call(
        _experts_kernel,
        out_shape=jax.ShapeDtypeStruct(xs.shape, I32),
        grid_spec=pltpu.PrefetchScalarGridSpec(
            num_scalar_prefetch=1,
            grid=(E // EPS,),
            in_specs=[pl.BlockSpec((EPS, D, F), lambda s, bs: (s, 0, 0)),
                      pl.BlockSpec((EPS, D, F), lambda s, bs: (s, 0, 0)),
                      pl.BlockSpec((EPS, F, D), lambda s, bs: (s, 0, 0)),
                      pl.BlockSpec(memory_space=pl.ANY)],
            out_specs=pl.BlockSpec(memory_space=pl.ANY),
            scratch_shapes=[pltpu.VMEM((D, F), BF16), pltpu.VMEM((D, F), BF16), pltpu.VMEM((F, D), BF16),
                            pltpu.VMEM((NBUF_X, BM * RT, LANE), I32), pltpu.VMEM((NBUF_Y, BM * RT, LANE), I32),
                            pltpu.SemaphoreType.DMA((NBUF_X,)), pltpu.SemaphoreType.DMA((NBUF_Y,))],
        ),
        compiler_params=_cparams("arbitrary"),
        name="experts",
    )(bstart, w_gate, w_up, w_down, xs)


def _combine_kernel(dcur_ref, dnxt_ref, w_ref, x1_ref, h_ref, mod_ref, pg_ref, sg_ref, su_ref, sd_ref,
                    ys_ref, o_ref, rows_a, rows_b, sem):
    i = pl.program_id(0)
    n = pl.num_programs(0)
    tm = x1_ref.shape[0]

    def tile(r):
        return pl.ds(pl.multiple_of(r * RT, RT), RT)

    def request(dref, buf, s, t):
        for k in range(TOP_K):
            pltpu.make_async_copy(ys_ref.at[tile(dref[k, t]), :], buf.at[k, tile(t), :],
                                  sem.at[s]).start(priority=k % 2)

    def drain(buf, s):
        for k in range(TOP_K):
            pltpu.make_async_copy(ys_ref.at[pl.ds(0, tm * RT), :], buf.at[k], sem.at[s]).wait()

    @pl.when(i == 0)
    def _():
        lax.fori_loop(0, tm, lambda t, _: (request(dcur_ref, rows_a, 0, t), 0)[1], 0)

    def step(cur, s_cur, nxt, s_nxt):
        drain(cur, s_cur)
        for t in range(tm):
            request(dnxt_ref, nxt, s_nxt, t)
        w = w_ref[...]
        routed = w[:, 0:1] * _unpack_rows(_rt_load(cur, tm, 0), F32)
        for k in range(1, TOP_K):
            routed = routed + w[:, k:k + 1] * _unpack_rows(_rt_load(cur, tm, k), F32)
        hb = _unpack_rows(_rt_load(h_ref, tm), BF16)
        g = _dot(hb, sg_ref[...])
        u = _dot(hb, su_ref[...])
        shared = _dot(((g * _sigmoid(g)) * u).astype(BF16), sd_ref[...])
        y = routed + shared
        g_f = mod_ref[5:6, :]
        o_ref[...] = x1_ref[...] + g_f * (_rms(y, NORM_EPS) * pg_ref[...])

    even = i % 2 == 0

    @pl.when(even)
    def _():
        step(rows_a, 0, rows_b, 1)

    @pl.when(jnp.logical_not(even))
    def _():
        step(rows_b, 1, rows_a, 0)

    @pl.when(i == n - 1)
    def _():
        @pl.when(even)
        def _():
            drain(rows_b, 1)

        @pl.when(jnp.logical_not(even))
        def _():
            drain(rows_a, 0)


def _combine(dest, wts, x1, h2, mod3, ffn_post_g, sw_gate, sw_up, sw_down, ys, S):
    T, D = x1.shape
    tm = min(TM_COMB, S)
    nt = T // tm
    tpb = S // tm
    dest3 = dest.reshape(TOP_K, nt, tm).transpose(1, 0, 2)
    w_tk = wts.T
    full = lambda arr: pl.BlockSpec(arr.shape, lambda i: (0,) * arr.ndim)
    row = lambda w: pl.BlockSpec((tm, w), lambda i: (i, 0))
    sg, su, sd = sw_gate.astype(BF16), sw_up.astype(BF16), sw_down.astype(BF16)
    pg = ffn_post_g.reshape(1, D)
    return pl.pallas_call(
        _combine_kernel,
        out_shape=jax.ShapeDtypeStruct((T, D), F32),
        grid=(nt,),
        in_specs=[pl.BlockSpec((None, TOP_K, tm), lambda i: (i, 0, 0), memory_space=pltpu.SMEM),
                  pl.BlockSpec((None, TOP_K, tm), lambda i: (jnp.minimum(i + 1, nt - 1), 0, 0),
                               memory_space=pltpu.SMEM),
                  row(TOP_K), row(D), pl.BlockSpec((tm * RT, LANE), lambda i: (i, 0)),
                  pl.BlockSpec((None, 6, D), lambda i: (i // tpb, 0, 0)),
                  full(pg), full(sg), full(su), full(sd),
                  pl.BlockSpec(memory_space=pl.ANY)],
        out_specs=row(D),
        scratch_shapes=[pltpu.VMEM((TOP_K, tm * RT, LANE), I32), pltpu.VMEM((TOP_K, tm * RT, LANE), I32),
                        pltpu.SemaphoreType.DMA((2,))],
        compiler_params=_cparams("arbitrary"),
        name="combine",
    )(dest3, dest3, w_tk, x1, h2, mod3, pg, sg, su, sd, ys)


def _moe(h2, x1, idx, wts, rank, counts, mod3, ffn_post_g, exp_w_gate, exp_w_up, exp_w_down,
         sw_gate, sw_up, sw_down, S):
    T, D = x1.shape
    A = T * TOP_K
    P = A + N_EXPERTS * BM
    cnt = counts.reshape(N_EXPERTS)
    blocks = (cnt + BM - 1) // BM
    bstart = jnp.concatenate([jnp.zeros((1,), I32), jnp.cumsum(blocks).astype(I32)])
    pstart = bstart[:-1] * BM

    dest = _dest(idx, rank, pstart)
    xs = _dispatch(h2, dest, bstart, cnt, P)
    ys = _experts(xs, bstart, exp_w_gate, exp_w_up, exp_w_down)
    return _combine(dest, wts, x1, h2, mod3, ffn_post_g, sw_gate, sw_up, sw_down, ys, S)


def _layer(x, c, positions, lambda_init, w_ada, b_ada, attn_pre_g, attn_post_g, w_in, q_norm_g, kv_norm_g,
           w_uq, w_ukv, lam_q1, lam_k1, lam_q2, lam_k2, diff_subln_g, w_o, ffn_pre_g, ffn_post_g,
           router_w, router_b, exp_w_gate, exp_w_up, exp_w_down, sw_gate, sw_up, sw_down):
    B, S, D = x.shape
    T = B * S
    x2 = x.reshape(T, D)
    posf = positions.astype(F32)
    pos_col = posf.reshape(T, 1)
    pos_row = posf.reshape(B, 1, S)

    mod3 = _ada(c, w_ada, b_ada).reshape(B, 6, D)
    q, k, v, dq, dk, dv = _pre_attn(x2, mod3, pos_col, attn_pre_g, w_in, q_norm_g, kv_norm_g, w_uq, w_ukv, S)
    mla = _mla_attn(q, k, v, B, S)
    lam4 = jnp.stack([lam_q1, lam_k1, lam_q2, lam_k2])
    dif = _diff_attn(dq, dk, dv, pos_col, pos_row, lam4, diff_subln_g, lambda_init, B, S)
    x1, h2, idx, wts, rank, counts = _post_attn(x2, mla, dif, mod3, w_o, attn_post_g, ffn_pre_g,
                                                router_w, router_b, S)
    out = _moe(h2, x1, idx, wts, rank, counts, mod3, ffn_post_g, exp_w_gate, exp_w_up, exp_w_down,
               sw_gate, sw_up, sw_down, S)
    return out.reshape(B, S, D)


def kernel(x, c, positions, w_ada, b_ada, attn_pre_g, attn_post_g, w_in, q_norm_g, kv_norm_g, w_uq, w_ukv,
           lam_q1, lam_k1, lam_q2, lam_k2, diff_subln_g, w_o, ffn_pre_g, ffn_post_g, router_w, router_b,
           exp_w_gate, exp_w_up, exp_w_down, shared_w_gate, shared_w_up, shared_w_down):
    depth = w_ada.shape[0]
    for l in range(depth):
        lambda_init = 0.8 - 0.6 * math.exp(-0.3 * l)
        x = _layer(x, c, positions, lambda_init, w_ada[l], b_ada[l], attn_pre_g[l], attn_post_g[l], w_in[l],
                   q_norm_g[l], kv_norm_g[l], w_uq[l], w_ukv[l], lam_q1[l], lam_k1[l], lam_q2[l], lam_k2[l],
                   diff_subln_g[l], w_o[l], ffn_pre_g[l], ffn_post_g[l], router_w[l], router_b[l],
                   exp_w_gate[l], exp_w_up[l], exp_w_down[l], shared_w_gate[l], shared_w_up[l],
                   shared_w_down[l])
    return x
```
